```python
import jax, jax.numpy as jnp
from jax import lax
import numpy as np

D_MODEL = 2048
BATCH = 1
SEQ = 8192
DEPTH = 1
DEC_BATCH = 4
DEC_SEQ = 8192
PAST_LEN = 128

MIX_WIDTH = D_MODEL
FOURIER_WIDTH = MIX_WIDTH // 2
N_FOURIER_GROUPS = 4
FOURIER_GROUP_DIM = FOURIER_WIDTH // N_FOURIER_GROUPS
MLA_WIDTH = MIX_WIDTH - FOURIER_WIDTH
V_HEAD_DIM = 128
N_HEADS = MLA_WIDTH // V_HEAD_DIM
QK_NOPE_DIM = 128
QK_ROPE_DIM = 64
QK_HEAD_DIM = QK_NOPE_DIM + QK_ROPE_DIM
Q_LORA_RANK = D_MODEL // 4
KV_LORA_RANK = D_MODEL // 8
IN_WIDTH = FOURIER_WIDTH + Q_LORA_RANK + KV_LORA_RANK + QK_ROPE_DIM
ROPE_THETA = 10000.0
Q_BLOCK = 128
N_EXPERT_GROUPS = 4
EXPERTS_PER_GROUP = 8
N_EXPERTS = N_EXPERT_GROUPS * EXPERTS_PER_GROUP
TOP_K = 2
EXPERT_DIM = D_MODEL // 4
MOE_BLOCK = 128
EPS = 1e-6

kernel_name = 'hymba_fnet_mla_hier_moe_encoder'


def rms_norm(x, g):
    xf = x.astype(jnp.float32)
    y = xf * lax.rsqrt(jnp.mean(xf * xf, axis=-1, keepdims=True) + EPS)
    return (y * g.astype(jnp.float32)).astype(x.dtype)


def rope_tables(S):
    pos = jnp.arange(S, dtype=jnp.float32)
    inv_freq = 1.0 / (ROPE_THETA ** (jnp.arange(0, QK_ROPE_DIM, 2, dtype=jnp.float32) / QK_ROPE_DIM))
    ang = pos[:, None] * inv_freq[None, :]
    return jnp.cos(ang), jnp.sin(ang)


def apply_rope(x, cos, sin):
    xf = x.astype(jnp.float32)
    x1, x2 = jnp.split(xf, 2, axis=-1)
    return jnp.concatenate([x1 * cos - x2 * sin, x2 * cos + x1 * sin], axis=-1).astype(x.dtype)


def fourier_mix(u):
    B, S, _ = u.shape
    uf = u.astype(jnp.float32).reshape(B, S, N_FOURIER_GROUPS, FOURIER_GROUP_DIM)
    y = jnp.fft.fftn(uf, axes=(1, 3), norm='ortho').real
    return y.reshape(B, S, FOURIER_WIDTH).astype(u.dtype)


def mla_attention(q_nope, q_rope, k_nope, k_rope, v):
    B, S, H, _ = q_nope.shape
    nq = S // Q_BLOCK
    scale = QK_HEAD_DIM ** -0.5

    def to_blocks(t):
        return jnp.moveaxis(t.reshape((B, nq, Q_BLOCK) + t.shape[2:]), 1, 0)

    def block(args):
        qn, qr = args
        s = (jnp.einsum('bqhd,bkhd->bhqk', qn, k_nope, preferred_element_type=jnp.float32)
             + jnp.einsum('bqhr,bkr->bhqk', qr, k_rope, preferred_element_type=jnp.float32))
        p = jax.nn.softmax(s * scale, axis=-1)
        return jnp.einsum('bhqk,bkhd->bqhd', p.astype(v.dtype), v)

    o = lax.map(block, (to_blocks(q_nope), to_blocks(q_rope)))
    return jnp.moveaxis(o, 0, 1).reshape(B, S, H * V_HEAD_DIM)


def hierarchical_moe(x, w_rg, b_rg, w_re, b_re, w_gate, w_up, w_down):
    B, S, D = x.shape
    T = B * S
    xt = x.reshape(T, D)
    g_logits = (xt @ w_rg).astype(jnp.float32) + b_rg.astype(jnp.float32)
    g_prob = jax.nn.softmax(g_logits, axis=-1)
    g_sel = jnp.argmax(g_logits, axis=-1).astype(jnp.int32)
    g_p = jnp.take_along_axis(g_prob, g_sel[:, None], axis=-1)
    e_logits = ((xt @ w_re).astype(jnp.float32) + b_re.astype(jnp.float32)).reshape(T, N_EXPERT_GROUPS, EXPERTS_PER_GROUP)
    e_logits = jnp.take_along_axis(e_logits, g_sel[:, None, None], axis=1)[:, 0]
    e_prob = jax.nn.softmax(e_logits, axis=-1)
    top_p, top_i = lax.top_k(e_prob, TOP_K)
    weights = top_p / jnp.sum(top_p, axis=-1, keepdims=True) * g_p
    expert_id = g_sel[:, None] * EXPERTS_PER_GROUP + top_i.astype(jnp.int32)

    TK = T * TOP_K
    flat_e = expert_id.reshape(-1)
    flat_w = weights.reshape(-1)
    flat_tok = jnp.repeat(jnp.arange(T, dtype=jnp.int32), TOP_K)
    counts = jnp.bincount(flat_e, length=N_EXPERTS)
    padded = (counts + MOE_BLOCK - 1) // MOE_BLOCK * MOE_BLOCK
    pad_end = jnp.cumsum(padded)
    pad_start = pad_end - padded
    start = jnp.cumsum(counts) - counts
    order = jnp.argsort(flat_e)
    se = flat_e[order]
    dest = pad_start[se] + jnp.arange(TK, dtype=jnp.int32) - start[se]
    n_blocks = (TK + N_EXPERTS * (MOE_BLOCK - 1) + MOE_BLOCK - 1) // MOE_BLOCK
    P = n_blocks * MOE_BLOCK
    slot_tok = jnp.full((P,), T, dtype=jnp.int32).at[dest].set(flat_tok[order])
    slot_w = jnp.zeros((P,), jnp.float32).at[dest].set(flat_w[order])
    block_start = jnp.arange(n_blocks, dtype=jnp.int32) * MOE_BLOCK
    block_expert = jnp.minimum(jnp.searchsorted(pad_end, block_start, side='right'), N_EXPERTS - 1).astype(jnp.int32)
    x_pad = jnp.concatenate([xt, jnp.zeros((1, D), xt.dtype)], axis=0)
    xb = x_pad[slot_tok].reshape(n_blocks, MOE_BLOCK, D)

    def expert_block(args):
        xblk, e = args
        h = jax.nn.silu(xblk @ w_gate[e]) * (xblk @ w_up[e])
        return h @ w_down[e]

    yb = lax.map(expert_block, (xb, block_expert)).reshape(P, D)
    yb = yb * slot_w[:, None].astype(yb.dtype)
    out = jnp.zeros((T + 1, D), yb.dtype).at[slot_tok].add(yb)[:T]
    return out.reshape(B, S, D)


def encoder_layer(x, cos, sin, g_attn_norm, w_in, g_q_latent, w_uq, g_kv_latent, w_ukv,
                  g_out_fourier, g_out_mla, w_out, g_ffn_norm, w_router_group, b_router_group,
                  w_router_expert, b_router_expert, w_gate, w_up, w_down):
    B, S, _ = x.shape
    u = rms_norm(x, g_attn_norm)
    z = u @ w_in
    s1 = FOURIER_WIDTH
    s2 = s1 + Q_LORA_RANK
    s3 = s2 + KV_LORA_RANK
    z_f, q_lat, kv_lat, k_rope = z[..., :s1], z[..., s1:s2], z[..., s2:s3], z[..., s3:]
    f_out = fourier_mix(z_f)
    q = (rms_norm(q_lat, g_q_latent) @ w_uq).reshape(B, S, N_HEADS, QK_HEAD_DIM)
    q_nope, q_rope = q[..., :QK_NOPE_DIM], q[..., QK_NOPE_DIM:]
    kv = (rms_norm(kv_lat, g_kv_latent) @ w_ukv).reshape(B, S, N_HEADS, QK_NOPE_DIM + V_HEAD_DIM)
    k_nope, v = kv[..., :QK_NOPE_DIM], kv[..., QK_NOPE_DIM:]
    q_rope = apply_rope(q_rope, cos[:, None, :], sin[:, None, :])
    k_rope = apply_rope(k_rope, cos, sin)
    a_out = mla_attention(q_nope, q_rope, k_nope, k_rope, v)
    merged = jnp.concatenate([rms_norm(f_out, g_out_fourier), rms_norm(a_out, g_out_mla)], axis=-1)
    h = x + merged @ w_out
    h = h + hierarchical_moe(rms_norm(h, g_ffn_norm), w_router_group, b_router_group,
                             w_router_expert, b_router_expert, w_gate, w_up, w_down)
    return h


def setup_inputs(seed: int = 0) -> dict:
    key = jax.random.key(seed)
    ks = jax.random.split(key, 24)
    f32 = jnp.float32

    def nrm(k, shape, scale):
        return jax.random.normal(k, shape, f32) * scale

    def gain(k, n):
        return jnp.ones((DEPTH, n), f32) + 0.02 * jax.random.normal(k, (DEPTH, n), f32)

    return {
        'x_prompt': jax.random.normal(ks[0], (BATCH, SEQ, D_MODEL), f32),
        'x_sample': jax.random.normal(ks[1], (DEC_BATCH, DEC_SEQ, D_MODEL), f32),
        'g_attn_norm': gain(ks[2], D_MODEL),
        'w_in': nrm(ks[3], (DEPTH, D_MODEL, IN_WIDTH), D_MODEL ** -0.5),
        'g_q_latent': gain(ks[4], Q_LORA_RANK),
        'w_uq': nrm(ks[5], (DEPTH, Q_LORA_RANK, N_HEADS * QK_HEAD_DIM), Q_LORA_RANK ** -0.5),
        'g_kv_latent': gain(ks[6], KV_LORA_RANK),
        'w_ukv': nrm(ks[7], (DEPTH, KV_LORA_RANK, N_HEADS * (QK_NOPE_DIM + V_HEAD_DIM)), KV_LORA_RANK ** -0.5),
        'g_out_fourier': gain(ks[8], FOURIER_WIDTH),
        'g_out_mla': gain(ks[9], MLA_WIDTH),
        'w_out': nrm(ks[10], (DEPTH, MIX_WIDTH, D_MODEL), MIX_WIDTH ** -0.5),
        'g_ffn_norm': gain(ks[11], D_MODEL),
        'w_router_group': nrm(ks[12], (DEPTH, D_MODEL, N_EXPERT_GROUPS), D_MODEL ** -0.5),
        'b_router_group': nrm(ks[13], (DEPTH, N_EXPERT_GROUPS), 0.01),
        'w_router_expert': nrm(ks[14], (DEPTH, D_MODEL, N_EXPERTS), D_MODEL ** -0.5),
        'b_router_expert': nrm(ks[15], (DEPTH, N_EXPERTS), 0.01),
        'w_gate': nrm(ks[16], (DEPTH, N_EXPERTS, D_MODEL, EXPERT_DIM), D_MODEL ** -0.5),
        'w_up': nrm(ks[17], (DEPTH, N_EXPERTS, D_MODEL, EXPERT_DIM), D_MODEL ** -0.5),
        'w_down': nrm(ks[18], (DEPTH, N_EXPERTS, EXPERT_DIM, D_MODEL), EXPERT_DIM ** -0.5),
        'g_final': jnp.ones((D_MODEL,), f32) + 0.02 * jax.random.normal(ks[19], (D_MODEL,), f32),
    }


def reference(x_prompt, x_sample, g_attn_norm, w_in, g_q_latent, w_uq, g_kv_latent, w_ukv,
              g_out_fourier, g_out_mla, w_out, g_ffn_norm, w_router_group, b_router_group,
              w_router_expert, b_router_expert, w_gate, w_up, w_down, g_final):
    def run(x):
        cos, sin = rope_tables(x.shape[1])
        h = x
        for l in range(DEPTH):
            h = encoder_layer(h, cos, sin, g_attn_norm[l], w_in[l], g_q_latent[l], w_uq[l],
                              g_kv_latent[l], w_ukv[l], g_out_fourier[l], g_out_mla[l], w_out[l],
                              g_ffn_norm[l], w_router_group[l], b_router_group[l],
                              w_router_expert[l], b_router_expert[l], w_gate[l], w_up[l], w_down[l])
        return rms_norm(h, g_final)

    y_prompt = run(x_prompt)
    y_sample = run(x_sample)
    return (y_prompt, y_sample)
```

```python
import functools
import math

import jax
import jax.numpy as jnp
import numpy as np
from jax import lax
from jax.experimental import pallas as pl
from jax.experimental.pallas import tpu as pltpu

D_MODEL = 2048
FOURIER_WIDTH = 1024
N_FOURIER_GROUPS = 4
FOURIER_GROUP_DIM = 256
MLA_WIDTH = 1024
V_HEAD_DIM = 128
N_HEADS = 8
QK_NOPE_DIM = 128
QK_ROPE_DIM = 64
QK_HEAD_DIM = 192
Q_LORA_RANK = 512
KV_LORA_RANK = 256
ROPE_THETA = 10000.0
N_EXPERT_GROUPS = 4
EXPERTS_PER_GROUP = 8
N_EXPERTS = 32
TOP_K = 2
EXPERT_DIM = 512
EPS = 1e-6

LANES = 128
QK_PAD = 256
ROW_TILES = D_MODEL // LANES

DFT_N1 = 64
DFT_N2 = 128

TOKEN_TILE = 512
ATTN_TQ = 512
ATTN_TK = 512
EXPERT_TILE = 256
VMEM_LIMIT = 56 * 1024 * 1024

F32 = jnp.float32
BF16 = jnp.bfloat16


def _const_spec(shape):
    nd = len(shape)
    return pl.BlockSpec(shape, lambda *_: (0,) * nd, pipeline_mode=pl.Buffered(1))


def _rms(x, g):
    return x * lax.rsqrt(jnp.mean(x * x, axis=-1, keepdims=True) + EPS) * g


def _proj_kernel(x_ref, g_attn_ref, w_in_ref, cs_ref, g_q_ref, g_kv_ref, wqn_ref, wqr_ref, wqrot_ref,
                 wuk_ref, wuvt_ref, cos_t_ref, sin_t_ref, cos_p_ref, sin_p_ref,
                 xr_ref, xi_ref, qt_ref, kx_ref, vt_ref):
    tm = x_ref.shape[1]
    x = x_ref[0]
    u = _rms(x, g_attn_ref[...]).astype(BF16)
    z = jnp.dot(u, w_in_ref[...], preferred_element_type=F32)

    zf = z[:, :FOURIER_WIDTH].astype(BF16)
    xr, xi = [], []
    for g in range(N_FOURIER_GROUPS):
        zg = zf[:, g * FOURIER_GROUP_DIM:(g + 1) * FOURIER_GROUP_DIM]
        xg = jnp.dot(zg, cs_ref[...], preferred_element_type=F32)
        xr.append(xg[:, :FOURIER_GROUP_DIM])
        xi.append(xg[:, FOURIER_GROUP_DIM:])
    xr_ref[0] = jnp.concatenate(xr, axis=-1).astype(BF16)
    xi_ref[0] = jnp.concatenate(xi, axis=-1).astype(BF16)

    o = FOURIER_WIDTH
    q_lat = z[:, o:o + Q_LORA_RANK]
    o += Q_LORA_RANK
    kv_lat = z[:, o:o + KV_LORA_RANK]
    o += KV_LORA_RANK
    kr = z[:, o:o + LANES]
    kr_rot = z[:, o + LANES:o + 2 * LANES]

    qn = _rms(q_lat, g_q_ref[...]).astype(BF16)
    kvn = _rms(kv_lat, g_kv_ref[...]).astype(BF16)

    nt = (((1,), (1,)), ((), ()))
    qn_t = lax.dot_general(wqn_ref[...], qn, nt, preferred_element_type=F32)
    qr_t = lax.dot_general(wqr_ref[...], qn, nt, preferred_element_type=F32)
    qrot_t = lax.dot_general(wqrot_ref[...], qn, nt, preferred_element_type=F32)
    cos_t = cos_t_ref[...]
    sin_t = sin_t_ref[...]
    zpad = jnp.zeros((QK_PAD - QK_HEAD_DIM, tm), BF16)
    for h in range(N_HEADS):
        r0 = h * QK_PAD
        qt_ref[0, r0:r0 + QK_NOPE_DIM, :] = qn_t[h * QK_NOPE_DIM:(h + 1) * QK_NOPE_DIM].astype(BF16)
        sl = slice(h * QK_ROPE_DIM, (h + 1) * QK_ROPE_DIM)
        roped = qr_t[sl] * cos_t + qrot_t[sl] * sin_t
        qt_ref[0, r0 + QK_NOPE_DIM:r0 + QK_HEAD_DIM, :] = roped.astype(BF16)
        qt_ref[0, r0 + QK_HEAD_DIM:r0 + QK_PAD, :] = zpad

    k_nope = jnp.dot(kvn, wuk_ref[...], preferred_element_type=F32)
    k_rope = (kr * cos_p_ref[...] + kr_rot * sin_p_ref[...]).astype(BF16)
    for h in range(N_HEADS):
        kx_ref[0, h, :, :QK_NOPE_DIM] = k_nope[:, h * QK_NOPE_DIM:(h + 1) * QK_NOPE_DIM].astype(BF16)
        kx_ref[0, h, :, QK_NOPE_DIM:] = k_rope

    v_t = lax.dot_general(wuvt_ref[...], kvn, nt, preferred_element_type=F32)
    vt_ref[0, 0] = v_t.astype(BF16).reshape(N_HEADS, V_HEAD_DIM, tm)


def _proj(x, w):
    B, S, _ = x.shape
    tm = ATTN_TK
    grid = (B, S // tm)
    tok = lambda b, i: (b, i, 0)
    out_shape = [
        jax.ShapeDtypeStruct((B, S, FOURIER_WIDTH), BF16),
        jax.ShapeDtypeStruct((B, S, FOURIER_WIDTH), BF16),
        jax.ShapeDtypeStruct((B, N_HEADS * QK_PAD, S), BF16),
        jax.ShapeDtypeStruct((B, N_HEADS, S, QK_PAD), BF16),
        jax.ShapeDtypeStruct((B, S // tm, N_HEADS, V_HEAD_DIM, tm), BF16),
    ]
    in_specs = [
        pl.BlockSpec((1, tm, D_MODEL), tok),
        _const_spec((1, D_MODEL)),
        _const_spec(w['w_in'].shape),
        _const_spec(w['cs'].shape),
        _const_spec((1, Q_LORA_RANK)),
        _const_spec((1, KV_LORA_RANK)),
        _const_spec(w['wqn_t'].shape),
        _const_spec(w['wqr_t'].shape),
        _const_spec(w['wqrot_t'].shape),
        _const_spec(w['wuk'].shape),
        _const_spec(w['wuv_t'].shape),
        pl.BlockSpec((QK_ROPE_DIM, tm), lambda b, i: (0, i)),
        pl.BlockSpec((QK_ROPE_DIM, tm), lambda b, i: (0, i)),
        pl.BlockSpec((tm, LANES), lambda b, i: (i, 0)),
        pl.BlockSpec((tm, LANES), lambda b, i: (i, 0)),
    ]
    out_specs = [
        pl.BlockSpec((1, tm, FOURIER_WIDTH), tok),
        pl.BlockSpec((1, tm, FOURIER_WIDTH), tok),
        pl.BlockSpec((1, N_HEADS * QK_PAD, tm), lambda b, i: (b, 0, i)),
        pl.BlockSpec((1, N_HEADS, tm, QK_PAD), lambda b, i: (b, 0, i, 0)),
        pl.BlockSpec((1, 1, N_HEADS, V_HEAD_DIM, tm), lambda b, i: (b, i, 0, 0, 0)),
    ]
    return pl.pallas_call(
        _proj_kernel, grid=grid, in_specs=in_specs, out_specs=out_specs, out_shape=out_shape,
        compiler_params=pltpu.CompilerParams(
            dimension_semantics=("arbitrary", "arbitrary"), vmem_limit_bytes=VMEM_LIMIT),
        name="proj",
    )(x, w['g_attn'], w['w_in'], w['cs'], w['g_q'], w['g_kv'], w['wqn_t'], w['wqr_t'], w['wqrot_t'],
      w['wuk'], w['wuv_t'], w['cos_t'], w['sin_t'], w['cos_p'], w['sin_p'])


def _dft1_kernel(m1_ref, xr_ref, xi_ref, zr_ref, zi_ref):
    xc = jnp.concatenate([xr_ref[0], xi_ref[0]], axis=0)
    z = jnp.dot(m1_ref[...], xc, preferred_element_type=F32)
    zr_ref[0] = z[:DFT_N1].astype(BF16)
    zi_ref[0] = z[DFT_N1:].astype(BF16)


def _dft2_kernel(g_ref, zr_ref, zi_ref, o_ref):
    kb = zr_ref.shape[1]
    for j in range(kb):
        zc = jnp.concatenate([zr_ref[0, j], zi_ref[0, j]], axis=0)
        y = jnp.dot(g_ref[j], zc, preferred_element_type=F32)
        o_ref[0, :, j * FOURIER_WIDTH:(j + 1) * FOURIER_WIDTH] = y.astype(BF16)


def _seq_dft(xr, xi, w):
    B, S, C = xr.shape
    assert S == DFT_N1 * DFT_N2
    wide = DFT_N2 * C
    wc = 8192
    xr2 = xr.reshape(B, DFT_N1, wide)
    xi2 = xi.reshape(B, DFT_N1, wide)
    col = lambda b, j: (b, 0, j)
    zr, zi = pl.pallas_call(
        _dft1_kernel, grid=(B, wide // wc),
        in_specs=[_const_spec((2 * DFT_N1, 2 * DFT_N1)),
                  pl.BlockSpec((1, DFT_N1, wc), col), pl.BlockSpec((1, DFT_N1, wc), col)],
        out_specs=[pl.BlockSpec((1, DFT_N1, wc), col), pl.BlockSpec((1, DFT_N1, wc), col)],
        out_shape=[jax.ShapeDtypeStruct((B, DFT_N1, wide), BF16)] * 2,
        compiler_params=pltpu.CompilerParams(
            dimension_semantics=("arbitrary", "arbitrary"), vmem_limit_bytes=VMEM_LIMIT),
        name="dft1",
    )(w['dft_m1'], xr2, xi2)
    kb = 8
    zr4 = zr.reshape(B, DFT_N1, DFT_N2, C)
    zi4 = zi.reshape(B, DFT_N1, DFT_N2, C)
    blk = lambda b, k: (b, k, 0, 0)
    y = pl.pallas_call(
        _dft2_kernel, grid=(B, DFT_N1 // kb),
        in_specs=[pl.BlockSpec((kb, DFT_N2, 2 * DFT_N2), lambda b, k: (k, 0, 0)),
                  pl.BlockSpec((1, kb, DFT_N2, C), blk), pl.BlockSpec((1, kb, DFT_N2, C), blk)],
        out_specs=pl.BlockSpec((1, DFT_N2, kb * C), lambda b, k: (b, 0, k)),
        out_shape=jax.ShapeDtypeStruct((B, DFT_N2, DFT_N1 * C), BF16),
        compiler_params=pltpu.CompilerParams(
            dimension_semantics=("arbitrary", "arbitrary"), vmem_limit_bytes=VMEM_LIMIT),
        name="dft2",
    )(w['dft_g'], zr4, zi4)
    return y.reshape(B, S, C)


def _attn_kernel(q_ref, k_ref, v_ref, o_ref, m_ref, l_ref, acc_ref):
    tk = v_ref.shape[-1]
    nch = v_ref.shape[1]
    q_t = q_ref[0]
    m_ref[...] = jnp.full(m_ref.shape, -1e30, F32)
    l_ref[...] = jnp.zeros(l_ref.shape, F32)
    acc_ref[...] = jnp.zeros(acc_ref.shape, F32)

    def chunk(c, carry):
        k_c = k_ref[0, 0, pl.ds(pl.multiple_of(c * tk, tk), tk), :]
        s = jnp.dot(k_c, q_t, preferred_element_type=F32)
        m_prev = m_ref[...]
        m_new = jnp.maximum(m_prev, jnp.max(s, axis=0, keepdims=True))
        alpha = jnp.exp2(m_prev - m_new)
        p = jnp.exp2(s - m_new)
        l_ref[...] = alpha * l_ref[...] + jnp.sum(p, axis=0, keepdims=True)
        pv = jnp.dot(v_ref[0, c, 0], p.astype(BF16), preferred_element_type=F32)
        acc_ref[...] = alpha * acc_ref[...] + pv
        m_ref[...] = m_new
        return carry

    lax.fori_loop(0, nch, chunk, 0)
    o_t = acc_ref[...] * (1.0 / l_ref[...])
    o_ref[0] = o_t.T.astype(BF16)


def _attention(qt, kx, vt):
    B, _, S = qt.shape
    nch, tk = vt.shape[1], vt.shape[-1]
    tq = ATTN_TQ
    return pl.pallas_call(
        _attn_kernel, grid=(B, N_HEADS, S // tq),
        in_specs=[pl.BlockSpec((1, QK_PAD, tq), lambda b, h, i: (b, h, i)),
                  pl.BlockSpec((1, 1, S, QK_PAD), lambda b, h, i: (b, h, 0, 0)),
                  pl.BlockSpec((1, nch, 1, V_HEAD_DIM, tk), lambda b, h, i: (b, 0, h, 0, 0))],
        out_specs=pl.BlockSpec((1, tq, V_HEAD_DIM), lambda b, h, i: (b, i, h)),
        out_shape=jax.ShapeDtypeStruct((B, S, MLA_WIDTH), BF16),
        scratch_shapes=[pltpu.VMEM((1, tq), F32), pltpu.VMEM((1, tq), F32), pltpu.VMEM((V_HEAD_DIM, tq), F32)],
        compiler_params=pltpu.CompilerParams(
            dimension_semantics=("arbitrary", "arbitrary", "arbitrary"), vmem_limit_bytes=VMEM_LIMIT),
        name="attn",
    )(qt, kx, vt)


def _merge_kernel(x_ref, f_ref, a_ref, g_f_ref, g_a_ref, w_out_ref, g_ffn_ref, wr_hi_ref, wr_lo_ref, b_r_ref,
                  h_ref, hn_ref, route_ref, w0_ref, w1_ref):
    tm = x_ref.shape[1]
    fn = _rms(f_ref[0].astype(F32), g_f_ref[...]).astype(BF16)
    an = _rms(a_ref[0].astype(F32), g_a_ref[...]).astype(BF16)
    merged = jnp.concatenate([fn, an], axis=-1)
    h = x_ref[0] + jnp.dot(merged, w_out_ref[...], preferred_element_type=F32)
    h_ref[0] = h
    hn = _rms(h, g_ffn_ref[...])
    hn_ref[0] = hn.reshape(tm, ROW_TILES, LANES).astype(BF16)

    hn_hi = hn.astype(BF16)
    hn_lo = (hn - hn_hi.astype(F32)).astype(BF16)
    logits = (jnp.dot(hn_hi, wr_hi_ref[...], preferred_element_type=F32)
              + jnp.dot(hn_hi, wr_lo_ref[...], preferred_element_type=F32)
              + jnp.dot(hn_lo, wr_hi_ref[...], preferred_element_type=F32)
              + b_r_ref[...])

    lane = lax.broadcasted_iota(jnp.int32, logits.shape, 1)
    neg = jnp.float32(-1e30)
    big = jnp.int32(LANES)
    is_g = lane < N_EXPERT_GROUPS
    gl = jnp.where(is_g, logits, neg)
    gmax = jnp.max(gl, axis=-1, keepdims=True)
    g_sel = jnp.min(jnp.where(gl == gmax, lane, big), axis=-1, keepdims=True)
    g_p = 1.0 / jnp.sum(jnp.where(is_g, jnp.exp(gl - gmax), 0.0), axis=-1, keepdims=True)
    e_lo = N_EXPERT_GROUPS + EXPERTS_PER_GROUP * g_sel
    in_grp = jnp.where(lane >= e_lo, jnp.where(lane < e_lo + EXPERTS_PER_GROUP, 1, 0), 0) == 1
    el = jnp.where(in_grp, logits, neg)
    m1 = jnp.max(el, axis=-1, keepdims=True)
    i1 = jnp.min(jnp.where(el == m1, lane, big), axis=-1, keepdims=True)
    el2 = jnp.where(lane == i1, neg, el)
    m2 = jnp.max(el2, axis=-1, keepdims=True)
    i2 = jnp.min(jnp.where(el2 == m2, lane, big), axis=-1, keepdims=True)
    t = jnp.exp(m2 - m1)
    inv = 1.0 / (1.0 + t)
    w0 = g_p * inv
    w1 = g_p * t * inv
    e0 = (i1 - N_EXPERT_GROUPS).astype(F32)
    e1 = (i2 - N_EXPERT_GROUPS).astype(F32)
    route_ref[0] = jnp.where(lane == 0, e0, jnp.where(lane == 1, e1, 0.0))
    w0_ref[0] = jnp.broadcast_to(w0, logits.shape)
    w1_ref[0] = jnp.broadcast_to(w1, logits.shape)


def _merge(x, f_out, a_out, w):
    B, S, _ = x.shape
    tm = TOKEN_TILE
    tok = lambda b, i: (b, i, 0)
    out_shape = [
        jax.ShapeDtypeStruct((B, S, D_MODEL), F32),
        jax.ShapeDtypeStruct((B, S, ROW_TILES, LANES), BF16),
        jax.ShapeDtypeStruct((B, S, LANES), F32),
        jax.ShapeDtypeStruct((B, S, LANES), F32),
        jax.ShapeDtypeStruct((B, S, LANES), F32),
    ]
    in_specs = [
        pl.BlockSpec((1, tm, D_MODEL), tok),
        pl.BlockSpec((1, tm, FOURIER_WIDTH), tok),
        pl.BlockSpec((1, tm, MLA_WIDTH), tok),
        _const_spec((1, FOURIER_WIDTH)),
        _const_spec((1, MLA_WIDTH)),
        _const_spec((D_MODEL, D_MODEL)),
        _const_spec((1, D_MODEL)),
        _const_spec((D_MODEL, LANES)),
        _const_spec((D_MODEL, LANES)),
        _const_spec((1, LANES)),
    ]
    out_specs = [
        pl.BlockSpec((1, tm, D_MODEL), tok),
        pl.BlockSpec((1, tm, ROW_TILES, LANES), lambda b, i: (b, i, 0, 0)),
        pl.BlockSpec((1, tm, LANES), tok),
        pl.BlockSpec((1, tm, LANES), tok),
        pl.BlockSpec((1, tm, LANES), tok),
    ]
    return pl.pallas_call(
        _merge_kernel, grid=(B, S // tm), in_specs=in_specs, out_specs=out_specs, out_shape=out_shape,
        compiler_params=pltpu.CompilerParams(
            dimension_semantics=("arbitrary", "arbitrary"), vmem_limit_bytes=VMEM_LIMIT),
        name="merge",
    )(x, f_out, a_out, w['g_f'], w['g_a'], w['w_out'], w['g_ffn'], w['wr_hi'], w['wr_lo'], w['b_r'])


def _dispatch_kernel(dest_ref, hn_ref, xs_in_ref, xs_ref, sem):
    del xs_in_ref
    tm = hn_ref.shape[0]

    def body(r, carry):
        for k in range(TOP_K):
            d = dest_ref[0, 0, TOP_K * r + k]
            pltpu.make_async_copy(hn_ref.at[r], xs_ref.at[d], sem).start()
        return carry

    lax.fori_loop(0, tm, body, 0)
    for _ in range(TOP_K):
        pltpu.make_async_copy(hn_ref, xs_ref.at[pl.ds(0, tm)], sem).wait()


def _dispatch(dest, hn3, n_rows):
    T = hn3.shape[0]
    tm = TOKEN_TILE
    xs0 = jnp.zeros((n_rows, ROW_TILES, LANES), BF16)
    return pl.pallas_call(
        _dispatch_kernel, grid=(T // tm,),
        in_specs=[pl.BlockSpec((1, 1, TOP_K * tm), lambda i: (i, 0, 0), memory_space=pltpu.SMEM),
                  pl.BlockSpec((tm, ROW_TILES, LANES), lambda i: (i, 0, 0)),
                  pl.BlockSpec(memory_space=pl.ANY)],
        out_specs=pl.BlockSpec(memory_space=pl.ANY),
        out_shape=jax.ShapeDtypeStruct((n_rows, ROW_TILES, LANES), BF16),
        scratch_shapes=[pltpu.SemaphoreType.DMA(())],
        input_output_aliases={2: 0},
        compiler_params=pltpu.CompilerParams(dimension_semantics=("arbitrary",), vmem_limit_bytes=VMEM_LIMIT),
        name="dispatch",
    )(dest.reshape(T // tm, 1, TOP_K * tm), hn3, xs0)


def _expert_kernel(be_ref, nused_ref, xs_ref, wg_ref, wu_ref, wd_ref, y_ref):
    i = pl.program_id(0)

    @pl.when(i < nused_ref[0])
    def _():
        tm = xs_ref.shape[0]
        x = xs_ref[...].reshape(tm, D_MODEL)
        g = jnp.dot(x, wg_ref[0], preferred_element_type=F32)
        u = jnp.dot(x, wu_ref[0], preferred_element_type=F32)
        hmid = (g * jax.nn.sigmoid(g) * u).astype(BF16)
        y = jnp.dot(hmid, wd_ref[0], preferred_element_type=F32)
        y_ref[...] = y.reshape(tm, ROW_TILES, LANES).astype(BF16)

    @pl.when(i >= nused_ref[0])
    def _():
        y_ref[...] = jnp.zeros(y_ref.shape, BF16)


def _experts(block_expert, n_used, xs, w):
    n_rows = xs.shape[0]
    tm = EXPERT_TILE
    nb = n_rows // tm
    row = lambda i, be, nu: (jnp.minimum(i, nu[0] - 1), 0, 0)
    out_row = lambda i, be, nu: (i, 0, 0)
    wsel = lambda i, be, nu: (be[i], 0, 0)
    grid_spec = pltpu.PrefetchScalarGridSpec(
        num_scalar_prefetch=2, grid=(nb,),
        in_specs=[pl.BlockSpec((tm, ROW_TILES, LANES), row),
                  pl.BlockSpec((1, D_MODEL, EXPERT_DIM), wsel),
                  pl.BlockSpec((1, D_MODEL, EXPERT_DIM), wsel),
                  pl.BlockSpec((1, EXPERT_DIM, D_MODEL), wsel)],
        out_specs=pl.BlockSpec((tm, ROW_TILES, LANES), out_row),
    )
    return pl.pallas_call(
        _expert_kernel, grid_spec=grid_spec,
        out_shape=jax.ShapeDtypeStruct((n_rows, ROW_TILES, LANES), BF16),
        compiler_params=pltpu.CompilerParams(dimension_semantics=("arbitrary",), vmem_limit_bytes=VMEM_LIMIT),
        name="experts",
    )(block_expert, n_used, xs, w['w_gate'], w['w_up'], w['w_down'])


def _combine_kernel(dest_ref, h_ref, w0_ref, w1_ref, g_ref, y_ref, o_ref, ybuf, sem):
    tm = h_ref.shape[0]

    def body(r, carry):
        for k in range(TOP_K):
            d = dest_ref[0, 0, TOP_K * r + k]
            pltpu.make_async_copy(y_ref.at[d], ybuf.at[k, r], sem).start()
        return carry

    lax.fori_loop(0, tm, body, 0)
    for k in range(TOP_K):
        pltpu.make_async_copy(y_ref.at[pl.ds(0, tm)], ybuf.at[k], sem).wait()

    y0 = ybuf[0].reshape(tm, D_MODEL)
    y1 = ybuf[1].reshape(tm, D_MODEL)
    w0 = w0_ref[...]
    w1 = w1_ref[...]
    cols = []
    for j in range(ROW_TILES):
        sl = slice(j * LANES, (j + 1) * LANES)
        cols.append(h_ref[:, sl] + w0 * y0[:, sl].astype(F32) + w1 * y1[:, sl].astype(F32))
    hs = jnp.concatenate(cols, axis=-1)
    o_ref[...] = _rms(hs, g_ref[...])


def _combine(dest, h, w0, w1, g_final, y):
    T = h.shape[0]
    tm = TOKEN_TILE
    return pl.pallas_call(
        _combine_kernel, grid=(T // tm,),
        in_specs=[pl.BlockSpec((1, 1, TOP_K * tm), lambda i: (i, 0, 0), memory_space=pltpu.SMEM),
                  pl.BlockSpec((tm, D_MODEL), lambda i: (i, 0)),
                  pl.BlockSpec((tm, LANES), lambda i: (i, 0)),
                  pl.BlockSpec((tm, LANES), lambda i: (i, 0)),
                  _const_spec((1, D_MODEL)),
                  pl.BlockSpec(memory_space=pl.ANY)],
        out_specs=pl.BlockSpec((tm, D_MODEL), lambda i: (i, 0)),
        out_shape=jax.ShapeDtypeStruct((T, D_MODEL), F32),
        scratch_shapes=[pltpu.VMEM((TOP_K, tm, ROW_TILES, LANES), BF16), pltpu.SemaphoreType.DMA(())],
        compiler_params=pltpu.CompilerParams(dimension_semantics=("arbitrary",), vmem_limit_bytes=VMEM_LIMIT),
        name="combine",
    )(dest.reshape(T // tm, 1, TOP_K * tm), h, w0, w1, g_final, y)


def _tables(S):
    pos = jnp.arange(S, dtype=F32)
    inv_freq = 1.0 / (ROPE_THETA ** (jnp.arange(0, QK_ROPE_DIM, 2, dtype=F32) / QK_ROPE_DIM))
    ang = pos[:, None] * inv_freq[None, :]
    cos, sin = jnp.cos(ang), jnp.sin(ang)
    cos2 = jnp.concatenate([cos, cos], axis=-1)
    sin2 = jnp.concatenate([sin, sin], axis=-1)
    zp = jnp.zeros((S, LANES - QK_ROPE_DIM), F32)
    t = {
        'cos_t': cos2.T, 'sin_t': sin2.T,
        'cos_p': jnp.concatenate([cos2, zp], axis=-1), 'sin_p': jnp.concatenate([sin2, zp], axis=-1),
    }

    def phase(rows, cols, n):
        m = (rows[:, None] * cols[None, :]) % n
        a = m.astype(F32) * jnp.float32(2.0 * math.pi / n)
        return jnp.cos(a), jnp.sin(a)

    i1 = jnp.arange(DFT_N1, dtype=jnp.int32)
    c1, s1 = phase(i1, i1, DFT_N1)
    t['dft_m1'] = jnp.concatenate(
        [jnp.concatenate([c1, s1], axis=1), jnp.concatenate([-s1, c1], axis=1)], axis=0).astype(BF16)
    k = jnp.arange(S, dtype=jnp.int32)
    n2 = jnp.arange(DFT_N2, dtype=jnp.int32)
    cg, sg = phase(k, n2, S)
    g = jnp.concatenate([cg, sg], axis=1) * jnp.float32(S ** -0.5)
    t['dft_g'] = g.reshape(DFT_N2, DFT_N1, 2 * DFT_N2).transpose(1, 0, 2).astype(BF16)
    return t


def _layer_weights(g_attn_norm, w_in, g_q_latent, w_uq, g_kv_latent, w_ukv, g_out_fourier, g_out_mla, w_out,
                   g_ffn_norm, w_router_group, b_router_group, w_router_expert, b_router_expert,
                   w_gate, w_up, w_down):
    half = QK_ROPE_DIM // 2
    s3 = FOURIER_WIDTH + Q_LORA_RANK + KV_LORA_RANK
    w_kr = w_in[:, s3:s3 + QK_ROPE_DIM]
    w_kr_rot = jnp.concatenate([-w_kr[:, half:], w_kr[:, :half]], axis=1)
    zc = jnp.zeros((D_MODEL, LANES - QK_ROPE_DIM), F32)
    w_in_ext = jnp.concatenate([w_in[:, :s3], w_kr, zc, w_kr_rot, zc], axis=1)

    c = jnp.arange(FOURIER_GROUP_DIM, dtype=jnp.int32)
    m = (c[:, None] * c[None, :]) % FOURIER_GROUP_DIM
    a = m.astype(F32) * jnp.float32(2.0 * math.pi / FOURIER_GROUP_DIM)
    cs = jnp.concatenate([jnp.cos(a), -jnp.sin(a)], axis=1) * jnp.float32(FOURIER_GROUP_DIM ** -0.5)

    qscale = jnp.float32(QK_HEAD_DIM ** -0.5 * math.log2(math.e))
    wq = (w_uq * qscale).reshape(Q_LORA_RANK, N_HEADS, QK_HEAD_DIM)
    wq_n = wq[:, :, :QK_NOPE_DIM]
    wq_r = wq[:, :, QK_NOPE_DIM:]
    wq_rot = jnp.concatenate([-wq_r[:, :, half:], wq_r[:, :, :half]], axis=2)
    to_t = lambda t: t.reshape(Q_LORA_RANK, -1).T
    wkv = w_ukv.reshape(KV_LORA_RANK, N_HEADS, QK_NOPE_DIM + V_HEAD_DIM)
    w_r = jnp.concatenate(
        [w_router_group, w_router_expert,
         jnp.zeros((D_MODEL, LANES - N_EXPERT_GROUPS - N_EXPERTS), F32)], axis=1)
    wr_hi = w_r.astype(BF16)
    b_r = jnp.concatenate([b_router_group, b_router_expert,
                           jnp.zeros((LANES - N_EXPERT_GROUPS - N_EXPERTS,), F32)])
    return {
        'g_attn': g_attn_norm[None, :], 'w_in': w_in_ext.astype(BF16), 'cs': cs.astype(BF16),
        'g_q': g_q_latent[None, :], 'g_kv': g_kv_latent[None, :],
        'wqn_t': to_t(wq_n).astype(BF16), 'wqr_t': to_t(wq_r).astype(BF16), 'wqrot_t': to_t(wq_rot).astype(BF16),
        'wuk': wkv[:, :, :QK_NOPE_DIM].reshape(KV_LORA_RANK, -1).astype(BF16),
        'wuv_t': wkv[:, :, QK_NOPE_DIM:].reshape(KV_LORA_RANK, -1).T.astype(BF16),
        'g_f': g_out_fourier[None, :], 'g_a': g_out_mla[None, :], 'w_out': w_out.astype(BF16),
        'g_ffn': g_ffn_norm[None, :], 'wr_hi': wr_hi, 'wr_lo': (w_r - wr_hi.astype(F32)).astype(BF16),
        'b_r': b_r[None, :],
        'w_gate': w_gate.astype(BF16), 'w_up': w_up.astype(BF16), 'w_down': w_down.astype(BF16),
    }


def _routing_plan(route, T):
    e = route[:, :TOP_K].astype(jnp.int32)
    flat_e = e.reshape(-1)
    onehot = (flat_e[:, None] == jnp.arange(N_EXPERTS, dtype=jnp.int32)[None, :]).astype(jnp.int32)
    csum = jnp.cumsum(onehot, axis=0)
    rank = jnp.sum(csum * onehot, axis=1) - 1
    counts = csum[-1]
    nblk = (counts + EXPERT_TILE - 1) // EXPERT_TILE
    blk_end = jnp.cumsum(nblk)
    blk_start = blk_end - nblk
    dest = jnp.sum(onehot * blk_start[None, :], axis=1) * EXPERT_TILE + rank
    n_blocks = (T * TOP_K + N_EXPERTS * (EXPERT_TILE - 1) + EXPERT_TILE - 1) // EXPERT_TILE
    n_used = blk_end[-1]
    j = jnp.minimum(jnp.arange(n_blocks, dtype=jnp.int32), n_used - 1)
    block_expert = jnp.minimum(jnp.searchsorted(blk_end, j, side='right'), N_EXPERTS - 1).astype(jnp.int32)
    return dest.astype(jnp.int32), block_expert, n_used.astype(jnp.int32)[None], n_blocks * EXPERT_TILE


def _run(x, w, tables, g_final):
    B, S, _ = x.shape
    T = B * S
    wt = dict(w)
    wt.update(tables)
    xr, xi, qt, kx, vt = _proj(x, wt)
    f_out = _seq_dft(xr, xi, wt)
    a_out = _attention(qt, kx, vt)
    h, hn3, route, w0, w1 = _merge(x, f_out, a_out, wt)
    h = h.reshape(T, D_MODEL)
    hn3 = hn3.reshape(T, ROW_TILES, LANES)
    route = route.reshape(T, LANES)
    w0 = w0.reshape(T, LANES)
    w1 = w1.reshape(T, LANES)
    dest, block_expert, n_used, n_rows = _routing_plan(route, T)
    xs = _dispatch(dest, hn3, n_rows)
    y = _experts(block_expert, n_used, xs, wt)
    out = _combine(dest, h, w0, w1, g_final[None, :], y)
    return out.reshape(B, S, D_MODEL)


def kernel(x_prompt, x_sample, g_attn_norm, w_in, g_q_latent, w_uq, g_kv_latent, w_ukv, g_out_fourier, g_out_mla, w_out, g_ffn_norm, w_router_group, b_router_group, w_router_expert, b_router_expert, w_gate, w_up, w_down, g_final):
    assert g_attn_norm.shape[0] == 1, "single-layer configuration"
    w = _layer_weights(g_attn_norm[0], w_in[0], g_q_latent[0], w_uq[0], g_kv_latent[0], w_ukv[0],
                       g_out_fourier[0], g_out_mla[0], w_out[0], g_ffn_norm[0], w_router_group[0],
                       b_router_group[0], w_router_expert[0], b_router_expert[0], w_gate[0], w_up[0], w_down[0])
    assert x_prompt.shape[1] == x_sample.shape[1]
    tables = _tables(x_prompt.shape[1])
    return (_run(x_prompt, w, tables, g_final), _run(x_sample, w, tables, g_final))
```

```python
import functools
import math

import jax
import jax.numpy as jnp
import numpy as np
from jax import lax
from jax.experimental import pallas as pl
from jax.experimental.pallas import tpu as pltpu

D_MODEL = 2048
FOURIER_WIDTH = 1024
N_FOURIER_GROUPS = 4
FOURIER_GROUP_DIM = 256
MLA_WIDTH = 1024
V_HEAD_DIM = 128
N_HEADS = 8
QK_NOPE_DIM = 128
QK_ROPE_DIM = 64
QK_HEAD_DIM = 192
Q_LORA_RANK = 512
KV_LORA_RANK = 256
ROPE_THETA = 10000.0
N_EXPERT_GROUPS = 4
EXPERTS_PER_GROUP = 8
N_EXPERTS = 32
TOP_K = 2
EXPERT_DIM = 512
EPS = 1e-6

LANES = 128
QK_PAD = 256
ROW_TILES = D_MODEL // LANES

DFT_N1 = 64
DFT_N2 = 128

TOKEN_TILE = 512
ATTN_TQ = 1024
ATTN_TK = 512
ATTN_SUB = 8
DFT_ROWS = 16
EXPERT_TILE = 256
VMEM_LIMIT = 56 * 1024 * 1024

F32 = jnp.float32
BF16 = jnp.bfloat16


def _const_spec(shape):
    nd = len(shape)
    return pl.BlockSpec(shape, lambda *_: (0,) * nd, pipeline_mode=pl.Buffered(1))


def _rms(x, g):
    return x * lax.rsqrt(jnp.mean(x * x, axis=-1, keepdims=True) + EPS) * g


def _proj_kernel(x_ref, g_attn_ref, w_in_ref, cs_ref, g_q_ref, g_kv_ref, wqn_ref, wqr_ref, wqrot_ref,
                 wuk_ref, wuvt_ref, cos_t_ref, sin_t_ref, cos_p_ref, sin_p_ref,
                 xr_ref, xi_ref, qt_ref, kx_ref, vt_ref):
    tm = x_ref.shape[1]
    x = x_ref[0]
    u = _rms(x, g_attn_ref[...]).astype(BF16)
    z = jnp.dot(u, w_in_ref[...], preferred_element_type=F32)

    zf = z[:, :FOURIER_WIDTH].astype(BF16)
    xr, xi = [], []
    for g in range(N_FOURIER_GROUPS):
        zg = zf[:, g * FOURIER_GROUP_DIM:(g + 1) * FOURIER_GROUP_DIM]
        xg = jnp.dot(zg, cs_ref[...], preferred_element_type=F32)
        xr.append(xg[:, :FOURIER_GROUP_DIM])
        xi.append(xg[:, FOURIER_GROUP_DIM:])
    xr_ref[0] = jnp.concatenate(xr, axis=-1).astype(BF16)
    xi_ref[0] = jnp.concatenate(xi, axis=-1).astype(BF16)

    o = FOURIER_WIDTH
    q_lat = z[:, o:o + Q_LORA_RANK]
    o += Q_LORA_RANK
    kv_lat = z[:, o:o + KV_LORA_RANK]
    o += KV_LORA_RANK
    kr = z[:, o:o + LANES]
    kr_rot = z[:, o + LANES:o + 2 * LANES]

    qn = _rms(q_lat, g_q_ref[...]).astype(BF16)
    kvn = _rms(kv_lat, g_kv_ref[...]).astype(BF16)

    nt = (((1,), (1,)), ((), ()))
    qn_t = lax.dot_general(wqn_ref[...], qn, nt, preferred_element_type=F32)
    qr_t = lax.dot_general(wqr_ref[...], qn, nt, preferred_element_type=F32)
    qrot_t = lax.dot_general(wqrot_ref[...], qn, nt, preferred_element_type=F32)
    cos_t = cos_t_ref[...]
    sin_t = sin_t_ref[...]
    zpad = jnp.zeros((QK_PAD - QK_HEAD_DIM, tm), BF16)
    for h in range(N_HEADS):
        r0 = h * QK_PAD
        qt_ref[0, r0:r0 + QK_NOPE_DIM, :] = qn_t[h * QK_NOPE_DIM:(h + 1) * QK_NOPE_DIM].astype(BF16)
        sl = slice(h * QK_ROPE_DIM, (h + 1) * QK_ROPE_DIM)
        roped = qr_t[sl] * cos_t + qrot_t[sl] * sin_t
        qt_ref[0, r0 + QK_NOPE_DIM:r0 + QK_HEAD_DIM, :] = roped.astype(BF16)
        qt_ref[0, r0 + QK_HEAD_DIM:r0 + QK_PAD, :] = zpad

    k_nope = jnp.dot(kvn, wuk_ref[...], preferred_element_type=F32)
    k_rope = (kr * cos_p_ref[...] + kr_rot * sin_p_ref[...]).astype(BF16)
    for h in range(N_HEADS):
        kx_ref[0, h, :, :QK_NOPE_DIM] = k_nope[:, h * QK_NOPE_DIM:(h + 1) * QK_NOPE_DIM].astype(BF16)
        kx_ref[0, h, :, QK_NOPE_DIM:] = k_rope

    v_t = lax.dot_general(wuvt_ref[...], kvn, nt, preferred_element_type=F32)
    vt_ref[0, 0] = v_t.astype(BF16).reshape(N_HEADS, V_HEAD_DIM, tm)


def _proj(x, w):
    B, S, _ = x.shape
    tm = ATTN_TK
    grid = (B, S // tm)
    tok = lambda b, i: (b, i, 0)
    out_shape = [
        jax.ShapeDtypeStruct((B, S, FOURIER_WIDTH), BF16),
        jax.ShapeDtypeStruct((B, S, FOURIER_WIDTH), BF16),
        jax.ShapeDtypeStruct((B, N_HEADS * QK_PAD, S), BF16),
        jax.ShapeDtypeStruct((B, N_HEADS, S, QK_PAD), BF16),
        jax.ShapeDtypeStruct((B, S // tm, N_HEADS, V_HEAD_DIM, tm), BF16),
    ]
    in_specs = [
        pl.BlockSpec((1, tm, D_MODEL), tok),
        _const_spec((1, D_MODEL)),
        _const_spec(w['w_in'].shape),
        _const_spec(w['cs'].shape),
        _const_spec((1, Q_LORA_RANK)),
        _const_spec((1, KV_LORA_RANK)),
        _const_spec(w['wqn_t'].shape),
        _const_spec(w['wqr_t'].shape),
        _const_spec(w['wqrot_t'].shape),
        _const_spec(w['wuk'].shape),
        _const_spec(w['wuv_t'].shape),
        pl.BlockSpec((QK_ROPE_DIM, tm), lambda b, i: (0, i)),
        pl.BlockSpec((QK_ROPE_DIM, tm), lambda b, i: (0, i)),
        pl.BlockSpec((tm, LANES), lambda b, i: (i, 0)),
        pl.BlockSpec((tm, LANES), lambda b, i: (i, 0)),
    ]
    out_specs = [
        pl.BlockSpec((1, tm, FOURIER_WIDTH), tok),
        pl.BlockSpec((1, tm, FOURIER_WIDTH), tok),
        pl.BlockSpec((1, N_HEADS * QK_PAD, tm), lambda b, i: (b, 0, i)),
        pl.BlockSpec((1, N_HEADS, tm, QK_PAD), lambda b, i: (b, 0, i, 0)),
        pl.BlockSpec((1, 1, N_HEADS, V_HEAD_DIM, tm), lambda b, i: (b, i, 0, 0, 0)),
    ]
    return pl.pallas_call(
        _proj_kernel, grid=grid, in_specs=in_specs, out_specs=out_specs, out_shape=out_shape,
        compiler_params=pltpu.CompilerParams(
            dimension_semantics=("arbitrary", "arbitrary"), vmem_limit_bytes=VMEM_LIMIT),
        name="proj",
    )(x, w['g_attn'], w['w_in'], w['cs'], w['g_q'], w['g_kv'], w['wqn_t'], w['wqr_t'], w['wqrot_t'],
      w['wuk'], w['wuv_t'], w['cos_t'], w['sin_t'], w['cos_p'], w['sin_p'])


def _dft1_kernel(m1_ref, xr_ref, xi_ref, zr_ref, zi_ref):
    nb, c = xr_ref.shape[2], xr_ref.shape[3]
    xr = xr_ref[0].reshape(DFT_N1, nb * c)
    xi = xi_ref[0].reshape(DFT_N1, nb * c)
    xc = jnp.concatenate([xr, xi], axis=0)
    z = jnp.dot(m1_ref[...], xc, preferred_element_type=F32).astype(BF16)
    zr_ref[0] = z[:DFT_N1].reshape(DFT_N1, nb, c)
    zi_ref[0] = z[DFT_N1:].reshape(DFT_N1, nb, c)


def _dft2_kernel(g_ref, zr_ref, zi_ref, o_ref):
    kb, c = zr_ref.shape[1], zr_ref.shape[3]
    ys = []
    for j in range(kb):
        zc = jnp.concatenate([zr_ref[0, j], zi_ref[0, j]], axis=0)
        ys.append(jnp.dot(g_ref[j], zc, preferred_element_type=F32).astype(BF16))
    o_ref[0] = jnp.concatenate(ys, axis=-1).reshape(DFT_N2, kb, c)


def _seq_dft(xr, xi, w):
    B, S, C = xr.shape
    assert S == DFT_N1 * DFT_N2
    nb = DFT_ROWS
    xr4 = xr.reshape(B, DFT_N1, DFT_N2, C)
    xi4 = xi.reshape(B, DFT_N1, DFT_N2, C)
    rows = pl.BlockSpec((1, DFT_N1, nb, C), lambda b, j: (b, 0, j, 0))
    zr, zi = pl.pallas_call(
        _dft1_kernel, grid=(B, DFT_N2 // nb),
        in_specs=[_const_spec((2 * DFT_N1, 2 * DFT_N1)), rows, rows],
        out_specs=[rows, rows],
        out_shape=[jax.ShapeDtypeStruct((B, DFT_N1, DFT_N2, C), BF16)] * 2,
        compiler_params=pltpu.CompilerParams(
            dimension_semantics=("arbitrary", "arbitrary"), vmem_limit_bytes=VMEM_LIMIT),
        name="dft1",
    )(w['dft_m1'], xr4, xi4)
    kb = DFT_ROWS
    blk = pl.BlockSpec((1, kb, DFT_N2, C), lambda b, k: (b, k, 0, 0))
    y = pl.pallas_call(
        _dft2_kernel, grid=(B, DFT_N1 // kb),
        in_specs=[pl.BlockSpec((kb, DFT_N2, 2 * DFT_N2), lambda b, k: (k, 0, 0)), blk, blk],
        out_specs=pl.BlockSpec((1, DFT_N2, kb, C), lambda b, k: (b, 0, k, 0)),
        out_shape=jax.ShapeDtypeStruct((B, DFT_N2, DFT_N1, C), BF16),
        compiler_params=pltpu.CompilerParams(
            dimension_semantics=("arbitrary", "arbitrary"), vmem_limit_bytes=VMEM_LIMIT),
        name="dft2",
    )(w['dft_g'], zr, zi)
    return y.reshape(B, S, C)


def _attn_kernel(q_ref, k_ref, v_ref, o_ref, s_scr, acc_ref):
    tk = v_ref.shape[-1]
    nch = v_ref.shape[1]
    tq = q_ref.shape[-1]
    q_t = q_ref[0]

    def scores(c):
        k_c = k_ref[0, 0, pl.ds(pl.multiple_of(c * tk, tk), tk), :]
        return jnp.dot(k_c, q_t, preferred_element_type=F32)

    def step(c, slot, m, l):
        s_scr[1 - slot] = scores(jnp.where(c + 1 >= nch, 0, c + 1))
        s = s_scr[slot]
        m_new = jnp.maximum(m, jnp.max(s, axis=0, keepdims=True))
        alpha = jnp.exp2(m - m_new)
        p = jnp.exp2(s - m_new)
        l = alpha * l + jnp.sum(p, axis=0, keepdims=True)
        pv = jnp.dot(v_ref[0, c, 0], p.astype(BF16), preferred_element_type=F32)
        acc_ref[...] = alpha * acc_ref[...] + pv
        return m_new, l

    acc_ref[...] = jnp.zeros(acc_ref.shape, F32)
    s_scr[0] = scores(0)

    def body(g, carry):
        m, l = carry
        for si in range(ATTN_SUB):
            m, l = step(g * ATTN_SUB + si, si % 2, m, l)
        return m, l

    m0 = jnp.full((1, tq), -1e30, F32)
    l0 = jnp.zeros((1, tq), F32)
    _, l = lax.fori_loop(0, nch // ATTN_SUB, body, (m0, l0))
    o_t = acc_ref[...] * (1.0 / l)
    o_ref[0] = o_t.T.astype(BF16)


def _attention(qt, kx, vt):
    B, _, S = qt.shape
    nch, tk = vt.shape[1], vt.shape[-1]
    tq = ATTN_TQ
    assert ATTN_SUB % 2 == 0 and nch % ATTN_SUB == 0
    return pl.pallas_call(
        _attn_kernel, grid=(B, N_HEADS, S // tq),
        in_specs=[pl.BlockSpec((1, QK_PAD, tq), lambda b, h, i: (b, h, i)),
                  pl.BlockSpec((1, 1, S, QK_PAD), lambda b, h, i: (b, h, 0, 0)),
                  pl.BlockSpec((1, nch, 1, V_HEAD_DIM, tk), lambda b, h, i: (b, 0, h, 0, 0))],
        out_specs=pl.BlockSpec((1, tq, V_HEAD_DIM), lambda b, h, i: (b, i, h)),
        out_shape=jax.ShapeDtypeStruct((B, S, MLA_WIDTH), BF16),
        scratch_shapes=[pltpu.VMEM((2, tk, tq), F32), pltpu.VMEM((V_HEAD_DIM, tq), F32)],
        compiler_params=pltpu.CompilerParams(
            dimension_semantics=("arbitrary", "arbitrary", "arbitrary"), vmem_limit_bytes=VMEM_LIMIT),
        name="attn",
    )(qt, kx, vt)


def _merge_kernel(x_ref, f_ref, a_ref, g_f_ref, g_a_ref, w_out_ref, g_ffn_ref, wr_ref, b_r_ref, tri_ref,
                  h_ref, hn_ref, route_ref, w0_ref, w1_ref, cnt_ref, carry_ref):
    tm = x_ref.shape[1]

    @pl.when((pl.program_id(0) == 0) & (pl.program_id(1) == 0))
    def _():
        carry_ref[...] = jnp.zeros(carry_ref.shape, F32)

    fn = _rms(f_ref[0].astype(F32), g_f_ref[...]).astype(BF16)
    an = _rms(a_ref[0].astype(F32), g_a_ref[...]).astype(BF16)
    merged = jnp.concatenate([fn, an], axis=-1)
    h = x_ref[0] + jnp.dot(merged, w_out_ref[...], preferred_element_type=F32)
    h_ref[0] = h
    hn = _rms(h, g_ffn_ref[...])
    hn_ref[0] = hn.reshape(tm, ROW_TILES, LANES).astype(BF16)

    hn_hi = hn.astype(BF16)
    hn_lo = (hn - hn_hi.astype(F32)).astype(BF16)
    p_hi = jnp.dot(hn_hi, wr_ref[...], preferred_element_type=F32)
    p_lo = jnp.dot(hn_lo, wr_ref[...], preferred_element_type=F32)
    logits = ((p_hi[:, :LANES] + p_hi[:, LANES:]) + (p_lo[:, :LANES] + p_lo[:, LANES:])
              + b_r_ref[...])

    lane = lax.broadcasted_iota(jnp.int32, logits.shape, 1)
    neg = jnp.float32(-1e30)
    big = jnp.int32(LANES)
    is_g = lane < N_EXPERT_GROUPS
    gl = jnp.where(is_g, logits, neg)
    gmax = jnp.max(gl, axis=-1, keepdims=True)
    g_sel = jnp.min(jnp.where(gl == gmax, lane, big), axis=-1, keepdims=True)
    g_p = 1.0 / jnp.sum(jnp.where(is_g, jnp.exp(gl - gmax), 0.0), axis=-1, keepdims=True)
    e_lo = N_EXPERT_GROUPS + EXPERTS_PER_GROUP * g_sel
    in_grp = jnp.where(lane >= e_lo, jnp.where(lane < e_lo + EXPERTS_PER_GROUP, 1, 0), 0) == 1
    el = jnp.where(in_grp, logits, neg)
    m1 = jnp.max(el, axis=-1, keepdims=True)
    i1 = jnp.min(jnp.where(el == m1, lane, big), axis=-1, keepdims=True)
    el2 = jnp.where(lane == i1, neg, el)
    m2 = jnp.max(el2, axis=-1, keepdims=True)
    i2 = jnp.min(jnp.where(el2 == m2, lane, big), axis=-1, keepdims=True)
    t = jnp.exp(m2 - m1)
    inv = 1.0 / (1.0 + t)
    w0 = g_p * inv
    w1 = g_p * t * inv
    e0 = (i1 - N_EXPERT_GROUPS).astype(F32)
    e1 = (i2 - N_EXPERT_GROUPS).astype(F32)

    oh0 = jnp.where(lane == i1, 1.0, 0.0)
    oh1 = jnp.where(lane == i2, 1.0, 0.0)
    oh = oh0 + oh1
    before = jnp.dot(tri_ref[...], oh.astype(BF16), preferred_element_type=F32) + carry_ref[...]
    r0 = jnp.sum(before * oh0, axis=-1, keepdims=True)
    r1 = jnp.sum(before * oh1, axis=-1, keepdims=True)
    total = carry_ref[...] + jnp.sum(oh, axis=0, keepdims=True)
    carry_ref[...] = total
    cnt_ref[...] = total

    route_ref[0] = jnp.where(lane == 0, e0, jnp.where(lane == 1, e1, jnp.where(lane == 2, r0, jnp.where(lane == 3, r1, 0.0))))
    w0_ref[0] = jnp.broadcast_to(w0, logits.shape)
    w1_ref[0] = jnp.broadcast_to(w1, logits.shape)


def _merge(x, f_out, a_out, w):
    B, S, _ = x.shape
    tm = TOKEN_TILE
    tok = lambda b, i: (b, i, 0)
    out_shape = [
        jax.ShapeDtypeStruct((B, S, D_MODEL), F32),
        jax.ShapeDtypeStruct((B, S, ROW_TILES, LANES), BF16),
        jax.ShapeDtypeStruct((B, S, LANES), F32),
        jax.ShapeDtypeStruct((B, S, LANES), F32),
        jax.ShapeDtypeStruct((B, S, LANES), F32),
        jax.ShapeDtypeStruct((1, LANES), F32),
    ]
    in_specs = [
        pl.BlockSpec((1, tm, D_MODEL), tok),
        pl.BlockSpec((1, tm, FOURIER_WIDTH), tok),
        pl.BlockSpec((1, tm, MLA_WIDTH), tok),
        _const_spec((1, FOURIER_WIDTH)),
        _const_spec((1, MLA_WIDTH)),
        _const_spec((D_MODEL, D_MODEL)),
        _const_spec((1, D_MODEL)),
        _const_spec((D_MODEL, 2 * LANES)),
        _const_spec((1, LANES)),
        _const_spec((tm, tm)),
    ]
    out_specs = [
        pl.BlockSpec((1, tm, D_MODEL), tok),
        pl.BlockSpec((1, tm, ROW_TILES, LANES), lambda b, i: (b, i, 0, 0)),
        pl.BlockSpec((1, tm, LANES), tok),
        pl.BlockSpec((1, tm, LANES), tok),
        pl.BlockSpec((1, tm, LANES), tok),
        pl.BlockSpec((1, LANES), lambda b, i: (0, 0)),
    ]
    tri = (lax.broadcasted_iota(jnp.int32, (tm, tm), 1) < lax.broadcasted_iota(jnp.int32, (tm, tm), 0)).astype(BF16)
    return pl.pallas_call(
        _merge_kernel, grid=(B, S // tm), in_specs=in_specs, out_specs=out_specs, out_shape=out_shape,
        scratch_shapes=[pltpu.VMEM((1, LANES), F32)],
        compiler_params=pltpu.CompilerParams(
            dimension_semantics=("arbitrary", "arbitrary"), vmem_limit_bytes=VMEM_LIMIT),
        name="merge",
    )(x, f_out, a_out, w['g_f'], w['g_a'], w['w_out'], w['g_ffn'], w['wr'], w['b_r'], tri)


def _dispatch_kernel(dest_ref, hn_ref, xs_in_ref, xs_ref, sem):
    del xs_in_ref
    tm = hn_ref.shape[0]

    def body(r, carry):
        for k in range(TOP_K):
            d = dest_ref[0, 0, TOP_K * r + k]
            pltpu.make_async_copy(hn_ref.at[r], xs_ref.at[d], sem).start(priority=k)
        return carry

    lax.fori_loop(0, tm, body, 0)
    for _ in range(TOP_K):
        pltpu.make_async_copy(hn_ref, xs_ref.at[pl.ds(0, tm)], sem).wait()


def _dispatch(dest, hn3, n_rows):
    T = hn3.shape[0]
    tm = TOKEN_TILE
    xs0 = jnp.zeros((n_rows, ROW_TILES, LANES), BF16)
    return pl.pallas_call(
        _dispatch_kernel, grid=(T // tm,),
        in_specs=[pl.BlockSpec((1, 1, TOP_K * tm), lambda i: (i, 0, 0), memory_space=pltpu.SMEM),
                  pl.BlockSpec((tm, ROW_TILES, LANES), lambda i: (i, 0, 0)),
                  pl.BlockSpec(memory_space=pl.ANY)],
        out_specs=pl.BlockSpec(memory_space=pl.ANY),
        out_shape=jax.ShapeDtypeStruct((n_rows, ROW_TILES, LANES), BF16),
        scratch_shapes=[pltpu.SemaphoreType.DMA(())],
        input_output_aliases={2: 0},
        compiler_params=pltpu.CompilerParams(dimension_semantics=("arbitrary",), vmem_limit_bytes=VMEM_LIMIT),
        name="dispatch",
    )(dest.reshape(T // tm, 1, TOP_K * tm), hn3, xs0)


def _expert_kernel(be_ref, nused_ref, xs_ref, wg_ref, wu_ref, wd_ref, y_ref):
    i = pl.program_id(0)

    @pl.when(i < nused_ref[0])
    def _():
        tm = xs_ref.shape[0]
        x = xs_ref[...].reshape(tm, D_MODEL)
        g = jnp.dot(x, wg_ref[0], preferred_element_type=F32)
        u = jnp.dot(x, wu_ref[0], preferred_element_type=F32)
        hmid = (g * jax.nn.sigmoid(g) * u).astype(BF16)
        y = jnp.dot(hmid, wd_ref[0], preferred_element_type=F32)
        y_ref[...] = y.reshape(tm, ROW_TILES, LANES).astype(BF16)

    @pl.when(i >= nused_ref[0])
    def _():
        y_ref[...] = jnp.zeros(y_ref.shape, BF16)


def _experts(block_expert, n_used, xs, w):
    n_rows = xs.shape[0]
    tm = EXPERT_TILE
    nb = n_rows // tm
    row = lambda i, be, nu: (jnp.minimum(i, nu[0] - 1), 0, 0)
    out_row = lambda i, be, nu: (i, 0, 0)
    wsel = lambda i, be, nu: (be[i], 0, 0)
    grid_spec = pltpu.PrefetchScalarGridSpec(
        num_scalar_prefetch=2, grid=(nb,),
        in_specs=[pl.BlockSpec((tm, ROW_TILES, LANES), row),
                  pl.BlockSpec((1, D_MODEL, EXPERT_DIM), wsel),
                  pl.BlockSpec((1, D_MODEL, EXPERT_DIM), wsel),
                  pl.BlockSpec((1, EXPERT_DIM, D_MODEL), wsel)],
        out_specs=pl.BlockSpec((tm, ROW_TILES, LANES), out_row),
    )
    return pl.pallas_call(
        _expert_kernel, grid_spec=grid_spec,
        out_shape=jax.ShapeDtypeStruct((n_rows, ROW_TILES, LANES), BF16),
        compiler_params=pltpu.CompilerParams(dimension_semantics=("arbitrary",), vmem_limit_bytes=VMEM_LIMIT),
        name="experts",
    )(block_expert, n_used, xs, w['w_gate'], w['w_up'], w['w_down'])


def _combine_kernel(dest_ref, dest_next_ref, h_ref, w0_ref, w1_ref, g_ref, y_ref, o_ref, ybuf, sems):
    tm = h_ref.shape[0]
    i = pl.program_id(0)
    slot = lax.rem(i, 2)

    def start_gather(d_ref, s):
        def body(r, carry):
            for k in range(TOP_K):
                d = d_ref[0, 0, TOP_K * r + k]
                pltpu.make_async_copy(y_ref.at[d], ybuf.at[s, k, r], sems.at[s]).start(priority=k)
            return carry

        lax.fori_loop(0, tm, body, 0)

    @pl.when(i == 0)
    def _():
        start_gather(dest_ref, 0)

    @pl.when(i + 1 < pl.num_programs(0))
    def _():
        start_gather(dest_next_ref, 1 - slot)

    for k in range(TOP_K):
        pltpu.make_async_copy(y_ref.at[pl.ds(0, tm)], ybuf.at[slot, k], sems.at[slot]).wait()

    y0 = ybuf[slot, 0].reshape(tm, D_MODEL)
    y1 = ybuf[slot, 1].reshape(tm, D_MODEL)
    w0 = w0_ref[...]
    w1 = w1_ref[...]
    cols = []
    for j in range(ROW_TILES):
        sl = slice(j * LANES, (j + 1) * LANES)
        cols.append(h_ref[:, sl] + w0 * y0[:, sl].astype(F32) + w1 * y1[:, sl].astype(F32))
    hs = jnp.concatenate(cols, axis=-1)
    o_ref[...] = _rms(hs, g_ref[...])


def _combine(dest, h, w0, w1, g_final, y):
    T = h.shape[0]
    tm = TOKEN_TILE
    n = T // tm
    dest3 = dest.reshape(n, 1, TOP_K * tm)
    return pl.pallas_call(
        _combine_kernel, grid=(n,),
        in_specs=[pl.BlockSpec((1, 1, TOP_K * tm), lambda i: (i, 0, 0), memory_space=pltpu.SMEM),
                  pl.BlockSpec((1, 1, TOP_K * tm), lambda i: (jnp.minimum(i + 1, n - 1), 0, 0),
                               memory_space=pltpu.SMEM),
                  pl.BlockSpec((tm, D_MODEL), lambda i: (i, 0)),
                  pl.BlockSpec((tm, LANES), lambda i: (i, 0)),
                  pl.BlockSpec((tm, LANES), lambda i: (i, 0)),
                  _const_spec((1, D_MODEL)),
                  pl.BlockSpec(memory_space=pl.ANY)],
        out_specs=pl.BlockSpec((tm, D_MODEL), lambda i: (i, 0)),
        out_shape=jax.ShapeDtypeStruct((T, D_MODEL), F32),
        scratch_shapes=[pltpu.VMEM((2, TOP_K, tm, ROW_TILES, LANES), BF16), pltpu.SemaphoreType.DMA((2,))],
        compiler_params=pltpu.CompilerParams(dimension_semantics=("arbitrary",), vmem_limit_bytes=VMEM_LIMIT),
        name="combine",
    )(dest3, dest3, h, w0, w1, g_final, y)


def _tables(S):
    pos = jnp.arange(S, dtype=F32)
    inv_freq = 1.0 / (ROPE_THETA ** (jnp.arange(0, QK_ROPE_DIM, 2, dtype=F32) / QK_ROPE_DIM))
    ang = pos[:, None] * inv_freq[None, :]
    cos, sin = jnp.cos(ang), jnp.sin(ang)
    cos2 = jnp.concatenate([cos, cos], axis=-1)
    sin2 = jnp.concatenate([sin, sin], axis=-1)
    zp = jnp.zeros((S, LANES - QK_ROPE_DIM), F32)
    t = {
        'cos_t': cos2.T, 'sin_t': sin2.T,
        'cos_p': jnp.concatenate([cos2, zp], axis=-1), 'sin_p': jnp.concatenate([sin2, zp], axis=-1),
    }

    def phase(rows, cols, n):
        m = (rows[:, None] * cols[None, :]) % n
        a = m.astype(F32) * jnp.float32(2.0 * math.pi / n)
        return jnp.cos(a), jnp.sin(a)

    i1 = jnp.arange(DFT_N1, dtype=jnp.int32)
    c1, s1 = phase(i1, i1, DFT_N1)
    t['dft_m1'] = jnp.concatenate(
        [jnp.concatenate([c1, s1], axis=1), jnp.concatenate([-s1, c1], axis=1)], axis=0).astype(BF16)
    k = jnp.arange(S, dtype=jnp.int32)
    n2 = jnp.arange(DFT_N2, dtype=jnp.int32)
    cg, sg = phase(k, n2, S)
    g = jnp.concatenate([cg, sg], axis=1) * jnp.float32(S ** -0.5)
    t['dft_g'] = g.reshape(DFT_N2, DFT_N1, 2 * DFT_N2).transpose(1, 0, 2).astype(BF16)
    return t


def _layer_weights(g_attn_norm, w_in, g_q_latent, w_uq, g_kv_latent, w_ukv, g_out_fourier, g_out_mla, w_out,
                   g_ffn_norm, w_router_group, b_router_group, w_router_expert, b_router_expert,
                   w_gate, w_up, w_down):
    half = QK_ROPE_DIM // 2
    s3 = FOURIER_WIDTH + Q_LORA_RANK + KV_LORA_RANK
    w_kr = w_in[:, s3:s3 + QK_ROPE_DIM]
    w_kr_rot = jnp.concatenate([-w_kr[:, half:], w_kr[:, :half]], axis=1)
    zc = jnp.zeros((D_MODEL, LANES - QK_ROPE_DIM), F32)
    w_in_ext = jnp.concatenate([w_in[:, :s3], w_kr, zc, w_kr_rot, zc], axis=1)

    c = jnp.arange(FOURIER_GROUP_DIM, dtype=jnp.int32)
    m = (c[:, None] * c[None, :]) % FOURIER_GROUP_DIM
    a = m.astype(F32) * jnp.float32(2.0 * math.pi / FOURIER_GROUP_DIM)
    cs = jnp.concatenate([jnp.cos(a), -jnp.sin(a)], axis=1) * jnp.float32(FOURIER_GROUP_DIM ** -0.5)

    qscale = jnp.float32(QK_HEAD_DIM ** -0.5 * math.log2(math.e))
    wq = (w_uq * qscale).reshape(Q_LORA_RANK, N_HEADS, QK_HEAD_DIM)
    wq_n = wq[:, :, :QK_NOPE_DIM]
    wq_r = wq[:, :, QK_NOPE_DIM:]
    wq_rot = jnp.concatenate([-wq_r[:, :, half:], wq_r[:, :, :half]], axis=2)
    to_t = lambda t: t.reshape(Q_LORA_RANK, -1).T
    wkv = w_ukv.reshape(KV_LORA_RANK, N_HEADS, QK_NOPE_DIM + V_HEAD_DIM)
    w_r = jnp.concatenate(
        [w_router_group, w_router_expert,
         jnp.zeros((D_MODEL, LANES - N_EXPERT_GROUPS - N_EXPERTS), F32)], axis=1)
    wr_hi = w_r.astype(BF16)
    b_r = jnp.concatenate([b_router_group, b_router_expert,
                           jnp.zeros((LANES - N_EXPERT_GROUPS - N_EXPERTS,), F32)])
    return {
        'g_attn': g_attn_norm[None, :], 'w_in': w_in_ext.astype(BF16), 'cs': cs.astype(BF16),
        'g_q': g_q_latent[None, :], 'g_kv': g_kv_latent[None, :],
        'wqn_t': to_t(wq_n).astype(BF16), 'wqr_t': to_t(wq_r).astype(BF16), 'wqrot_t': to_t(wq_rot).astype(BF16),
        'wuk': wkv[:, :, :QK_NOPE_DIM].reshape(KV_LORA_RANK, -1).astype(BF16),
        'wuv_t': wkv[:, :, QK_NOPE_DIM:].reshape(KV_LORA_RANK, -1).T.astype(BF16),
        'g_f': g_out_fourier[None, :], 'g_a': g_out_mla[None, :], 'w_out': w_out.astype(BF16),
        'g_ffn': g_ffn_norm[None, :],
        'wr': jnp.concatenate([wr_hi, (w_r - wr_hi.astype(F32)).astype(BF16)], axis=1),
        'b_r': b_r[None, :],
        'w_gate': w_gate.astype(BF16), 'w_up': w_up.astype(BF16), 'w_down': w_down.astype(BF16),
    }


def _routing_plan(route, counts, T):
    e = route[:, :TOP_K].astype(jnp.int32)
    rank = route[:, TOP_K:2 * TOP_K].astype(jnp.int32)
    counts = counts[0, N_EXPERT_GROUPS:N_EXPERT_GROUPS + N_EXPERTS].astype(jnp.int32)
    nblk = (counts + EXPERT_TILE - 1) // EXPERT_TILE
    blk_end = jnp.cumsum(nblk)
    blk_start = blk_end - nblk
    onehot = e[:, :, None] == jnp.arange(N_EXPERTS, dtype=jnp.int32)[None, None, :]
    dest = (jnp.sum(jnp.where(onehot, blk_start[None, None, :], 0), axis=-1) * EXPERT_TILE + rank).reshape(-1)
    n_blocks = (T * TOP_K + N_EXPERTS * (EXPERT_TILE - 1) + EXPERT_TILE - 1) // EXPERT_TILE
    n_used = blk_end[-1]
    j = jnp.minimum(jnp.arange(n_blocks, dtype=jnp.int32), n_used - 1)
    block_expert = jnp.minimum(jnp.searchsorted(blk_end, j, side='right'), N_EXPERTS - 1).astype(jnp.int32)
    return dest.astype(jnp.int32), block_expert, n_used.astype(jnp.int32)[None], n_blocks * EXPERT_TILE


def _run(x, w, tables, g_final):
    B, S, _ = x.shape
    T = B * S
    wt = dict(w)
    wt.update(tables)
    xr, xi, qt, kx, vt = _proj(x, wt)
    f_out = _seq_dft(xr, xi, wt)
    a_out = _attention(qt, kx, vt)
    h, hn3, route, w0, w1, counts = _merge(x, f_out, a_out, wt)
    h = h.reshape(T, D_MODEL)
    hn3 = hn3.reshape(T, ROW_TILES, LANES)
    route = route.reshape(T, LANES)
    w0 = w0.reshape(T, LANES)
    w1 = w1.reshape(T, LANES)
    dest, block_expert, n_used, n_rows = _routing_plan(route, counts, T)
    xs = _dispatch(dest, hn3, n_rows)
    y = _experts(block_expert, n_used, xs, wt)
    out = _combine(dest, h, w0, w1, g_final[None, :], y)
    return out.reshape(B, S, D_MODEL)


def kernel(x_prompt, x_sample, g_attn_norm, w_in, g_q_latent, w_uq, g_kv_latent, w_ukv, g_out_fourier, g_out_mla, w_out, g_ffn_norm, w_router_group, b_router_group, w_router_expert, b_router_expert, w_gate, w_up, w_down, g_final):
    assert g_attn_norm.shape[0] == 1, "single-layer configuration"
    w = _layer_weights(g_attn_norm[0], w_in[0], g_q_latent[0], w_uq[0], g_kv_latent[0], w_ukv[0],
                       g_out_fourier[0], g_out_mla[0], w_out[0], g_ffn_norm[0], w_router_group[0],
                       b_router_group[0], w_router_expert[0], b_router_expert[0], w_gate[0], w_up[0], w_down[0])
    assert x_prompt.shape[1] == x_sample.shape[1]
    tables = _tables(x_prompt.shape[1])
    return (_run(x_prompt, w, tables, g_final), _run(x_sample, w, tables, g_final))
```

```python
import functools
import math

import jax
import jax.numpy as jnp
import numpy as np
from jax import lax
from jax.experimental import pallas as pl
from jax.experimental.pallas import tpu as pltpu

D_MODEL = 2048
FOURIER_WIDTH = 1024
N_FOURIER_GROUPS = 4
FOURIER_GROUP_DIM = 256
MLA_WIDTH = 1024
V_HEAD_DIM = 128
N_HEADS = 8
QK_NOPE_DIM = 128
QK_ROPE_DIM = 64
QK_HEAD_DIM = 192
Q_LORA_RANK = 512
KV_LORA_RANK = 256
ROPE_THETA = 10000.0
N_EXPERT_GROUPS = 4
EXPERTS_PER_GROUP = 8
N_EXPERTS = 32
TOP_K = 2
EXPERT_DIM = 512
EPS = 1e-6

LANES = 128
QK_PAD = 256
ROW_TILES = D_MODEL // LANES

DFT_N1 = 64
DFT_N2 = 128

TOKEN_TILE = 512
ATTN_TQ = 1024
ATTN_TK = 512
ATTN_SUB = 8
DFT_ROWS = 16
EXPERT_TILE = 256
VMEM_LIMIT = 56 * 1024 * 1024

F32 = jnp.float32
BF16 = jnp.bfloat16


def _const_spec(shape):
    nd = len(shape)
    return pl.BlockSpec(shape, lambda *_: (0,) * nd, pipeline_mode=pl.Buffered(1))


def _rms(x, g):
    return x * lax.rsqrt(jnp.mean(x * x, axis=-1, keepdims=True) + EPS) * g


def _proj_kernel(x_ref, g_attn_ref, w_in_ref, cs_ref, g_q_ref, g_kv_ref, wqn_ref, wqr_ref, wqrot_ref,
                 wuk_ref, wuvt_ref, cos_t_ref, sin_t_ref, cos_p_ref, sin_p_ref,
                 xr_ref, xi_ref, qt_ref, kx_ref, vt_ref):
    tm = x_ref.shape[1]
    x = x_ref[0]
    u = _rms(x, g_attn_ref[...]).astype(BF16)
    z = jnp.dot(u, w_in_ref[...], preferred_element_type=F32)

    zf = z[:, :FOURIER_WIDTH].astype(BF16)
    xr, xi = [], []
    for g in range(N_FOURIER_GROUPS):
        zg = zf[:, g * FOURIER_GROUP_DIM:(g + 1) * FOURIER_GROUP_DIM]
        xg = jnp.dot(zg, cs_ref[...], preferred_element_type=F32)
        xr.append(xg[:, :FOURIER_GROUP_DIM])
        xi.append(xg[:, FOURIER_GROUP_DIM:])
    xr_ref[0] = jnp.concatenate(xr, axis=-1).astype(BF16)
    xi_ref[0] = jnp.concatenate(xi, axis=-1).astype(BF16)

    o = FOURIER_WIDTH
    q_lat = z[:, o:o + Q_LORA_RANK]
    o += Q_LORA_RANK
    kv_lat = z[:, o:o + KV_LORA_RANK]
    o += KV_LORA_RANK
    kr = z[:, o:o + LANES]
    kr_rot = z[:, o + LANES:o + 2 * LANES]

    qn = _rms(q_lat, g_q_ref[...]).astype(BF16)
    kvn = _rms(kv_lat, g_kv_ref[...]).astype(BF16)

    nt = (((1,), (1,)), ((), ()))
    qn_t = lax.dot_general(wqn_ref[...], qn, nt, preferred_element_type=F32)
    qr_t = lax.dot_general(wqr_ref[...], qn, nt, preferred_element_type=F32)
    qrot_t = lax.dot_general(wqrot_ref[...], qn, nt, preferred_element_type=F32)
    cos_t = cos_t_ref[...]
    sin_t = sin_t_ref[...]
    zpad = jnp.zeros((QK_PAD - QK_HEAD_DIM, tm), BF16)
    for h in range(N_HEADS):
        r0 = h * QK_PAD
        qt_ref[0, r0:r0 + QK_NOPE_DIM, :] = qn_t[h * QK_NOPE_DIM:(h + 1) * QK_NOPE_DIM].astype(BF16)
        sl = slice(h * QK_ROPE_DIM, (h + 1) * QK_ROPE_DIM)
        roped = qr_t[sl] * cos_t + qrot_t[sl] * sin_t
        qt_ref[0, r0 + QK_NOPE_DIM:r0 + QK_HEAD_DIM, :] = roped.astype(BF16)
        qt_ref[0, r0 + QK_HEAD_DIM:r0 + QK_PAD, :] = zpad

    k_nope = jnp.dot(kvn, wuk_ref[...], preferred_element_type=F32)
    k_rope = (kr * cos_p_ref[...] + kr_rot * sin_p_ref[...]).astype(BF16)
    for h in range(N_HEADS):
        kx_ref[0, h, :, :QK_NOPE_DIM] = k_nope[:, h * QK_NOPE_DIM:(h + 1) * QK_NOPE_DIM].astype(BF16)
        kx_ref[0, h, :, QK_NOPE_DIM:] = k_rope

    v_t = lax.dot_general(wuvt_ref[...], kvn, nt, preferred_element_type=F32)
    vt_ref[0, 0] = v_t.astype(BF16).reshape(N_HEADS, V_HEAD_DIM, tm)


def _proj(x, w):
    B, S, _ = x.shape
    tm = ATTN_TK
    grid = (B, S // tm)
    tok = lambda b, i: (b, i, 0)
    out_shape = [
        jax.ShapeDtypeStruct((B, S, FOURIER_WIDTH), BF16),
        jax.ShapeDtypeStruct((B, S, FOURIER_WIDTH), BF16),
        jax.ShapeDtypeStruct((B, N_HEADS * QK_PAD, S), BF16),
        jax.ShapeDtypeStruct((B, N_HEADS, S, QK_PAD), BF16),
        jax.ShapeDtypeStruct((B, S // tm, N_HEADS, V_HEAD_DIM, tm), BF16),
    ]
    in_specs = [
        pl.BlockSpec((1, tm, D_MODEL), tok),
        _const_spec((1, D_MODEL)),
        _const_spec(w['w_in'].shape),
        _const_spec(w['cs'].shape),
        _const_spec((1, Q_LORA_RANK)),
        _const_spec((1, KV_LORA_RANK)),
        _const_spec(w['wqn_t'].shape),
        _const_spec(w['wqr_t'].shape),
        _const_spec(w['wqrot_t'].shape),
        _const_spec(w['wuk'].shape),
        _const_spec(w['wuv_t'].shape),
        pl.BlockSpec((QK_ROPE_DIM, tm), lambda b, i: (0, i)),
        pl.BlockSpec((QK_ROPE_DIM, tm), lambda b, i: (0, i)),
        pl.BlockSpec((tm, LANES), lambda b, i: (i, 0)),
        pl.BlockSpec((tm, LANES), lambda b, i: (i, 0)),
    ]
    out_specs = [
        pl.BlockSpec((1, tm, FOURIER_WIDTH), tok),
        pl.BlockSpec((1, tm, FOURIER_WIDTH), tok),
        pl.BlockSpec((1, N_HEADS * QK_PAD, tm), lambda b, i: (b, 0, i)),
        pl.BlockSpec((1, N_HEADS, tm, QK_PAD), lambda b, i: (b, 0, i, 0)),
        pl.BlockSpec((1, 1, N_HEADS, V_HEAD_DIM, tm), lambda b, i: (b, i, 0, 0, 0)),
    ]
    return pl.pallas_call(
        _proj_kernel, grid=grid, in_specs=in_specs, out_specs=out_specs, out_shape=out_shape,
        compiler_params=pltpu.CompilerParams(
            dimension_semantics=("arbitrary", "arbitrary"), vmem_limit_bytes=VMEM_LIMIT),
        name="proj",
    )(x, w['g_attn'], w['w_in'], w['cs'], w['g_q'], w['g_kv'], w['wqn_t'], w['wqr_t'], w['wqrot_t'],
      w['wuk'], w['wuv_t'], w['cos_t'], w['sin_t'], w['cos_p'], w['sin_p'])


def _dft1_kernel(m1_ref, xr_ref, xi_ref, zr_ref, zi_ref):
    nb, c = xr_ref.shape[2], xr_ref.shape[3]
    xr = xr_ref[0].reshape(DFT_N1, nb * c)
    xi = xi_ref[0].reshape(DFT_N1, nb * c)
    xc = jnp.concatenate([xr, xi], axis=0)
    z = jnp.dot(m1_ref[...], xc, preferred_element_type=F32).astype(BF16)
    zr_ref[0] = z[:DFT_N1].reshape(DFT_N1, nb, c)
    zi_ref[0] = z[DFT_N1:].reshape(DFT_N1, nb, c)


def _dft2_kernel(g_ref, zr_ref, zi_ref, o_ref):
    kb, c = zr_ref.shape[1], zr_ref.shape[3]
    ys = []
    for j in range(kb):
        zc = jnp.concatenate([zr_ref[0, j], zi_ref[0, j]], axis=0)
        ys.append(jnp.dot(g_ref[j], zc, preferred_element_type=F32).astype(BF16))
    o_ref[0] = jnp.concatenate(ys, axis=-1).reshape(DFT_N2, kb, c)


def _seq_dft(xr, xi, w):
    B, S, C = xr.shape
    assert S == DFT_N1 * DFT_N2
    nb = DFT_ROWS
    xr4 = xr.reshape(B, DFT_N1, DFT_N2, C)
    xi4 = xi.reshape(B, DFT_N1, DFT_N2, C)
    rows = pl.BlockSpec((1, DFT_N1, nb, C), lambda b, j: (b, 0, j, 0))
    zr, zi = pl.pallas_call(
        _dft1_kernel, grid=(B, DFT_N2 // nb),
        in_specs=[_const_spec((2 * DFT_N1, 2 * DFT_N1)), rows, rows],
        out_specs=[rows, rows],
        out_shape=[jax.ShapeDtypeStruct((B, DFT_N1, DFT_N2, C), BF16)] * 2,
        compiler_params=pltpu.CompilerParams(
            dimension_semantics=("arbitrary", "arbitrary"), vmem_limit_bytes=VMEM_LIMIT),
        name="dft1",
    )(w['dft_m1'], xr4, xi4)
    kb = DFT_ROWS
    blk = pl.BlockSpec((1, kb, DFT_N2, C), lambda b, k: (b, k, 0, 0))
    y = pl.pallas_call(
        _dft2_kernel, grid=(B, DFT_N1 // kb),
        in_specs=[pl.BlockSpec((kb, DFT_N2, 2 * DFT_N2), lambda b, k: (k, 0, 0)), blk, blk],
        out_specs=pl.BlockSpec((1, DFT_N2, kb, C), lambda b, k: (b, 0, k, 0)),
        out_shape=jax.ShapeDtypeStruct((B, DFT_N2, DFT_N1, C), BF16),
        compiler_params=pltpu.CompilerParams(
            dimension_semantics=("arbitrary", "arbitrary"), vmem_limit_bytes=VMEM_LIMIT),
        name="dft2",
    )(w['dft_g'], zr, zi)
    return y.reshape(B, S, C)


def _attn_kernel(q_ref, k_ref, v_ref, o_ref, s_scr, acc_ref):
    tk = v_ref.shape[-1]
    nch = v_ref.shape[1]
    tq = q_ref.shape[-1]
    q_t = q_ref[0]

    def scores(c):
        k_c = k_ref[0, 0, pl.ds(pl.multiple_of(c * tk, tk), tk), :]
        return jnp.dot(k_c, q_t, preferred_element_type=F32)

    def step(c, slot, m, l):
        s_scr[1 - slot] = scores(jnp.where(c + 1 >= nch, 0, c + 1))
        s = s_scr[slot]
        m_new = jnp.maximum(m, jnp.max(s, axis=0, keepdims=True))
        alpha = jnp.exp2(m - m_new)
        p = jnp.exp2(s - m_new)
        l = alpha * l + jnp.sum(p, axis=0, keepdims=True)
        pv = jnp.dot(v_ref[0, c, 0], p.astype(BF16), preferred_element_type=F32)
        acc_ref[...] = alpha * acc_ref[...] + pv
        return m_new, l

    acc_ref[...] = jnp.zeros(acc_ref.shape, F32)
    s_scr[0] = scores(0)

    def body(g, carry):
        m, l = carry
        for si in range(ATTN_SUB):
            m, l = step(g * ATTN_SUB + si, si % 2, m, l)
        return m, l

    m0 = jnp.full((1, tq), -1e30, F32)
    l0 = jnp.zeros((1, tq), F32)
    _, l = lax.fori_loop(0, nch // ATTN_SUB, body, (m0, l0))
    o_t = acc_ref[...] * (1.0 / l)
    o_ref[0] = o_t.T.astype(BF16)


def _attention(qt, kx, vt):
    B, _, S = qt.shape
    nch, tk = vt.shape[1], vt.shape[-1]
    tq = ATTN_TQ
    assert ATTN_SUB % 2 == 0 and nch % ATTN_SUB == 0
    return pl.pallas_call(
        _attn_kernel, grid=(B, N_HEADS, S // tq),
        in_specs=[pl.BlockSpec((1, QK_PAD, tq), lambda b, h, i: (b, h, i)),
                  pl.BlockSpec((1, 1, S, QK_PAD), lambda b, h, i: (b, h, 0, 0)),
                  pl.BlockSpec((1, nch, 1, V_HEAD_DIM, tk), lambda b, h, i: (b, 0, h, 0, 0))],
        out_specs=pl.BlockSpec((1, tq, V_HEAD_DIM), lambda b, h, i: (b, i, h)),
        out_shape=jax.ShapeDtypeStruct((B, S, MLA_WIDTH), BF16),
        scratch_shapes=[pltpu.VMEM((2, tk, tq), F32), pltpu.VMEM((V_HEAD_DIM, tq), F32)],
        compiler_params=pltpu.CompilerParams(
            dimension_semantics=("arbitrary", "arbitrary", "arbitrary"), vmem_limit_bytes=VMEM_LIMIT),
        name="attn",
    )(qt, kx, vt)


def _merge_kernel(x_ref, f_ref, a_ref, g_f_ref, g_a_ref, w_out_ref, g_ffn_ref, wr_ref, b_r_ref, tri_ref, cnt0_ref,
                  h_ref, hn_ref, route_ref, w0_ref, w1_ref, cnt_ref, carry_ref):
    tm = x_ref.shape[1]

    @pl.when((pl.program_id(0) == 0) & (pl.program_id(1) == 0))
    def _():
        carry_ref[...] = cnt0_ref[...]

    fn = _rms(f_ref[0].astype(F32), g_f_ref[...]).astype(BF16)
    an = _rms(a_ref[0].astype(F32), g_a_ref[...]).astype(BF16)
    merged = jnp.concatenate([fn, an], axis=-1)
    h = x_ref[0] + jnp.dot(merged, w_out_ref[...], preferred_element_type=F32)
    h_ref[0] = h
    hn = _rms(h, g_ffn_ref[...])
    hn_ref[0] = hn.reshape(tm, ROW_TILES, LANES).astype(BF16)

    hn_hi = hn.astype(BF16)
    hn_lo = (hn - hn_hi.astype(F32)).astype(BF16)
    p_hi = jnp.dot(hn_hi, wr_ref[...], preferred_element_type=F32)
    p_lo = jnp.dot(hn_lo, wr_ref[...], preferred_element_type=F32)
    logits = ((p_hi[:, :LANES] + p_hi[:, LANES:]) + (p_lo[:, :LANES] + p_lo[:, LANES:])
              + b_r_ref[...])

    lane = lax.broadcasted_iota(jnp.int32, logits.shape, 1)
    neg = jnp.float32(-1e30)
    big = jnp.int32(LANES)
    is_g = lane < N_EXPERT_GROUPS
    gl = jnp.where(is_g, logits, neg)
    gmax = jnp.max(gl, axis=-1, keepdims=True)
    g_sel = jnp.min(jnp.where(gl == gmax, lane, big), axis=-1, keepdims=True)
    g_p = 1.0 / jnp.sum(jnp.where(is_g, jnp.exp(gl - gmax), 0.0), axis=-1, keepdims=True)
    e_lo = N_EXPERT_GROUPS + EXPERTS_PER_GROUP * g_sel
    in_grp = jnp.where(lane >= e_lo, jnp.where(lane < e_lo + EXPERTS_PER_GROUP, 1, 0), 0) == 1
    el = jnp.where(in_grp, logits, neg)
    m1 = jnp.max(el, axis=-1, keepdims=True)
    i1 = jnp.min(jnp.where(el == m1, lane, big), axis=-1, keepdims=True)
    el2 = jnp.where(lane == i1, neg, el)
    m2 = jnp.max(el2, axis=-1, keepdims=True)
    i2 = jnp.min(jnp.where(el2 == m2, lane, big), axis=-1, keepdims=True)
    t = jnp.exp(m2 - m1)
    inv = 1.0 / (1.0 + t)
    w0 = g_p * inv
    w1 = g_p * t * inv
    e0 = (i1 - N_EXPERT_GROUPS).astype(F32)
    e1 = (i2 - N_EXPERT_GROUPS).astype(F32)

    oh0 = jnp.where(lane == i1, 1.0, 0.0)
    oh1 = jnp.where(lane == i2, 1.0, 0.0)
    oh = oh0 + oh1
    before = jnp.dot(tri_ref[...], oh.astype(BF16), preferred_element_type=F32) + carry_ref[...]
    r0 = jnp.sum(before * oh0, axis=-1, keepdims=True)
    r1 = jnp.sum(before * oh1, axis=-1, keepdims=True)
    total = carry_ref[...] + jnp.sum(oh, axis=0, keepdims=True)
    carry_ref[...] = total
    cnt_ref[...] = total

    route_ref[0] = jnp.where(lane == 0, e0, jnp.where(lane == 1, e1, jnp.where(lane == 2, r0, jnp.where(lane == 3, r1, 0.0))))
    w0_ref[0] = jnp.broadcast_to(w0, logits.shape)
    w1_ref[0] = jnp.broadcast_to(w1, logits.shape)


def _merge(x, f_out, a_out, w, counts0):
    B, S, _ = x.shape
    tm = TOKEN_TILE
    tok = lambda b, i: (b, i, 0)
    out_shape = [
        jax.ShapeDtypeStruct((B, S, D_MODEL), F32),
        jax.ShapeDtypeStruct((B, S, ROW_TILES, LANES), BF16),
        jax.ShapeDtypeStruct((B, S, LANES), F32),
        jax.ShapeDtypeStruct((B, S, LANES), F32),
        jax.ShapeDtypeStruct((B, S, LANES), F32),
        jax.ShapeDtypeStruct((1, LANES), F32),
    ]
    in_specs = [
        pl.BlockSpec((1, tm, D_MODEL), tok),
        pl.BlockSpec((1, tm, FOURIER_WIDTH), tok),
        pl.BlockSpec((1, tm, MLA_WIDTH), tok),
        _const_spec((1, FOURIER_WIDTH)),
        _const_spec((1, MLA_WIDTH)),
        _const_spec((D_MODEL, D_MODEL)),
        _const_spec((1, D_MODEL)),
        _const_spec((D_MODEL, 2 * LANES)),
        _const_spec((1, LANES)),
        _const_spec((tm, tm)),
        _const_spec((1, LANES)),
    ]
    out_specs = [
        pl.BlockSpec((1, tm, D_MODEL), tok),
        pl.BlockSpec((1, tm, ROW_TILES, LANES), lambda b, i: (b, i, 0, 0)),
        pl.BlockSpec((1, tm, LANES), tok),
        pl.BlockSpec((1, tm, LANES), tok),
        pl.BlockSpec((1, tm, LANES), tok),
        pl.BlockSpec((1, LANES), lambda b, i: (0, 0)),
    ]
    tri = (lax.broadcasted_iota(jnp.int32, (tm, tm), 1) < lax.broadcasted_iota(jnp.int32, (tm, tm), 0)).astype(BF16)
    return pl.pallas_call(
        _merge_kernel, grid=(B, S // tm), in_specs=in_specs, out_specs=out_specs, out_shape=out_shape,
        scratch_shapes=[pltpu.VMEM((1, LANES), F32)],
        compiler_params=pltpu.CompilerParams(
            dimension_semantics=("arbitrary", "arbitrary"), vmem_limit_bytes=VMEM_LIMIT),
        name="merge",
    )(x, f_out, a_out, w['g_f'], w['g_a'], w['w_out'], w['g_ffn'], w['wr'], w['b_r'], tri, counts0)


def _dispatch_kernel(blk_end_ref, nused_ref, dest_ref, hn_a_ref, hn_b_ref, xs_ref, zero_buf, sem, zsem, *, n_a):
    i = pl.program_id(0)
    tm = hn_a_ref.shape[0]
    nb = xs_ref.shape[0] // EXPERT_TILE

    @pl.when(i == 0)
    def _():
        zero_buf[...] = jnp.zeros(zero_buf.shape, BF16)

        def zero_block(j):
            return pltpu.make_async_copy(zero_buf, xs_ref.at[pl.ds(j * EXPERT_TILE, EXPERT_TILE)], zsem)

        def each_group(fn):
            for e in range(N_EXPERTS):
                end = blk_end_ref[e]
                start = blk_end_ref[e - 1] if e else 0

                @pl.when(end > start)
                def _():
                    fn(end - 1)

        def each_tail(fn):
            def body(j, carry):
                fn(j)
                return carry

            lax.fori_loop(nused_ref[0], nb, body, 0)

        each_group(lambda j: zero_block(j).start())
        each_tail(lambda j: zero_block(j).start())
        each_group(lambda j: zero_block(j).wait())
        each_tail(lambda j: zero_block(j).wait())

    def scatter(src_ref):
        def body(r, carry):
            for k in range(TOP_K):
                d = dest_ref[0, 0, TOP_K * r + k]
                pltpu.make_async_copy(src_ref.at[r], xs_ref.at[d], sem).start(priority=k)
            return carry

        lax.fori_loop(0, tm, body, 0)
        for _ in range(TOP_K):
            pltpu.make_async_copy(src_ref, xs_ref.at[pl.ds(0, tm)], sem).wait()

    @pl.when(i < n_a)
    def _():
        scatter(hn_a_ref)

    @pl.when(i >= n_a)
    def _():
        scatter(hn_b_ref)


def _dispatch(blk_end, n_used, dest, hn_a, hn_b, n_rows):
    tm = TOKEN_TILE
    n_a, n_b = hn_a.shape[0] // tm, hn_b.shape[0] // tm
    tile = (tm, ROW_TILES, LANES)
    grid_spec = pltpu.PrefetchScalarGridSpec(
        num_scalar_prefetch=2, grid=(n_a + n_b,),
        in_specs=[pl.BlockSpec((1, 1, TOP_K * tm), lambda i, be, nu: (i, 0, 0), memory_space=pltpu.SMEM),
                  pl.BlockSpec(tile, lambda i, be, nu: (jnp.minimum(i, n_a - 1), 0, 0)),
                  pl.BlockSpec(tile, lambda i, be, nu: (jnp.maximum(i - n_a, 0), 0, 0))],
        out_specs=pl.BlockSpec(memory_space=pl.ANY),
        scratch_shapes=[pltpu.VMEM((EXPERT_TILE, ROW_TILES, LANES), BF16),
                        pltpu.SemaphoreType.DMA(()), pltpu.SemaphoreType.DMA(())],
    )
    return pl.pallas_call(
        functools.partial(_dispatch_kernel, n_a=n_a), grid_spec=grid_spec,
        out_shape=jax.ShapeDtypeStruct((n_rows, ROW_TILES, LANES), BF16),
        compiler_params=pltpu.CompilerParams(dimension_semantics=("arbitrary",), vmem_limit_bytes=VMEM_LIMIT),
        name="dispatch",
    )(blk_end, n_used, dest.reshape(n_a + n_b, 1, TOP_K * tm), hn_a, hn_b)


def _expert_kernel(be_ref, nused_ref, xs_ref, wg_ref, wu_ref, wd_ref, y_ref, wgu_s, wd_s):
    i = pl.program_id(0)

    @pl.when((i == 0) | (be_ref[i] != be_ref[jnp.maximum(i - 1, 0)]))
    def _():
        wgu_s[:, :EXPERT_DIM] = wg_ref[0].astype(BF16)
        wgu_s[:, EXPERT_DIM:] = wu_ref[0].astype(BF16)
        wd_s[...] = wd_ref[0].astype(BF16)

    @pl.when(i < nused_ref[0])
    def _():
        tm = xs_ref.shape[0]
        x = xs_ref[...].reshape(tm, D_MODEL)
        gu = jnp.dot(x, wgu_s[...], preferred_element_type=F32)
        g = gu[:, :EXPERT_DIM]
        u = gu[:, EXPERT_DIM:]
        hmid = (g * jax.nn.sigmoid(g) * u).astype(BF16)
        y = jnp.dot(hmid, wd_s[...], preferred_element_type=F32)
        y_ref[...] = y.reshape(tm, ROW_TILES, LANES).astype(BF16)

    @pl.when(i >= nused_ref[0])
    def _():
        y_ref[...] = jnp.zeros(y_ref.shape, BF16)


def _experts(block_expert, n_used, xs, w):
    n_rows = xs.shape[0]
    tm = EXPERT_TILE
    nb = n_rows // tm
    row = lambda i, be, nu: (jnp.minimum(i, nu[0] - 1), 0, 0)
    out_row = lambda i, be, nu: (i, 0, 0)
    wsel = lambda i, be, nu: (be[i], 0, 0)
    grid_spec = pltpu.PrefetchScalarGridSpec(
        num_scalar_prefetch=2, grid=(nb,),
        in_specs=[pl.BlockSpec((tm, ROW_TILES, LANES), row),
                  pl.BlockSpec((1, D_MODEL, EXPERT_DIM), wsel),
                  pl.BlockSpec((1, D_MODEL, EXPERT_DIM), wsel),
                  pl.BlockSpec((1, EXPERT_DIM, D_MODEL), wsel)],
        out_specs=pl.BlockSpec((tm, ROW_TILES, LANES), out_row),
        scratch_shapes=[pltpu.VMEM((D_MODEL, 2 * EXPERT_DIM), BF16), pltpu.VMEM((EXPERT_DIM, D_MODEL), BF16)],
    )
    return pl.pallas_call(
        _expert_kernel, grid_spec=grid_spec,
        out_shape=jax.ShapeDtypeStruct((n_rows, ROW_TILES, LANES), BF16),
        compiler_params=pltpu.CompilerParams(dimension_semantics=("arbitrary",), vmem_limit_bytes=VMEM_LIMIT),
        name="experts",
    )(block_expert, n_used, xs, w['w_gate'], w['w_up'], w['w_down'])


def _combine_kernel(dest_ref, dest_next_ref, h_ref, w0_ref, w1_ref, g_ref, y_ref, o_ref, ybuf, sems, *, n_steps):
    tm = h_ref.shape[0]
    i = pl.program_id(0)
    slot = lax.rem(i, 2)

    def start_gather(d_ref, s):
        def body(r, carry):
            for k in range(TOP_K):
                d = d_ref[0, 0, TOP_K * r + k]
                pltpu.make_async_copy(y_ref.at[d], ybuf.at[s, k, r], sems.at[s]).start(priority=k)
            return carry

        lax.fori_loop(0, tm, body, 0)

    @pl.when(i == 0)
    def _():
        start_gather(dest_ref, 0)

    @pl.when(i + 1 < n_steps)
    def _():
        start_gather(dest_next_ref, 1 - slot)

    for k in range(TOP_K):
        pltpu.make_async_copy(y_ref.at[pl.ds(0, tm)], ybuf.at[slot, k], sems.at[slot]).wait()

    y0 = ybuf[slot, 0].reshape(tm, D_MODEL)
    y1 = ybuf[slot, 1].reshape(tm, D_MODEL)
    w0 = w0_ref[...]
    w1 = w1_ref[...]
    cols = []
    for j in range(ROW_TILES):
        sl = slice(j * LANES, (j + 1) * LANES)
        cols.append(h_ref[:, sl] + w0 * y0[:, sl].astype(F32) + w1 * y1[:, sl].astype(F32))
    hs = jnp.concatenate(cols, axis=-1)
    o_ref[...] = _rms(hs, g_ref[...])


def _combine(dest, h, w0, w1, g_final, y):
    T = h.shape[0]
    tm = TOKEN_TILE
    n = T // tm
    dest3 = dest.reshape(n, 1, TOP_K * tm)
    return pl.pallas_call(
        functools.partial(_combine_kernel, n_steps=n), grid=(n,),
        in_specs=[pl.BlockSpec((1, 1, TOP_K * tm), lambda i: (i, 0, 0), memory_space=pltpu.SMEM),
                  pl.BlockSpec((1, 1, TOP_K * tm), lambda i: (jnp.minimum(i + 1, n - 1), 0, 0),
                               memory_space=pltpu.SMEM),
                  pl.BlockSpec((tm, D_MODEL), lambda i: (i, 0)),
                  pl.BlockSpec((tm, LANES), lambda i: (i, 0)),
                  pl.BlockSpec((tm, LANES), lambda i: (i, 0)),
                  _const_spec((1, D_MODEL)),
                  pl.BlockSpec(memory_space=pl.ANY)],
        out_specs=pl.BlockSpec((tm, D_MODEL), lambda i: (i, 0)),
        out_shape=jax.ShapeDtypeStruct((T, D_MODEL), F32),
        scratch_shapes=[pltpu.VMEM((2, TOP_K, tm, ROW_TILES, LANES), BF16), pltpu.SemaphoreType.DMA((2,))],
        compiler_params=pltpu.CompilerParams(dimension_semantics=("arbitrary",), vmem_limit_bytes=VMEM_LIMIT),
        name="combine",
    )(dest3, dest3, h, w0, w1, g_final, y)


def _tables(S):
    pos = jnp.arange(S, dtype=F32)
    inv_freq = 1.0 / (ROPE_THETA ** (jnp.arange(0, QK_ROPE_DIM, 2, dtype=F32) / QK_ROPE_DIM))
    ang = pos[:, None] * inv_freq[None, :]
    cos, sin = jnp.cos(ang), jnp.sin(ang)
    cos2 = jnp.concatenate([cos, cos], axis=-1)
    sin2 = jnp.concatenate([sin, sin], axis=-1)
    zp = jnp.zeros((S, LANES - QK_ROPE_DIM), F32)
    t = {
        'cos_t': cos2.T, 'sin_t': sin2.T,
        'cos_p': jnp.concatenate([cos2, zp], axis=-1), 'sin_p': jnp.concatenate([sin2, zp], axis=-1),
    }

    def phase(rows, cols, n):
        m = (rows[:, None] * cols[None, :]) % n
        a = m.astype(F32) * jnp.float32(2.0 * math.pi / n)
        return jnp.cos(a), jnp.sin(a)

    i1 = jnp.arange(DFT_N1, dtype=jnp.int32)
    c1, s1 = phase(i1, i1, DFT_N1)
    t['dft_m1'] = jnp.concatenate(
        [jnp.concatenate([c1, s1], axis=1), jnp.concatenate([-s1, c1], axis=1)], axis=0).astype(BF16)
    k = jnp.arange(S, dtype=jnp.int32)
    n2 = jnp.arange(DFT_N2, dtype=jnp.int32)
    cg, sg = phase(k, n2, S)
    g = jnp.concatenate([cg, sg], axis=1) * jnp.float32(S ** -0.5)
    t['dft_g'] = g.reshape(DFT_N2, DFT_N1, 2 * DFT_N2).transpose(1, 0, 2).astype(BF16)
    return t


def _layer_weights(g_attn_norm, w_in, g_q_latent, w_uq, g_kv_latent, w_ukv, g_out_fourier, g_out_mla, w_out,
                   g_ffn_norm, w_router_group, b_router_group, w_router_expert, b_router_expert,
                   w_gate, w_up, w_down):
    half = QK_ROPE_DIM // 2
    s3 = FOURIER_WIDTH + Q_LORA_RANK + KV_LORA_RANK
    w_kr = w_in[:, s3:s3 + QK_ROPE_DIM]
    w_kr_rot = jnp.concatenate([-w_kr[:, half:], w_kr[:, :half]], axis=1)
    zc = jnp.zeros((D_MODEL, LANES - QK_ROPE_DIM), F32)
    w_in_ext = jnp.concatenate([w_in[:, :s3], w_kr, zc, w_kr_rot, zc], axis=1)

    c = jnp.arange(FOURIER_GROUP_DIM, dtype=jnp.int32)
    m = (c[:, None] * c[None, :]) % FOURIER_GROUP_DIM
    a = m.astype(F32) * jnp.float32(2.0 * math.pi / FOURIER_GROUP_DIM)
    cs = jnp.concatenate([jnp.cos(a), -jnp.sin(a)], axis=1) * jnp.float32(FOURIER_GROUP_DIM ** -0.5)

    qscale = jnp.float32(QK_HEAD_DIM ** -0.5 * math.log2(math.e))
    wq = (w_uq * qscale).reshape(Q_LORA_RANK, N_HEADS, QK_HEAD_DIM)
    wq_n = wq[:, :, :QK_NOPE_DIM]
    wq_r = wq[:, :, QK_NOPE_DIM:]
    wq_rot = jnp.concatenate([-wq_r[:, :, half:], wq_r[:, :, :half]], axis=2)
    to_t = lambda t: t.reshape(Q_LORA_RANK, -1).T
    wkv = w_ukv.reshape(KV_LORA_RANK, N_HEADS, QK_NOPE_DIM + V_HEAD_DIM)
    w_r = jnp.concatenate(
        [w_router_group, w_router_expert,
         jnp.zeros((D_MODEL, LANES - N_EXPERT_GROUPS - N_EXPERTS), F32)], axis=1)
    wr_hi = w_r.astype(BF16)
    b_r = jnp.concatenate([b_router_group, b_router_expert,
                           jnp.zeros((LANES - N_EXPERT_GROUPS - N_EXPERTS,), F32)])
    return {
        'g_attn': g_attn_norm[None, :], 'w_in': w_in_ext.astype(BF16), 'cs': cs.astype(BF16),
        'g_q': g_q_latent[None, :], 'g_kv': g_kv_latent[None, :],
        'wqn_t': to_t(wq_n).astype(BF16), 'wqr_t': to_t(wq_r).astype(BF16), 'wqrot_t': to_t(wq_rot).astype(BF16),
        'wuk': wkv[:, :, :QK_NOPE_DIM].reshape(KV_LORA_RANK, -1).astype(BF16),
        'wuv_t': wkv[:, :, QK_NOPE_DIM:].reshape(KV_LORA_RANK, -1).T.astype(BF16),
        'g_f': g_out_fourier[None, :], 'g_a': g_out_mla[None, :], 'w_out': w_out.astype(BF16),
        'g_ffn': g_ffn_norm[None, :],
        'wr': jnp.concatenate([wr_hi, (w_r - wr_hi.astype(F32)).astype(BF16)], axis=1),
        'b_r': b_r[None, :],
        'w_gate': w_gate, 'w_up': w_up, 'w_down': w_down,
    }


def _block_plan(counts, n_assign):
    counts = counts[0, N_EXPERT_GROUPS:N_EXPERT_GROUPS + N_EXPERTS].astype(jnp.int32)
    nblk = (counts + EXPERT_TILE - 1) // EXPERT_TILE
    blk_end = jnp.cumsum(nblk)
    blk_start = blk_end - nblk
    n_blocks = (n_assign + N_EXPERTS * (EXPERT_TILE - 1) + EXPERT_TILE - 1) // EXPERT_TILE
    n_used = blk_end[-1]
    j = jnp.minimum(jnp.arange(n_blocks, dtype=jnp.int32), n_used - 1)
    block_expert = jnp.minimum(jnp.sum((blk_end[None, :] <= j[:, None]).astype(jnp.int32), axis=1), N_EXPERTS - 1)
    return blk_start, blk_end, block_expert, n_used[None], n_blocks * EXPERT_TILE


def _dest_rows(route, blk_start):
    e = route[:, :TOP_K].astype(jnp.int32)
    rank = route[:, TOP_K:2 * TOP_K].astype(jnp.int32)
    onehot = e[:, :, None] == jnp.arange(N_EXPERTS, dtype=jnp.int32)[None, None, :]
    first = jnp.sum(jnp.where(onehot, blk_start[None, None, :], 0), axis=-1)
    return (first * EXPERT_TILE + rank).reshape(-1)


def _mixers(x, wt, counts0):
    B, S, _ = x.shape
    T = B * S
    xr, xi, qt, kx, vt = _proj(x, wt)
    f_out = _seq_dft(xr, xi, wt)
    a_out = _attention(qt, kx, vt)
    h, hn3, route, w0, w1, counts = _merge(x, f_out, a_out, wt, counts0)
    flat = lambda t: t.reshape((T,) + t.shape[2:])
    return flat(h), flat(hn3), flat(route), flat(w0), flat(w1), counts


def kernel(x_prompt, x_sample, g_attn_norm, w_in, g_q_latent, w_uq, g_kv_latent, w_ukv, g_out_fourier, g_out_mla, w_out, g_ffn_norm, w_router_group, b_router_group, w_router_expert, b_router_expert, w_gate, w_up, w_down, g_final):
    assert g_attn_norm.shape[0] == 1, "single-layer configuration"
    w = _layer_weights(g_attn_norm[0], w_in[0], g_q_latent[0], w_uq[0], g_kv_latent[0], w_ukv[0],
                       g_out_fourier[0], g_out_mla[0], w_out[0], g_ffn_norm[0], w_router_group[0],
                       b_router_group[0], w_router_expert[0], b_router_expert[0], w_gate[0], w_up[0], w_down[0])
    assert x_prompt.shape[1] == x_sample.shape[1]
    w.update(_tables(x_prompt.shape[1]))

    counts = jnp.zeros((1, LANES), F32)
    batches = []
    for x in (x_prompt, x_sample):
        h, hn3, route, w0, w1, counts = _mixers(x, w, counts)
        batches.append((x.shape, h, hn3, route, w0, w1))

    n_assign = sum(b[1].shape[0] for b in batches) * TOP_K
    blk_start, blk_end, block_expert, n_used, n_rows = _block_plan(counts, n_assign)
    dests = [_dest_rows(b[3], blk_start) for b in batches]
    xs = _dispatch(blk_end, n_used, jnp.concatenate(dests), batches[0][2], batches[1][2], n_rows)
    y = _experts(block_expert, n_used, xs, w)
    outs = []
    for (shape, h, _, _, w0, w1), dest in zip(batches, dests):
        outs.append(_combine(dest, h, w0, w1, g_final[None, :], y).reshape(shape))
    return tuple(outs)
```

```python
import functools
import math

import jax
import jax.numpy as jnp
import numpy as np
from jax import lax
from jax.experimental import pallas as pl
from jax.experimental.pallas import tpu as pltpu

D_MODEL = 2048
FOURIER_WIDTH = 1024
N_FOURIER_GROUPS = 4
FOURIER_GROUP_DIM = 256
MLA_WIDTH = 1024
V_HEAD_DIM = 128
N_HEADS = 8
QK_NOPE_DIM = 128
QK_ROPE_DIM = 64
QK_HEAD_DIM = 192
Q_LORA_RANK = 512
KV_LORA_RANK = 256
ROPE_THETA = 10000.0
N_EXPERT_GROUPS = 4
EXPERTS_PER_GROUP = 8
N_EXPERTS = 32
TOP_K = 2
EXPERT_DIM = 512
EPS = 1e-6

LANES = 128
QK_PAD = 256
ROW_TILES = D_MODEL // LANES

DFT_N1 = 64
DFT_N2 = 128

TOKEN_TILE = 512
ATTN_TQ = 1024
ATTN_TK = 512
ATTN_SUB = 8
DFT_ROWS = 16
EXPERT_TILE = 256
VMEM_LIMIT = 56 * 1024 * 1024

F32 = jnp.float32
BF16 = jnp.bfloat16


def _const_spec(shape):
    nd = len(shape)
    return pl.BlockSpec(shape, lambda *_: (0,) * nd, pipeline_mode=pl.Buffered(1))


def _rms(x, g):
    return x * lax.rsqrt(jnp.mean(x * x, axis=-1, keepdims=True) + EPS) * g


def _proj_kernel(x_ref, g_attn_ref, w_in_ref, cs_ref, g_q_ref, g_kv_ref, wqn_ref, wqr_ref,
                 wuk_ref, wuvt_ref, cos_t_ref, sin_t_ref, cos_p_ref, sin_p_ref,
                 xr_ref, xi_ref, qt_ref, kx_ref, vt_ref):
    tm = x_ref.shape[1]
    x = x_ref[0]
    u = _rms(x, g_attn_ref[...]).astype(BF16)
    z = jnp.dot(u, w_in_ref[...], preferred_element_type=F32)

    zf = z[:, :FOURIER_WIDTH].astype(BF16)
    xr, xi = [], []
    for g in range(N_FOURIER_GROUPS):
        zg = zf[:, g * FOURIER_GROUP_DIM:(g + 1) * FOURIER_GROUP_DIM]
        xg = jnp.dot(zg, cs_ref[...], preferred_element_type=F32)
        xr.append(xg[:, :FOURIER_GROUP_DIM])
        xi.append(xg[:, FOURIER_GROUP_DIM:])
    xr_ref[0] = jnp.concatenate(xr, axis=-1).astype(BF16)
    xi_ref[0] = jnp.concatenate(xi, axis=-1).astype(BF16)

    o = FOURIER_WIDTH
    q_lat = z[:, o:o + Q_LORA_RANK]
    o += Q_LORA_RANK
    kv_lat = z[:, o:o + KV_LORA_RANK]
    o += KV_LORA_RANK
    kr = z[:, o:o + LANES]
    kr_rot = z[:, o + LANES:o + 2 * LANES]

    qn = _rms(q_lat, g_q_ref[...]).astype(BF16)
    kvn = _rms(kv_lat, g_kv_ref[...]).astype(BF16)

    nt = (((1,), (1,)), ((), ()))
    qn_t = lax.dot_general(wqn_ref[...], qn, nt, preferred_element_type=F32)
    qr_t = lax.dot_general(wqr_ref[...], qn, nt, preferred_element_type=F32)
    cos_t = cos_t_ref[...]
    sin_t = sin_t_ref[...]
    zpad = jnp.zeros((QK_PAD - QK_HEAD_DIM, tm), BF16)
    half = QK_ROPE_DIM // 2
    for h in range(N_HEADS):
        r0 = h * QK_PAD
        qt_ref[0, r0:r0 + QK_NOPE_DIM, :] = qn_t[h * QK_NOPE_DIM:(h + 1) * QK_NOPE_DIM].astype(BF16)
        qr = qr_t[h * QK_ROPE_DIM:(h + 1) * QK_ROPE_DIM]
        rot = jnp.concatenate([-qr[half:], qr[:half]], axis=0)
        roped = qr * cos_t + rot * sin_t
        qt_ref[0, r0 + QK_NOPE_DIM:r0 + QK_HEAD_DIM, :] = roped.astype(BF16)
        qt_ref[0, r0 + QK_HEAD_DIM:r0 + QK_PAD, :] = zpad

    k_nope = jnp.dot(kvn, wuk_ref[...], preferred_element_type=F32)
    k_rope = (kr * cos_p_ref[...] + kr_rot * sin_p_ref[...]).astype(BF16)
    for h in range(N_HEADS):
        kx_ref[0, h, :, :QK_NOPE_DIM] = k_nope[:, h * QK_NOPE_DIM:(h + 1) * QK_NOPE_DIM].astype(BF16)
        kx_ref[0, h, :, QK_NOPE_DIM:] = k_rope

    v_t = lax.dot_general(wuvt_ref[...], kvn, nt, preferred_element_type=F32)
    vt_ref[0, 0] = v_t.astype(BF16).reshape(N_HEADS, V_HEAD_DIM, tm)


def _proj(x, w):
    B, S, _ = x.shape
    tm = ATTN_TK
    grid = (B, S // tm)
    tok = lambda b, i: (b, i, 0)
    out_shape = [
        jax.ShapeDtypeStruct((B, S, FOURIER_WIDTH), BF16),
        jax.ShapeDtypeStruct((B, S, FOURIER_WIDTH), BF16),
        jax.ShapeDtypeStruct((B, N_HEADS * QK_PAD, S), BF16),
        jax.ShapeDtypeStruct((B, N_HEADS, S, QK_PAD), BF16),
        jax.ShapeDtypeStruct((B, S // tm, N_HEADS, V_HEAD_DIM, tm), BF16),
    ]
    in_specs = [
        pl.BlockSpec((1, tm, D_MODEL), tok),
        _const_spec((1, D_MODEL)),
        _const_spec(w['w_in'].shape),
        _const_spec(w['cs'].shape),
        _const_spec((1, Q_LORA_RANK)),
        _const_spec((1, KV_LORA_RANK)),
        _const_spec(w['wqn_t'].shape),
        _const_spec(w['wqr_t'].shape),
        _const_spec(w['wuk'].shape),
        _const_spec(w['wuv_t'].shape),
        pl.BlockSpec((QK_ROPE_DIM, tm), lambda b, i: (0, i)),
        pl.BlockSpec((QK_ROPE_DIM, tm), lambda b, i: (0, i)),
        pl.BlockSpec((tm, LANES), lambda b, i: (i, 0)),
        pl.BlockSpec((tm, LANES), lambda b, i: (i, 0)),
    ]
    out_specs = [
        pl.BlockSpec((1, tm, FOURIER_WIDTH), tok),
        pl.BlockSpec((1, tm, FOURIER_WIDTH), tok),
        pl.BlockSpec((1, N_HEADS * QK_PAD, tm), lambda b, i: (b, 0, i)),
        pl.BlockSpec((1, N_HEADS, tm, QK_PAD), lambda b, i: (b, 0, i, 0)),
        pl.BlockSpec((1, 1, N_HEADS, V_HEAD_DIM, tm), lambda b, i: (b, i, 0, 0, 0)),
    ]
    return pl.pallas_call(
        _proj_kernel, grid=grid, in_specs=in_specs, out_specs=out_specs, out_shape=out_shape,
        compiler_params=pltpu.CompilerParams(
            dimension_semantics=("arbitrary", "arbitrary"), vmem_limit_bytes=VMEM_LIMIT),
        name="proj",
    )(x, w['g_attn'], w['w_in'], w['cs'], w['g_q'], w['g_kv'], w['wqn_t'], w['wqr_t'],
      w['wuk'], w['wuv_t'], w['cos_t'], w['sin_t'], w['cos_p'], w['sin_p'])


def _dft1_kernel(m1_ref, xr_ref, xi_ref, zr_ref, zi_ref):
    nb, c = xr_ref.shape[2], xr_ref.shape[3]
    xr = xr_ref[0].reshape(DFT_N1, nb * c)
    xi = xi_ref[0].reshape(DFT_N1, nb * c)
    xc = jnp.concatenate([xr, xi], axis=0)
    z = jnp.dot(m1_ref[...], xc, preferred_element_type=F32).astype(BF16)
    zr_ref[0] = z[:DFT_N1].reshape(DFT_N1, nb, c)
    zi_ref[0] = z[DFT_N1:].reshape(DFT_N1, nb, c)


def _dft2_kernel(g_ref, zr_ref, zi_ref, o_ref):
    kb, c = zr_ref.shape[1], zr_ref.shape[3]
    ys = []
    for j in range(kb):
        zc = jnp.concatenate([zr_ref[0, j], zi_ref[0, j]], axis=0)
        ys.append(jnp.dot(g_ref[j], zc, preferred_element_type=F32).astype(BF16))
    o_ref[0] = jnp.concatenate(ys, axis=-1).reshape(DFT_N2, kb, c)


def _seq_dft(xr, xi, w):
    B, S, C = xr.shape
    assert S == DFT_N1 * DFT_N2
    nb = DFT_ROWS
    xr4 = xr.reshape(B, DFT_N1, DFT_N2, C)
    xi4 = xi.reshape(B, DFT_N1, DFT_N2, C)
    rows = pl.BlockSpec((1, DFT_N1, nb, C), lambda b, j: (b, 0, j, 0))
    zr, zi = pl.pallas_call(
        _dft1_kernel, grid=(B, DFT_N2 // nb),
        in_specs=[_const_spec((2 * DFT_N1, 2 * DFT_N1)), rows, rows],
        out_specs=[rows, rows],
        out_shape=[jax.ShapeDtypeStruct((B, DFT_N1, DFT_N2, C), BF16)] * 2,
        compiler_params=pltpu.CompilerParams(
            dimension_semantics=("arbitrary", "arbitrary"), vmem_limit_bytes=VMEM_LIMIT),
        name="dft1",
    )(w['dft_m1'], xr4, xi4)
    kb = DFT_ROWS
    blk = pl.BlockSpec((1, kb, DFT_N2, C), lambda b, k: (b, k, 0, 0))
    y = pl.pallas_call(
        _dft2_kernel, grid=(B, DFT_N1 // kb),
        in_specs=[pl.BlockSpec((kb, DFT_N2, 2 * DFT_N2), lambda b, k: (k, 0, 0)), blk, blk],
        out_specs=pl.BlockSpec((1, DFT_N2, kb, C), lambda b, k: (b, 0, k, 0)),
        out_shape=jax.ShapeDtypeStruct((B, DFT_N2, DFT_N1, C), BF16),
        compiler_params=pltpu.CompilerParams(
            dimension_semantics=("arbitrary", "arbitrary"), vmem_limit_bytes=VMEM_LIMIT),
        name="dft2",
    )(w['dft_g'], zr, zi)
    return y.reshape(B, S, C)


def _attn_kernel(q_ref, k_ref, v_ref, o_ref, s_scr, acc_ref):
    tk = v_ref.shape[-1]
    nch = v_ref.shape[1]
    tq = q_ref.shape[-1]
    q_t = q_ref[0]

    def scores(c):
        k_c = k_ref[0, 0, pl.ds(pl.multiple_of(c * tk, tk), tk), :]
        return jnp.dot(k_c, q_t, preferred_element_type=F32)

    def step(c, slot, m, l, s_max):
        s_next = scores(jnp.where(c + 1 >= nch, 0, c + 1))
        s_scr[1 - slot] = s_next
        next_max = jnp.max(s_next, axis=0, keepdims=True)
        s = s_scr[slot]
        m_new = jnp.maximum(m, s_max)
        alpha = jnp.exp2(m - m_new)
        p = jnp.exp2(s - m_new)
        l = alpha * l + jnp.sum(p, axis=0, keepdims=True)
        pv = jnp.dot(v_ref[0, c, 0], p.astype(BF16), preferred_element_type=F32)
        acc_ref[...] = alpha * acc_ref[...] + pv
        return m_new, l, next_max

    acc_ref[...] = jnp.zeros(acc_ref.shape, F32)
    s_first = scores(0)
    s_scr[0] = s_first

    def body(g, carry):
        m, l, s_max = carry
        for si in range(ATTN_SUB):
            m, l, s_max = step(g * ATTN_SUB + si, si % 2, m, l, s_max)
        return m, l, s_max

    m0 = jnp.full((1, tq), -1e30, F32)
    l0 = jnp.zeros((1, tq), F32)
    _, l, _ = lax.fori_loop(0, nch // ATTN_SUB, body, (m0, l0, jnp.max(s_first, axis=0, keepdims=True)))
    o_t = acc_ref[...] * (1.0 / l)
    o_ref[0] = o_t.T.astype(BF16)


def _attention(qt, kx, vt):
    B, _, S = qt.shape
    nch, tk = vt.shape[1], vt.shape[-1]
    tq = ATTN_TQ
    assert ATTN_SUB % 2 == 0 and nch % ATTN_SUB == 0
    return pl.pallas_call(
        _attn_kernel, grid=(B, N_HEADS, S // tq),
        in_specs=[pl.BlockSpec((1, QK_PAD, tq), lambda b, h, i: (b, h, i)),
                  pl.BlockSpec((1, 1, S, QK_PAD), lambda b, h, i: (b, h, 0, 0)),
                  pl.BlockSpec((1, nch, 1, V_HEAD_DIM, tk), lambda b, h, i: (b, 0, h, 0, 0))],
        out_specs=pl.BlockSpec((1, tq, V_HEAD_DIM), lambda b, h, i: (b, i, h)),
        out_shape=jax.ShapeDtypeStruct((B, S, MLA_WIDTH), BF16),
        scratch_shapes=[pltpu.VMEM((2, tk, tq), F32), pltpu.VMEM((V_HEAD_DIM, tq), F32)],
        compiler_params=pltpu.CompilerParams(
            dimension_semantics=("arbitrary", "arbitrary", "arbitrary"), vmem_limit_bytes=VMEM_LIMIT),
        name="attn",
    )(qt, kx, vt)


def _merge_kernel(x_ref, f_ref, a_ref, g_f_ref, g_a_ref, w_out_ref, g_ffn_ref, wr_ref, b_r_ref, tri_ref, cnt0_ref,
                  h_ref, hn_ref, route_ref, w0_ref, w1_ref, cnt_ref, carry_ref):
    tm = x_ref.shape[1]

    @pl.when((pl.program_id(0) == 0) & (pl.program_id(1) == 0))
    def _():
        carry_ref[...] = cnt0_ref[...]

    half = tri_ref.shape[0]
    carry = carry_ref[...]
    for r in range(0, tm, half):
        carry = _merge_rows(slice(r, r + half), carry, x_ref, f_ref, a_ref, g_f_ref, g_a_ref, w_out_ref,
                            g_ffn_ref, wr_ref, b_r_ref, tri_ref, h_ref, hn_ref, route_ref, w0_ref, w1_ref)
    carry_ref[...] = carry
    cnt_ref[...] = carry


def _merge_rows(rows, carry, x_ref, f_ref, a_ref, g_f_ref, g_a_ref, w_out_ref, g_ffn_ref, wr_ref, b_r_ref, tri_ref,
                h_ref, hn_ref, route_ref, w0_ref, w1_ref):
    n = rows.stop - rows.start
    fn = _rms(f_ref[0, rows].astype(F32), g_f_ref[...]).astype(BF16)
    an = _rms(a_ref[0, rows].astype(F32), g_a_ref[...]).astype(BF16)
    merged = jnp.concatenate([fn, an], axis=-1)
    h = x_ref[0, rows] + jnp.dot(merged, w_out_ref[...], preferred_element_type=F32)
    h_ref[0, rows] = h
    hn = _rms(h, g_ffn_ref[...])
    hn_ref[0, rows] = hn.reshape(n, ROW_TILES, LANES).astype(BF16)

    hn_hi = hn.astype(BF16)
    hn_lo = (hn - hn_hi.astype(F32)).astype(BF16)
    p_hi = jnp.dot(hn_hi, wr_ref[...], preferred_element_type=F32)
    p_lo = jnp.dot(hn_lo, wr_ref[...], preferred_element_type=F32)
    logits = ((p_hi[:, :LANES] + p_hi[:, LANES:]) + (p_lo[:, :LANES] + p_lo[:, LANES:])
              + b_r_ref[...])

    lane = lax.broadcasted_iota(jnp.int32, logits.shape, 1)
    neg = jnp.float32(-1e30)
    big = jnp.int32(LANES)
    is_g = lane < N_EXPERT_GROUPS
    gl = jnp.where(is_g, logits, neg)
    gmax = jnp.max(gl, axis=-1, keepdims=True)
    g_sel = jnp.min(jnp.where(gl == gmax, lane, big), axis=-1, keepdims=True)
    g_p = 1.0 / jnp.sum(jnp.where(is_g, jnp.exp(gl - gmax), 0.0), axis=-1, keepdims=True)
    e_lo = N_EXPERT_GROUPS + EXPERTS_PER_GROUP * g_sel
    in_grp = jnp.where(lane >= e_lo, jnp.where(lane < e_lo + EXPERTS_PER_GROUP, 1, 0), 0) == 1
    el = jnp.where(in_grp, logits, neg)
    m1 = jnp.max(el, axis=-1, keepdims=True)
    i1 = jnp.min(jnp.where(el == m1, lane, big), axis=-1, keepdims=True)
    el2 = jnp.where(lane == i1, neg, el)
    m2 = jnp.max(el2, axis=-1, keepdims=True)
    i2 = jnp.min(jnp.where(el2 == m2, lane, big), axis=-1, keepdims=True)
    t = jnp.exp(m2 - m1)
    inv = 1.0 / (1.0 + t)
    w0 = g_p * inv
    w1 = g_p * t * inv
    e0 = (i1 - N_EXPERT_GROUPS).astype(F32)
    e1 = (i2 - N_EXPERT_GROUPS).astype(F32)

    oh0 = jnp.where(lane == i1, 1.0, 0.0)
    oh1 = jnp.where(lane == i2, 1.0, 0.0)
    oh = oh0 + oh1
    before = jnp.dot(tri_ref[...], oh.astype(BF16), preferred_element_type=F32) + carry
    r0 = jnp.sum(before * oh0, axis=-1, keepdims=True)
    r1 = jnp.sum(before * oh1, axis=-1, keepdims=True)

    route_ref[0, rows] = jnp.where(lane == 0, e0, jnp.where(lane == 1, e1, jnp.where(lane == 2, r0, jnp.where(lane == 3, r1, 0.0))))
    w0_ref[0, rows] = jnp.broadcast_to(w0, logits.shape)
    w1_ref[0, rows] = jnp.broadcast_to(w1, logits.shape)
    return carry + jnp.sum(oh, axis=0, keepdims=True)


def _merge(x, f_out, a_out, w, counts0):
    B, S, _ = x.shape
    tm = TOKEN_TILE
    half = tm
    tok = lambda b, i: (b, i, 0)
    out_shape = [
        jax.ShapeDtypeStruct((B, S, D_MODEL), F32),
        jax.ShapeDtypeStruct((B, S, ROW_TILES, LANES), BF16),
        jax.ShapeDtypeStruct((B, S, LANES), F32),
        jax.ShapeDtypeStruct((B, S, LANES), F32),
        jax.ShapeDtypeStruct((B, S, LANES), F32),
        jax.ShapeDtypeStruct((1, LANES), F32),
    ]
    in_specs = [
        pl.BlockSpec((1, tm, D_MODEL), tok),
        pl.BlockSpec((1, tm, FOURIER_WIDTH), tok),
        pl.BlockSpec((1, tm, MLA_WIDTH), tok),
        _const_spec((1, FOURIER_WIDTH)),
        _const_spec((1, MLA_WIDTH)),
        _const_spec((D_MODEL, D_MODEL)),
        _const_spec((1, D_MODEL)),
        _const_spec((D_MODEL, 2 * LANES)),
        _const_spec((1, LANES)),
        _const_spec((half, half)),
        _const_spec((1, LANES)),
    ]
    out_specs = [
        pl.BlockSpec((1, tm, D_MODEL), tok),
        pl.BlockSpec((1, tm, ROW_TILES, LANES), lambda b, i: (b, i, 0, 0)),
        pl.BlockSpec((1, tm, LANES), tok),
        pl.BlockSpec((1, tm, LANES), tok),
        pl.BlockSpec((1, tm, LANES), tok),
        pl.BlockSpec((1, LANES), lambda b, i: (0, 0)),
    ]
    tri = (lax.broadcasted_iota(jnp.int32, (half, half), 1)
           < lax.broadcasted_iota(jnp.int32, (half, half), 0)).astype(BF16)
    return pl.pallas_call(
        _merge_kernel, grid=(B, S // tm), in_specs=in_specs, out_specs=out_specs, out_shape=out_shape,
        scratch_shapes=[pltpu.VMEM((1, LANES), F32)],
        compiler_params=pltpu.CompilerParams(
            dimension_semantics=("arbitrary", "arbitrary"), vmem_limit_bytes=VMEM_LIMIT),
        name="merge",
    )(x, f_out, a_out, w['g_f'], w['g_a'], w['w_out'], w['g_ffn'], w['wr'], w['b_r'], tri, counts0)


def _dispatch_kernel(blk_end_ref, nused_ref, dest_ref, hn_a_ref, hn_b_ref, xs_ref, zero_buf, sem, zsem, *, n_a):
    i = pl.program_id(0)
    tm = hn_a_ref.shape[0]
    nb = xs_ref.shape[0] // EXPERT_TILE

    @pl.when(i == 0)
    def _():
        zero_buf[...] = jnp.zeros(zero_buf.shape, BF16)

        def zero_block(j):
            return pltpu.make_async_copy(zero_buf, xs_ref.at[pl.ds(j * EXPERT_TILE, EXPERT_TILE)], zsem)

        def each_group(fn):
            for e in range(N_EXPERTS):
                end = blk_end_ref[e]
                start = blk_end_ref[e - 1] if e else 0

                @pl.when(end > start)
                def _():
                    fn(end - 1)

        def each_tail(fn):
            def body(j, carry):
                fn(j)
                return carry

            lax.fori_loop(nused_ref[0], nb, body, 0)

        each_group(lambda j: zero_block(j).start())
        each_tail(lambda j: zero_block(j).start())
        each_group(lambda j: zero_block(j).wait())
        each_tail(lambda j: zero_block(j).wait())

    def scatter(src_ref):
        def body(r, carry):
            for k in range(TOP_K):
                d = dest_ref[0, 0, TOP_K * r + k]
                pltpu.make_async_copy(src_ref.at[r], xs_ref.at[d], sem).start(priority=k)
            return carry

        lax.fori_loop(0, tm, body, 0)
        for _ in range(TOP_K):
            pltpu.make_async_copy(src_ref, xs_ref.at[pl.ds(0, tm)], sem).wait()

    @pl.when(i < n_a)
    def _():
        scatter(hn_a_ref)

    @pl.when(i >= n_a)
    def _():
        scatter(hn_b_ref)


def _dispatch(blk_end, n_used, dest, hn_a, hn_b, n_rows):
    tm = TOKEN_TILE
    n_a, n_b = hn_a.shape[0] // tm, hn_b.shape[0] // tm
    tile = (tm, ROW_TILES, LANES)
    grid_spec = pltpu.PrefetchScalarGridSpec(
        num_scalar_prefetch=2, grid=(n_a + n_b,),
        in_specs=[pl.BlockSpec((1, 1, TOP_K * tm), lambda i, be, nu: (i, 0, 0), memory_space=pltpu.SMEM),
                  pl.BlockSpec(tile, lambda i, be, nu: (jnp.minimum(i, n_a - 1), 0, 0)),
                  pl.BlockSpec(tile, lambda i, be, nu: (jnp.maximum(i - n_a, 0), 0, 0))],
        out_specs=pl.BlockSpec(memory_space=pl.ANY),
        scratch_shapes=[pltpu.VMEM((EXPERT_TILE, ROW_TILES, LANES), BF16),
                        pltpu.SemaphoreType.DMA(()), pltpu.SemaphoreType.DMA(())],
    )
    return pl.pallas_call(
        functools.partial(_dispatch_kernel, n_a=n_a), grid_spec=grid_spec,
        out_shape=jax.ShapeDtypeStruct((n_rows, ROW_TILES, LANES), BF16),
        compiler_params=pltpu.CompilerParams(dimension_semantics=("arbitrary",), vmem_limit_bytes=VMEM_LIMIT),
        name="dispatch",
    )(blk_end, n_used, dest.reshape(n_a + n_b, 1, TOP_K * tm), hn_a, hn_b)


def _expert_kernel(be_ref, nused_ref, blk_end_ref, xs_ref, wg_ref, wu_ref, wd_ref, y_ref,
                   wg_f, wu_f, wd_f, wgu_s, wd_s, group_ref, sems):
    i = pl.program_id(0)
    e = be_ref[i]

    def weight_copies(expert, slot):
        return [pltpu.make_async_copy(src.at[expert], dst.at[slot], sems.at[slot])
                for src, dst in ((wg_ref, wg_f), (wu_ref, wu_f), (wd_ref, wd_f))]

    @pl.when(i == 0)
    def _():
        group_ref[0] = 0
        for cp in weight_copies(e, 0):
            cp.start()

    @pl.when((i == 0) | (e != be_ref[jnp.maximum(i - 1, 0)]))
    def _():
        slot = lax.rem(group_ref[0], 2)
        for cp in weight_copies(e, slot):
            cp.wait()
        wgu_s[:, :EXPERT_DIM] = wg_f[slot].astype(BF16)
        wgu_s[:, EXPERT_DIM:] = wu_f[slot].astype(BF16)
        wd_s[...] = wd_f[slot].astype(BF16)
        nxt = blk_end_ref[e]

        @pl.when(nxt < nused_ref[0])
        def _():
            for cp in weight_copies(be_ref[nxt], 1 - slot):
                cp.start()

        group_ref[0] = group_ref[0] + 1

    @pl.when(i < nused_ref[0])
    def _():
        tm = xs_ref.shape[0]
        x = xs_ref[...].reshape(tm, D_MODEL)
        gu = jnp.dot(x, wgu_s[...], preferred_element_type=F32)
        g = gu[:, :EXPERT_DIM]
        u = gu[:, EXPERT_DIM:]
        hmid = (g * jax.nn.sigmoid(g) * u).astype(BF16)
        y = jnp.dot(hmid, wd_s[...], preferred_element_type=F32)
        y_ref[...] = y.reshape(tm, ROW_TILES, LANES).astype(BF16)

    @pl.when(i >= nused_ref[0])
    def _():
        y_ref[...] = jnp.zeros(y_ref.shape, BF16)


def _experts(block_expert, n_used, blk_end, xs, w):
    n_rows = xs.shape[0]
    tm = EXPERT_TILE
    nb = n_rows // tm
    row = lambda i, be, nu, bend: (jnp.minimum(i, nu[0] - 1), 0, 0)
    out_row = lambda i, be, nu, bend: (i, 0, 0)
    hbm = pl.BlockSpec(memory_space=pl.ANY)
    grid_spec = pltpu.PrefetchScalarGridSpec(
        num_scalar_prefetch=3, grid=(nb,),
        in_specs=[pl.BlockSpec((tm, ROW_TILES, LANES), row), hbm, hbm, hbm],
        out_specs=pl.BlockSpec((tm, ROW_TILES, LANES), out_row),
        scratch_shapes=[pltpu.VMEM((2, D_MODEL, EXPERT_DIM), F32), pltpu.VMEM((2, D_MODEL, EXPERT_DIM), F32),
                        pltpu.VMEM((2, EXPERT_DIM, D_MODEL), F32),
                        pltpu.VMEM((D_MODEL, 2 * EXPERT_DIM), BF16), pltpu.VMEM((EXPERT_DIM, D_MODEL), BF16),
                        pltpu.SMEM((1,), jnp.int32), pltpu.SemaphoreType.DMA((2,))],
    )
    return pl.pallas_call(
        _expert_kernel, grid_spec=grid_spec,
        out_shape=jax.ShapeDtypeStruct((n_rows, ROW_TILES, LANES), BF16),
        compiler_params=pltpu.CompilerParams(dimension_semantics=("arbitrary",), vmem_limit_bytes=VMEM_LIMIT),
        name="experts",
    )(block_expert, n_used, blk_end, xs, w['w_gate'], w['w_up'], w['w_down'])


def _combine_kernel(dest_ref, dest_next_ref, h_ref, w0_ref, w1_ref, g_ref, y_ref, o_ref, ybuf, sems, *, n_steps):
    tm = h_ref.shape[0]
    i = pl.program_id(0)
    slot = lax.rem(i, 2)

    def start_gather(d_ref, s):
        def body(r, carry):
            for k in range(TOP_K):
                d = d_ref[0, 0, TOP_K * r + k]
                pltpu.make_async_copy(y_ref.at[d], ybuf.at[s, k, r], sems.at[s]).start(priority=k)
            return carry

        lax.fori_loop(0, tm, body, 0)

    @pl.when(i == 0)
    def _():
        start_gather(dest_ref, 0)

    @pl.when(i + 1 < n_steps)
    def _():
        start_gather(dest_next_ref, 1 - slot)

    for k in range(TOP_K):
        pltpu.make_async_copy(y_ref.at[pl.ds(0, tm)], ybuf.at[slot, k], sems.at[slot]).wait()

    y0 = ybuf[slot, 0].reshape(tm, D_MODEL)
    y1 = ybuf[slot, 1].reshape(tm, D_MODEL)
    w0 = w0_ref[...]
    w1 = w1_ref[...]
    cols = []
    for j in range(ROW_TILES):
        sl = slice(j * LANES, (j + 1) * LANES)
        cols.append(h_ref[:, sl] + w0 * y0[:, sl].astype(F32) + w1 * y1[:, sl].astype(F32))
    hs = jnp.concatenate(cols, axis=-1)
    o_ref[...] = _rms(hs, g_ref[...])


def _combine(dest, h, w0, w1, g_final, y):
    T = h.shape[0]
    tm = TOKEN_TILE
    n = T // tm
    dest3 = dest.reshape(n, 1, TOP_K * tm)
    return pl.pallas_call(
        functools.partial(_combine_kernel, n_steps=n), grid=(n,),
        in_specs=[pl.BlockSpec((1, 1, TOP_K * tm), lambda i: (i, 0, 0), memory_space=pltpu.SMEM),
                  pl.BlockSpec((1, 1, TOP_K * tm), lambda i: (jnp.minimum(i + 1, n - 1), 0, 0),
                               memory_space=pltpu.SMEM),
                  pl.BlockSpec((tm, D_MODEL), lambda i: (i, 0)),
                  pl.BlockSpec((tm, LANES), lambda i: (i, 0)),
                  pl.BlockSpec((tm, LANES), lambda i: (i, 0)),
                  _const_spec((1, D_MODEL)),
                  pl.BlockSpec(memory_space=pl.ANY)],
        out_specs=pl.BlockSpec((tm, D_MODEL), lambda i: (i, 0)),
        out_shape=jax.ShapeDtypeStruct((T, D_MODEL), F32),
        scratch_shapes=[pltpu.VMEM((2, TOP_K, tm, ROW_TILES, LANES), BF16), pltpu.SemaphoreType.DMA((2,))],
        compiler_params=pltpu.CompilerParams(dimension_semantics=("arbitrary",), vmem_limit_bytes=VMEM_LIMIT),
        name="combine",
    )(dest3, dest3, h, w0, w1, g_final, y)


def _tables(S):
    pos = jnp.arange(S, dtype=F32)
    inv_freq = 1.0 / (ROPE_THETA ** (jnp.arange(0, QK_ROPE_DIM, 2, dtype=F32) / QK_ROPE_DIM))
    ang = pos[:, None] * inv_freq[None, :]
    cos, sin = jnp.cos(ang), jnp.sin(ang)
    cos2 = jnp.concatenate([cos, cos], axis=-1)
    sin2 = jnp.concatenate([sin, sin], axis=-1)
    zp = jnp.zeros((S, LANES - QK_ROPE_DIM), F32)
    t = {
        'cos_t': cos2.T, 'sin_t': sin2.T,
        'cos_p': jnp.concatenate([cos2, zp], axis=-1), 'sin_p': jnp.concatenate([sin2, zp], axis=-1),
    }

    def phase(rows, cols, n):
        m = (rows[:, None] * cols[None, :]) % n
        a = m.astype(F32) * jnp.float32(2.0 * math.pi / n)
        return jnp.cos(a), jnp.sin(a)

    i1 = jnp.arange(DFT_N1, dtype=jnp.int32)
    c1, s1 = phase(i1, i1, DFT_N1)
    t['dft_m1'] = jnp.concatenate(
        [jnp.concatenate([c1, s1], axis=1), jnp.concatenate([-s1, c1], axis=1)], axis=0).astype(BF16)
    k = jnp.arange(S, dtype=jnp.int32)
    n2 = jnp.arange(DFT_N2, dtype=jnp.int32)
    cg, sg = phase(k, n2, S)
    g = jnp.concatenate([cg, sg], axis=1) * jnp.float32(S ** -0.5)
    t['dft_g'] = g.reshape(DFT_N2, DFT_N1, 2 * DFT_N2).transpose(1, 0, 2).astype(BF16)
    return t


def _layer_weights(g_attn_norm, w_in, g_q_latent, w_uq, g_kv_latent, w_ukv, g_out_fourier, g_out_mla, w_out,
                   g_ffn_norm, w_router_group, b_router_group, w_router_expert, b_router_expert,
                   w_gate, w_up, w_down):
    half = QK_ROPE_DIM // 2
    s3 = FOURIER_WIDTH + Q_LORA_RANK + KV_LORA_RANK
    w_kr = w_in[:, s3:s3 + QK_ROPE_DIM]
    w_kr_rot = jnp.concatenate([-w_kr[:, half:], w_kr[:, :half]], axis=1)
    zc = jnp.zeros((D_MODEL, LANES - QK_ROPE_DIM), F32)
    w_in_ext = jnp.concatenate([w_in[:, :s3], w_kr, zc, w_kr_rot, zc], axis=1)

    c = jnp.arange(FOURIER_GROUP_DIM, dtype=jnp.int32)
    m = (c[:, None] * c[None, :]) % FOURIER_GROUP_DIM
    a = m.astype(F32) * jnp.float32(2.0 * math.pi / FOURIER_GROUP_DIM)
    cs = jnp.concatenate([jnp.cos(a), -jnp.sin(a)], axis=1) * jnp.float32(FOURIER_GROUP_DIM ** -0.5)

    qscale = jnp.float32(QK_HEAD_DIM ** -0.5 * math.log2(math.e))
    wq = (w_uq * qscale).reshape(Q_LORA_RANK, N_HEADS, QK_HEAD_DIM)
    wq_n = wq[:, :, :QK_NOPE_DIM]
    wq_r = wq[:, :, QK_NOPE_DIM:]
    wq_rot = jnp.concatenate([-wq_r[:, :, half:], wq_r[:, :, :half]], axis=2)
    to_t = lambda t: t.reshape(Q_LORA_RANK, -1).T
    wkv = w_ukv.reshape(KV_LORA_RANK, N_HEADS, QK_NOPE_DIM + V_HEAD_DIM)
    w_r = jnp.concatenate(
        [w_router_group, w_router_expert,
         jnp.zeros((D_MODEL, LANES - N_EXPERT_GROUPS - N_EXPERTS), F32)], axis=1)
    wr_hi = w_r.astype(BF16)
    b_r = jnp.concatenate([b_router_group, b_router_expert,
                           jnp.zeros((LANES - N_EXPERT_GROUPS - N_EXPERTS,), F32)])
    return {
        'g_attn': g_attn_norm[None, :], 'w_in': w_in_ext.astype(BF16), 'cs': cs.astype(BF16),
        'g_q': g_q_latent[None, :], 'g_kv': g_kv_latent[None, :],
        'wqn_t': to_t(wq_n).astype(BF16), 'wqr_t': to_t(wq_r).astype(BF16),
        'wuk': wkv[:, :, :QK_NOPE_DIM].reshape(KV_LORA_RANK, -1).astype(BF16),
        'wuv_t': wkv[:, :, QK_NOPE_DIM:].reshape(KV_LORA_RANK, -1).T.astype(BF16),
        'g_f': g_out_fourier[None, :], 'g_a': g_out_mla[None, :], 'w_out': w_out.astype(BF16),
        'g_ffn': g_ffn_norm[None, :],
        'wr': jnp.concatenate([wr_hi, (w_r - wr_hi.astype(F32)).astype(BF16)], axis=1),
        'b_r': b_r[None, :],
        'w_gate': w_gate, 'w_up': w_up, 'w_down': w_down,
    }


def _block_plan(counts, n_assign):
    counts = counts[0, N_EXPERT_GROUPS:N_EXPERT_GROUPS + N_EXPERTS].astype(jnp.int32)
    nblk = (counts + EXPERT_TILE - 1) // EXPERT_TILE
    blk_end = jnp.cumsum(nblk)
    blk_start = blk_end - nblk
    n_blocks = (n_assign + N_EXPERTS * (EXPERT_TILE - 1) + EXPERT_TILE - 1) // EXPERT_TILE
    n_used = blk_end[-1]
    j = jnp.minimum(jnp.arange(n_blocks, dtype=jnp.int32), n_used - 1)
    block_expert = jnp.minimum(jnp.sum((blk_end[None, :] <= j[:, None]).astype(jnp.int32), axis=1), N_EXPERTS - 1)
    return blk_start, blk_end, block_expert, n_used[None], n_blocks * EXPERT_TILE


def _dest_rows(route, blk_start):
    e = route[:, :TOP_K].astype(jnp.int32)
    rank = route[:, TOP_K:2 * TOP_K].astype(jnp.int32)
    onehot = e[:, :, None] == jnp.arange(N_EXPERTS, dtype=jnp.int32)[None, None, :]
    first = jnp.sum(jnp.where(onehot, blk_start[None, None, :], 0), axis=-1)
    return (first * EXPERT_TILE + rank).reshape(-1)


def _mixers(x, wt, counts0):
    B, S, _ = x.shape
    T = B * S
    xr, xi, qt, kx, vt = _proj(x, wt)
    f_out = _seq_dft(xr, xi, wt)
    a_out = _attention(qt, kx, vt)
    h, hn3, route, w0, w1, counts = _merge(x, f_out, a_out, wt, counts0)
    flat = lambda t: t.reshape((T,) + t.shape[2:])
    return flat(h), flat(hn3), flat(route), flat(w0), flat(w1), counts


def kernel(x_prompt, x_sample, g_attn_norm, w_in, g_q_latent, w_uq, g_kv_latent, w_ukv, g_out_fourier, g_out_mla, w_out, g_ffn_norm, w_router_group, b_router_group, w_router_expert, b_router_expert, w_gate, w_up, w_down, g_final):
    assert g_attn_norm.shape[0] == 1, "single-layer configuration"
    w = _layer_weights(g_attn_norm[0], w_in[0], g_q_latent[0], w_uq[0], g_kv_latent[0], w_ukv[0],
                       g_out_fourier[0], g_out_mla[0], w_out[0], g_ffn_norm[0], w_router_group[0],
                       b_router_group[0], w_router_expert[0], b_router_expert[0], w_gate[0], w_up[0], w_down[0])
    assert x_prompt.shape[1] == x_sample.shape[1]
    w.update(_tables(x_prompt.shape[1]))

    counts = jnp.zeros((1, LANES), F32)
    batches = []
    for x in (x_prompt, x_sample):
        h, hn3, route, w0, w1, counts = _mixers(x, w, counts)
        batches.append((x.shape, h, hn3, route, w0, w1))

    n_assign = sum(b[1].shape[0] for b in batches) * TOP_K
    blk_start, blk_end, block_expert, n_used, n_rows = _block_plan(counts, n_assign)
    dests = [_dest_rows(b[3], blk_start) for b in batches]
    xs = _dispatch(blk_end, n_used, jnp.concatenate(dests), batches[0][2], batches[1][2], n_rows)
    y = _experts(block_expert, n_used, blk_end, xs, w)
    outs = []
    for (shape, h, _, _, w0, w1), dest in zip(batches, dests):
        outs.append(_combine(dest, h, w0, w1, g_final[None, :], y).reshape(shape))
    return tuple(outs)
```

```python
import functools
import math

import jax
import jax.numpy as jnp
import numpy as np
from jax import lax
from jax.experimental import pallas as pl
from jax.experimental.pallas import tpu as pltpu

D_MODEL = 2048
FOURIER_WIDTH = 1024
N_FOURIER_GROUPS = 4
FOURIER_GROUP_DIM = 256
MLA_WIDTH = 1024
V_HEAD_DIM = 128
N_HEADS = 8
QK_NOPE_DIM = 128
QK_ROPE_DIM = 64
QK_HEAD_DIM = 192
Q_LORA_RANK = 512
KV_LORA_RANK = 256
ROPE_THETA = 10000.0
N_EXPERT_GROUPS = 4
EXPERTS_PER_GROUP = 8
N_EXPERTS = 32
TOP_K = 2
EXPERT_DIM = 512
EPS = 1e-6

LANES = 128
QK_PAD = 256
ROW_TILES = D_MODEL // LANES
ROUTE_ROWS = 8

DFT_N1 = 64
DFT_N2 = 128

TOKEN_TILE = 512
ATTN_TQ = 1024
ATTN_TK = 512
ATTN_SUB = 8
ATTN_QBLOCKS = 4
DFT_ROWS = 16
EXPERT_TILE = 256
VMEM_LIMIT = 56 * 1024 * 1024

F32 = jnp.float32
BF16 = jnp.bfloat16


def _const_spec(shape):
    nd = len(shape)
    return pl.BlockSpec(shape, lambda *_: (0,) * nd, pipeline_mode=pl.Buffered(1))


def _rms(x, g):
    return x * lax.rsqrt(jnp.mean(x * x, axis=-1, keepdims=True) + EPS) * g


def _proj_kernel(x_ref, g_attn_ref, w_in_ref, cs_ref, g_q_ref, g_kv_ref, wqn_ref, wqr_ref,
                 wuk_ref, wuvt_ref, cos_t_ref, sin_t_ref, cos_p_ref, sin_p_ref,
                 xr_ref, xi_ref, qt_ref, kx_ref, vt_ref):
    tm = x_ref.shape[1]
    x = x_ref[0]
    u = _rms(x, g_attn_ref[...]).astype(BF16)
    z = jnp.dot(u, w_in_ref[...], preferred_element_type=F32)

    zf = z[:, :FOURIER_WIDTH].astype(BF16)
    xr, xi = [], []
    for g in range(N_FOURIER_GROUPS):
        zg = zf[:, g * FOURIER_GROUP_DIM:(g + 1) * FOURIER_GROUP_DIM]
        xg = jnp.dot(zg, cs_ref[...], preferred_element_type=F32)
        xr.append(xg[:, :FOURIER_GROUP_DIM])
        xi.append(xg[:, FOURIER_GROUP_DIM:])
    xr_ref[0] = jnp.concatenate(xr, axis=-1).astype(BF16)
    xi_ref[0] = jnp.concatenate(xi, axis=-1).astype(BF16)

    o = FOURIER_WIDTH
    q_lat = z[:, o:o + Q_LORA_RANK]
    o += Q_LORA_RANK
    kv_lat = z[:, o:o + KV_LORA_RANK]
    o += KV_LORA_RANK
    kr = z[:, o:o + LANES]
    kr_rot = z[:, o + LANES:o + 2 * LANES]

    qn = _rms(q_lat, g_q_ref[...]).astype(BF16)
    kvn = _rms(kv_lat, g_kv_ref[...]).astype(BF16)

    nt = (((1,), (1,)), ((), ()))
    qn_t = lax.dot_general(wqn_ref[...], qn, nt, preferred_element_type=F32)
    qr_t = lax.dot_general(wqr_ref[...], qn, nt, preferred_element_type=F32)
    cos_t = cos_t_ref[...]
    sin_t = sin_t_ref[...]
    zpad = jnp.zeros((QK_PAD - QK_HEAD_DIM, tm), BF16)
    half = QK_ROPE_DIM // 2
    for h in range(N_HEADS):
        r0 = h * QK_PAD
        qt_ref[0, 0, r0:r0 + QK_NOPE_DIM, :] = qn_t[h * QK_NOPE_DIM:(h + 1) * QK_NOPE_DIM].astype(BF16)
        qr = qr_t[h * QK_ROPE_DIM:(h + 1) * QK_ROPE_DIM]
        rot = jnp.concatenate([-qr[half:], qr[:half]], axis=0)
        roped = qr * cos_t + rot * sin_t
        qt_ref[0, 0, r0 + QK_NOPE_DIM:r0 + QK_HEAD_DIM, :] = roped.astype(BF16)
        qt_ref[0, 0, r0 + QK_HEAD_DIM:r0 + QK_PAD, :] = zpad

    k_nope = jnp.dot(kvn, wuk_ref[...], preferred_element_type=F32)
    k_rope = (kr * cos_p_ref[...] + kr_rot * sin_p_ref[...]).astype(BF16)
    for h in range(N_HEADS):
        kx_ref[0, h, :, :QK_NOPE_DIM] = k_nope[:, h * QK_NOPE_DIM:(h + 1) * QK_NOPE_DIM].astype(BF16)
        kx_ref[0, h, :, QK_NOPE_DIM:] = k_rope

    v_t = lax.dot_general(wuvt_ref[...], kvn, nt, preferred_element_type=F32)
    vt_ref[0, 0] = v_t.astype(BF16).reshape(N_HEADS, V_HEAD_DIM, tm)


def _proj(x, w):
    B, S, _ = x.shape
    tm = ATTN_TK
    q_split = ATTN_TQ // tm
    grid = (B, S // tm)
    tok = lambda b, i: (b, i, 0)
    out_shape = [
        jax.ShapeDtypeStruct((B, S, FOURIER_WIDTH), BF16),
        jax.ShapeDtypeStruct((B, S, FOURIER_WIDTH), BF16),
        jax.ShapeDtypeStruct((B, S // ATTN_TQ, N_HEADS * QK_PAD, ATTN_TQ), BF16),
        jax.ShapeDtypeStruct((B, N_HEADS, S, QK_PAD), BF16),
        jax.ShapeDtypeStruct((B, S // tm, N_HEADS, V_HEAD_DIM, tm), BF16),
    ]
    in_specs = [
        pl.BlockSpec((1, tm, D_MODEL), tok),
        _const_spec((1, D_MODEL)),
        _const_spec(w['w_in'].shape),
        _const_spec(w['cs'].shape),
        _const_spec((1, Q_LORA_RANK)),
        _const_spec((1, KV_LORA_RANK)),
        _const_spec(w['wqn_t'].shape),
        _const_spec(w['wqr_t'].shape),
        _const_spec(w['wuk'].shape),
        _const_spec(w['wuv_t'].shape),
        pl.BlockSpec((QK_ROPE_DIM, tm), lambda b, i: (0, i)),
        pl.BlockSpec((QK_ROPE_DIM, tm), lambda b, i: (0, i)),
        pl.BlockSpec((tm, LANES), lambda b, i: (i, 0)),
        pl.BlockSpec((tm, LANES), lambda b, i: (i, 0)),
    ]
    out_specs = [
        pl.BlockSpec((1, tm, FOURIER_WIDTH), tok),
        pl.BlockSpec((1, tm, FOURIER_WIDTH), tok),
        pl.BlockSpec((1, 1, N_HEADS * QK_PAD, tm), lambda b, i: (b, i // q_split, 0, i % q_split)),
        pl.BlockSpec((1, N_HEADS, tm, QK_PAD), lambda b, i: (b, 0, i, 0)),
        pl.BlockSpec((1, 1, N_HEADS, V_HEAD_DIM, tm), lambda b, i: (b, i, 0, 0, 0)),
    ]
    return pl.pallas_call(
        _proj_kernel, grid=grid, in_specs=in_specs, out_specs=out_specs, out_shape=out_shape,
        compiler_params=pltpu.CompilerParams(
            dimension_semantics=("arbitrary", "arbitrary"), vmem_limit_bytes=VMEM_LIMIT),
        name="proj",
    )(x, w['g_attn'], w['w_in'], w['cs'], w['g_q'], w['g_kv'], w['wqn_t'], w['wqr_t'],
      w['wuk'], w['wuv_t'], w['cos_t'], w['sin_t'], w['cos_p'], w['sin_p'])


def _dft1_kernel(m1_ref, xr_ref, xi_ref, zr_ref, zi_ref):
    nb, c = xr_ref.shape[2], xr_ref.shape[3]
    xr = xr_ref[0].reshape(DFT_N1, nb * c)
    xi = xi_ref[0].reshape(DFT_N1, nb * c)
    xc = jnp.concatenate([xr, xi], axis=0)
    z = jnp.dot(m1_ref[...], xc, preferred_element_type=F32).astype(BF16)
    zr_ref[0] = z[:DFT_N1].reshape(DFT_N1, nb, c)
    zi_ref[0] = z[DFT_N1:].reshape(DFT_N1, nb, c)


def _dft2_kernel(g_ref, zr_ref, zi_ref, o_ref):
    kb, c = zr_ref.shape[1], zr_ref.shape[3]
    ys = []
    for j in range(kb):
        zc = jnp.concatenate([zr_ref[0, j], zi_ref[0, j]], axis=0)
        ys.append(jnp.dot(g_ref[j], zc, preferred_element_type=F32).astype(BF16))
    o_ref[0] = jnp.concatenate(ys, axis=-1).reshape(DFT_N2, kb, c)


def _seq_dft(xr, xi, w):
    B, S, C = xr.shape
    assert S == DFT_N1 * DFT_N2
    nb = DFT_ROWS
    xr4 = xr.reshape(B, DFT_N1, DFT_N2, C)
    xi4 = xi.reshape(B, DFT_N1, DFT_N2, C)
    rows = pl.BlockSpec((1, DFT_N1, nb, C), lambda b, j: (b, 0, j, 0))
    zr, zi = pl.pallas_call(
        _dft1_kernel, grid=(B, DFT_N2 // nb),
        in_specs=[_const_spec((2 * DFT_N1, 2 * DFT_N1)), rows, rows],
        out_specs=[rows, rows],
        out_shape=[jax.ShapeDtypeStruct((B, DFT_N1, DFT_N2, C), BF16)] * 2,
        compiler_params=pltpu.CompilerParams(
            dimension_semantics=("arbitrary", "arbitrary"), vmem_limit_bytes=VMEM_LIMIT),
        name="dft1",
    )(w['dft_m1'], xr4, xi4)
    kb = DFT_ROWS
    blk = pl.BlockSpec((1, kb, DFT_N2, C), lambda b, k: (b, k, 0, 0))
    y = pl.pallas_call(
        _dft2_kernel, grid=(B, DFT_N1 // kb),
        in_specs=[pl.BlockSpec((kb, DFT_N2, 2 * DFT_N2), lambda b, k: (k, 0, 0)), blk, blk],
        out_specs=pl.BlockSpec((1, DFT_N2, kb, C), lambda b, k: (b, 0, k, 0)),
        out_shape=jax.ShapeDtypeStruct((B, DFT_N2, DFT_N1, C), BF16),
        compiler_params=pltpu.CompilerParams(
            dimension_semantics=("arbitrary", "arbitrary"), vmem_limit_bytes=VMEM_LIMIT),
        name="dft2",
    )(w['dft_g'], zr, zi)
    return y.reshape(B, S, C)


def _attn_kernel(q_ref, k_ref, v_ref, o_ref, s_scr, acc_ref):
    tk = v_ref.shape[-1]
    nch = v_ref.shape[1]
    nq, tq = q_ref.shape[1], q_ref.shape[-1]
    groups = nch // ATTN_SUB

    def scores(f):
        j = jnp.minimum(f // nch, nq - 1)
        c = lax.rem(f, nch)
        k_c = k_ref[0, 0, pl.ds(pl.multiple_of(c * tk, tk), tk), :]
        return jnp.dot(k_c, q_ref[0, j], preferred_element_type=F32)

    def step(f, slot, m, l, s_max):
        s_next = scores(f + 1)
        s_scr[1 - slot] = s_next
        next_max = jnp.max(s_next, axis=0, keepdims=True)
        s = s_scr[slot]
        m_new = jnp.maximum(m, s_max)
        alpha = jnp.exp2(m - m_new)
        p = jnp.exp2(s - m_new)
        l = alpha * l + jnp.sum(p, axis=0, keepdims=True)
        pv = jnp.dot(v_ref[0, lax.rem(f, nch), 0], p.astype(BF16), preferred_element_type=F32)
        acc_ref[...] = alpha * acc_ref[...] + pv
        return m_new, l, next_max

    acc_ref[...] = jnp.zeros(acc_ref.shape, F32)
    s_first = scores(0)
    s_scr[0] = s_first
    m0 = jnp.full((1, tq), -1e30, F32)
    l0 = jnp.zeros((1, tq), F32)

    def body(g, carry):
        m, l, s_max = carry
        for si in range(ATTN_SUB):
            m, l, s_max = step(g * ATTN_SUB + si, si % 2, m, l, s_max)
        block_done = lax.rem(g, groups) == groups - 1

        @pl.when(block_done)
        def _():
            j = g // groups
            o_t = acc_ref[...] * (1.0 / l)
            o_ref[0, pl.ds(pl.multiple_of(j * tq, tq), tq), :] = o_t.T.astype(BF16)
            acc_ref[...] = jnp.zeros(acc_ref.shape, F32)

        return jnp.where(block_done, m0, m), jnp.where(block_done, l0, l), s_max

    lax.fori_loop(0, nq * groups, body, (m0, l0, jnp.max(s_first, axis=0, keepdims=True)))


def _attention(qt, kx, vt):
    B, nqb, _, tq = qt.shape
    S = nqb * tq
    nch, tk = vt.shape[1], vt.shape[-1]
    nq = ATTN_QBLOCKS
    assert ATTN_SUB % 2 == 0 and nch % ATTN_SUB == 0 and nqb % nq == 0
    return pl.pallas_call(
        _attn_kernel, grid=(B, N_HEADS, nqb // nq),
        in_specs=[pl.BlockSpec((1, nq, QK_PAD, tq), lambda b, h, i: (b, i, h, 0)),
                  pl.BlockSpec((1, 1, S, QK_PAD), lambda b, h, i: (b, h, 0, 0)),
                  pl.BlockSpec((1, nch, 1, V_HEAD_DIM, tk), lambda b, h, i: (b, 0, h, 0, 0))],
        out_specs=pl.BlockSpec((1, nq * tq, V_HEAD_DIM), lambda b, h, i: (b, i, h)),
        out_shape=jax.ShapeDtypeStruct((B, S, MLA_WIDTH), BF16),
        scratch_shapes=[pltpu.VMEM((2, tk, tq), F32), pltpu.VMEM((V_HEAD_DIM, tq), F32)],
        compiler_params=pltpu.CompilerParams(
            dimension_semantics=("arbitrary", "arbitrary", "arbitrary"), vmem_limit_bytes=VMEM_LIMIT),
        name="attn",
    )(qt, kx, vt)


def _merge_kernel(x_ref, f_ref, a_ref, g_f_ref, g_a_ref, w_out_ref, g_ffn_ref, wr_ref, b_r_ref, tri_ref, cnt0_ref,
                  h_ref, hn_ref, route_ref, w0_ref, w1_ref, cnt_ref, carry_ref):
    tm = x_ref.shape[1]

    @pl.when((pl.program_id(0) == 0) & (pl.program_id(1) == 0))
    def _():
        carry_ref[...] = cnt0_ref[...]

    half = tri_ref.shape[0]
    carry = carry_ref[...]
    for r in range(0, tm, half):
        carry = _merge_rows(slice(r, r + half), carry, x_ref, f_ref, a_ref, g_f_ref, g_a_ref, w_out_ref,
                            g_ffn_ref, wr_ref, b_r_ref, tri_ref, h_ref, hn_ref, route_ref, w0_ref, w1_ref)
    carry_ref[...] = carry
    cnt_ref[...] = carry


def _merge_rows(rows, carry, x_ref, f_ref, a_ref, g_f_ref, g_a_ref, w_out_ref, g_ffn_ref, wr_ref, b_r_ref, tri_ref,
                h_ref, hn_ref, route_ref, w0_ref, w1_ref):
    n = rows.stop - rows.start
    fn = _rms(f_ref[0, rows].astype(F32), g_f_ref[...]).astype(BF16)
    an = _rms(a_ref[0, rows].astype(F32), g_a_ref[...]).astype(BF16)
    merged = jnp.concatenate([fn, an], axis=-1)
    h = x_ref[0, rows] + jnp.dot(merged, w_out_ref[...], preferred_element_type=F32)
    h_ref[0, rows] = h
    hn = _rms(h, g_ffn_ref[...])
    hn_ref[0, rows] = hn.reshape(n, ROW_TILES, LANES).astype(BF16)

    hn_hi = hn.astype(BF16)
    hn_lo = (hn - hn_hi.astype(F32)).astype(BF16)
    p_hi = jnp.dot(hn_hi, wr_ref[...], preferred_element_type=F32)
    p_lo = jnp.dot(hn_lo, wr_ref[...], preferred_element_type=F32)
    logits = ((p_hi[:, :LANES] + p_hi[:, LANES:]) + (p_lo[:, :LANES] + p_lo[:, LANES:])
              + b_r_ref[...])

    lane = lax.broadcasted_iota(jnp.int32, logits.shape, 1)
    neg = jnp.float32(-1e30)
    big = jnp.int32(LANES)
    is_g = lane < N_EXPERT_GROUPS
    gl = jnp.where(is_g, logits, neg)
    gmax = jnp.max(gl, axis=-1, keepdims=True)
    g_sel = jnp.min(jnp.where(gl == gmax, lane, big), axis=-1, keepdims=True)
    g_p = 1.0 / jnp.sum(jnp.where(is_g, jnp.exp(gl - gmax), 0.0), axis=-1, keepdims=True)
    e_lo = N_EXPERT_GROUPS + EXPERTS_PER_GROUP * g_sel
    in_grp = jnp.where(lane >= e_lo, jnp.where(lane < e_lo + EXPERTS_PER_GROUP, 1, 0), 0) == 1
    el = jnp.where(in_grp, logits, neg)
    m1 = jnp.max(el, axis=-1, keepdims=True)
    i1 = jnp.min(jnp.where(el == m1, lane, big), axis=-1, keepdims=True)
    el2 = jnp.where(lane == i1, neg, el)
    m2 = jnp.max(el2, axis=-1, keepdims=True)
    i2 = jnp.min(jnp.where(el2 == m2, lane, big), axis=-1, keepdims=True)
    t = jnp.exp(m2 - m1)
    inv = 1.0 / (1.0 + t)
    w0 = g_p * inv
    w1 = g_p * t * inv
    e0 = (i1 - N_EXPERT_GROUPS).astype(F32)
    e1 = (i2 - N_EXPERT_GROUPS).astype(F32)

    oh0 = jnp.where(lane == i1, 1.0, 0.0)
    oh1 = jnp.where(lane == i2, 1.0, 0.0)
    oh = oh0 + oh1
    before = jnp.dot(tri_ref[...], oh.astype(BF16), preferred_element_type=F32) + carry
    r0 = jnp.sum(before * oh0, axis=-1, keepdims=True)
    r1 = jnp.sum(before * oh1, axis=-1, keepdims=True)

    route = jnp.where(lane == 0, e0, jnp.where(lane == 1, e1, jnp.where(lane == 2, r0, jnp.where(lane == 3, r1, 0.0))))
    route_ref[0, :, rows] = route.T[:ROUTE_ROWS]
    w0_ref[0, rows] = jnp.broadcast_to(w0, logits.shape)
    w1_ref[0, rows] = jnp.broadcast_to(w1, logits.shape)
    return carry + jnp.sum(oh, axis=0, keepdims=True)


def _merge(x, f_out, a_out, w, counts0):
    B, S, _ = x.shape
    tm = TOKEN_TILE
    half = tm
    tok = lambda b, i: (b, i, 0)
    out_shape = [
        jax.ShapeDtypeStruct((B, S, D_MODEL), F32),
        jax.ShapeDtypeStruct((B, S, ROW_TILES, LANES), BF16),
        jax.ShapeDtypeStruct((B, ROUTE_ROWS, S), F32),
        jax.ShapeDtypeStruct((B, S, LANES), F32),
        jax.ShapeDtypeStruct((B, S, LANES), F32),
        jax.ShapeDtypeStruct((1, LANES), F32),
    ]
    in_specs = [
        pl.BlockSpec((1, tm, D_MODEL), tok),
        pl.BlockSpec((1, tm, FOURIER_WIDTH), tok),
        pl.BlockSpec((1, tm, MLA_WIDTH), tok),
        _const_spec((1, FOURIER_WIDTH)),
        _const_spec((1, MLA_WIDTH)),
        _const_spec((D_MODEL, D_MODEL)),
        _const_spec((1, D_MODEL)),
        _const_spec((D_MODEL, 2 * LANES)),
        _const_spec((1, LANES)),
        _const_spec((half, half)),
        _const_spec((1, LANES)),
    ]
    out_specs = [
        pl.BlockSpec((1, tm, D_MODEL), tok),
        pl.BlockSpec((1, tm, ROW_TILES, LANES), lambda b, i: (b, i, 0, 0)),
        pl.BlockSpec((1, ROUTE_ROWS, tm), lambda b, i: (b, 0, i)),
        pl.BlockSpec((1, tm, LANES), tok),
        pl.BlockSpec((1, tm, LANES), tok),
        pl.BlockSpec((1, LANES), lambda b, i: (0, 0)),
    ]
    tri = (lax.broadcasted_iota(jnp.int32, (half, half), 1)
           < lax.broadcasted_iota(jnp.int32, (half, half), 0)).astype(BF16)
    return pl.pallas_call(
        _merge_kernel, grid=(B, S // tm), in_specs=in_specs, out_specs=out_specs, out_shape=out_shape,
        scratch_shapes=[pltpu.VMEM((1, LANES), F32)],
        compiler_params=pltpu.CompilerParams(
            dimension_semantics=("arbitrary", "arbitrary"), vmem_limit_bytes=VMEM_LIMIT),
        name="merge",
    )(x, f_out, a_out, w['g_f'], w['g_a'], w['w_out'], w['g_ffn'], w['wr'], w['b_r'], tri, counts0)


def _dispatch_kernel(blk_end_ref, nused_ref, dest_ref, hn_a_ref, hn_b_ref, xs_ref, zero_buf, sem, zsem, *, n_a):
    i = pl.program_id(0)
    tm = hn_a_ref.shape[0]
    nb = xs_ref.shape[0] // EXPERT_TILE

    @pl.when(i == 0)
    def _():
        zero_buf[...] = jnp.zeros(zero_buf.shape, BF16)

        def zero_block(j):
            return pltpu.make_async_copy(zero_buf, xs_ref.at[pl.ds(j * EXPERT_TILE, EXPERT_TILE)], zsem)

        def each_group(fn):
            for e in range(N_EXPERTS):
                end = blk_end_ref[e]
                start = blk_end_ref[e - 1] if e else 0

                @pl.when(end > start)
                def _():
                    fn(end - 1)

        def each_tail(fn):
            def body(j, carry):
                fn(j)
                return carry

            lax.fori_loop(nused_ref[0], nb, body, 0)

        each_group(lambda j: zero_block(j).start())
        each_tail(lambda j: zero_block(j).start())
        each_group(lambda j: zero_block(j).wait())
        each_tail(lambda j: zero_block(j).wait())

    def scatter(src_ref):
        def body(r, carry):
            for k in range(TOP_K):
                d = dest_ref[0, 0, k * tm + r]
                pltpu.make_async_copy(src_ref.at[r], xs_ref.at[d], sem).start(priority=k)
            return carry

        lax.fori_loop(0, tm, body, 0)
        for _ in range(TOP_K):
            pltpu.make_async_copy(src_ref, xs_ref.at[pl.ds(0, tm)], sem).wait()

    @pl.when(i < n_a)
    def _():
        scatter(hn_a_ref)

    @pl.when(i >= n_a)
    def _():
        scatter(hn_b_ref)


def _dispatch(blk_end, n_used, dest, hn_a, hn_b, n_rows):
    tm = TOKEN_TILE
    n_a, n_b = hn_a.shape[0] // tm, hn_b.shape[0] // tm
    tile = (tm, ROW_TILES, LANES)
    grid_spec = pltpu.PrefetchScalarGridSpec(
        num_scalar_prefetch=2, grid=(n_a + n_b,),
        in_specs=[pl.BlockSpec((1, 1, TOP_K * tm), lambda i, be, nu: (i, 0, 0), memory_space=pltpu.SMEM),
                  pl.BlockSpec(tile, lambda i, be, nu: (jnp.minimum(i, n_a - 1), 0, 0)),
                  pl.BlockSpec(tile, lambda i, be, nu: (jnp.maximum(i - n_a, 0), 0, 0))],
        out_specs=pl.BlockSpec(memory_space=pl.ANY),
        scratch_shapes=[pltpu.VMEM((EXPERT_TILE, ROW_TILES, LANES), BF16),
                        pltpu.SemaphoreType.DMA(()), pltpu.SemaphoreType.DMA(())],
    )
    return pl.pallas_call(
        functools.partial(_dispatch_kernel, n_a=n_a), grid_spec=grid_spec,
        out_shape=jax.ShapeDtypeStruct((n_rows, ROW_TILES, LANES), BF16),
        compiler_params=pltpu.CompilerParams(dimension_semantics=("arbitrary",), vmem_limit_bytes=VMEM_LIMIT),
        name="dispatch",
    )(blk_end, n_used, dest, hn_a, hn_b)


def _expert_kernel(be_ref, nused_ref, blk_end_ref, xs_ref, wg_ref, wu_ref, wd_ref, y_ref,
                   wg_f, wu_f, wd_f, wgu_s, wd_s, group_ref, sems):
    i = pl.program_id(0)
    e = be_ref[i]

    def weight_copies(expert, slot):
        return [pltpu.make_async_copy(src.at[expert], dst.at[slot], sems.at[slot])
                for src, dst in ((wg_ref, wg_f), (wu_ref, wu_f), (wd_ref, wd_f))]

    @pl.when(i == 0)
    def _():
        group_ref[0] = 0
        for cp in weight_copies(e, 0):
            cp.start()

    @pl.when((i == 0) | (e != be_ref[jnp.maximum(i - 1, 0)]))
    def _():
        slot = lax.rem(group_ref[0], 2)
        for cp in weight_copies(e, slot):
            cp.wait()
        wgu_s[:, :EXPERT_DIM] = wg_f[slot].astype(BF16)
        wgu_s[:, EXPERT_DIM:] = wu_f[slot].astype(BF16)
        wd_s[...] = wd_f[slot].astype(BF16)
        nxt = blk_end_ref[e]

        @pl.when(nxt < nused_ref[0])
        def _():
            for cp in weight_copies(be_ref[nxt], 1 - slot):
                cp.start()

        group_ref[0] = group_ref[0] + 1

    @pl.when(i < nused_ref[0])
    def _():
        tm = xs_ref.shape[0]
        x = xs_ref[...].reshape(tm, D_MODEL)
        gu = jnp.dot(x, wgu_s[...], preferred_element_type=F32)
        g = gu[:, :EXPERT_DIM]
        u = gu[:, EXPERT_DIM:]
        hmid = (g * jax.nn.sigmoid(g) * u).astype(BF16)
        y = jnp.dot(hmid, wd_s[...], preferred_element_type=F32)
        y_ref[...] = y.reshape(tm, ROW_TILES, LANES).astype(BF16)

    @pl.when(i >= nused_ref[0])
    def _():
        y_ref[...] = jnp.zeros(y_ref.shape, BF16)


def _experts(block_expert, n_used, blk_end, xs, w):
    n_rows = xs.shape[0]
    tm = EXPERT_TILE
    nb = n_rows // tm
    row = lambda i, be, nu, bend: (jnp.minimum(i, nu[0] - 1), 0, 0)
    out_row = lambda i, be, nu, bend: (i, 0, 0)
    hbm = pl.BlockSpec(memory_space=pl.ANY)
    grid_spec = pltpu.PrefetchScalarGridSpec(
        num_scalar_prefetch=3, grid=(nb,),
        in_specs=[pl.BlockSpec((tm, ROW_TILES, LANES), row), hbm, hbm, hbm],
        out_specs=pl.BlockSpec((tm, ROW_TILES, LANES), out_row),
        scratch_shapes=[pltpu.VMEM((2, D_MODEL, EXPERT_DIM), F32), pltpu.VMEM((2, D_MODEL, EXPERT_DIM), F32),
                        pltpu.VMEM((2, EXPERT_DIM, D_MODEL), F32),
                        pltpu.VMEM((D_MODEL, 2 * EXPERT_DIM), BF16), pltpu.VMEM((EXPERT_DIM, D_MODEL), BF16),
                        pltpu.SMEM((1,), jnp.int32), pltpu.SemaphoreType.DMA((2,))],
    )
    return pl.pallas_call(
        _expert_kernel, grid_spec=grid_spec,
        out_shape=jax.ShapeDtypeStruct((n_rows, ROW_TILES, LANES), BF16),
        compiler_params=pltpu.CompilerParams(dimension_semantics=("arbitrary",), vmem_limit_bytes=VMEM_LIMIT),
        name="experts",
    )(block_expert, n_used, blk_end, xs, w['w_gate'], w['w_up'], w['w_down'])


def _combine_kernel(dest_ref, dest_next_ref, h_ref, w0_ref, w1_ref, g_ref, y_ref, o_ref, ybuf, sems, *, n_steps):
    tm = h_ref.shape[0]
    i = pl.program_id(0)
    slot = lax.rem(i, 2)

    def start_gather(d_ref, s):
        def body(r, carry):
            for k in range(TOP_K):
                d = d_ref[0, 0, k * tm + r]
                pltpu.make_async_copy(y_ref.at[d], ybuf.at[s, k, r], sems.at[s]).start(priority=k)
            return carry

        lax.fori_loop(0, tm, body, 0)

    @pl.when(i == 0)
    def _():
        start_gather(dest_ref, 0)

    @pl.when(i + 1 < n_steps)
    def _():
        start_gather(dest_next_ref, 1 - slot)

    for k in range(TOP_K):
        pltpu.make_async_copy(y_ref.at[pl.ds(0, tm)], ybuf.at[slot, k], sems.at[slot]).wait()

    y0 = ybuf[slot, 0].reshape(tm, D_MODEL)
    y1 = ybuf[slot, 1].reshape(tm, D_MODEL)
    w0 = w0_ref[...]
    w1 = w1_ref[...]
    cols = []
    for j in range(ROW_TILES):
        sl = slice(j * LANES, (j + 1) * LANES)
        cols.append(h_ref[:, sl] + w0 * y0[:, sl].astype(F32) + w1 * y1[:, sl].astype(F32))
    hs = jnp.concatenate(cols, axis=-1)
    o_ref[...] = _rms(hs, g_ref[...])


def _combine(dest, h, w0, w1, g_final, y):
    T = h.shape[0]
    tm = TOKEN_TILE
    n = T // tm
    dest3 = dest
    return pl.pallas_call(
        functools.partial(_combine_kernel, n_steps=n), grid=(n,),
        in_specs=[pl.BlockSpec((1, 1, TOP_K * tm), lambda i: (i, 0, 0), memory_space=pltpu.SMEM),
                  pl.BlockSpec((1, 1, TOP_K * tm), lambda i: (jnp.minimum(i + 1, n - 1), 0, 0),
                               memory_space=pltpu.SMEM),
                  pl.BlockSpec((tm, D_MODEL), lambda i: (i, 0)),
                  pl.BlockSpec((tm, LANES), lambda i: (i, 0)),
                  pl.BlockSpec((tm, LANES), lambda i: (i, 0)),
                  _const_spec((1, D_MODEL)),
                  pl.BlockSpec(memory_space=pl.ANY)],
        out_specs=pl.BlockSpec((tm, D_MODEL), lambda i: (i, 0)),
        out_shape=jax.ShapeDtypeStruct((T, D_MODEL), F32),
        scratch_shapes=[pltpu.VMEM((2, TOP_K, tm, ROW_TILES, LANES), BF16), pltpu.SemaphoreType.DMA((2,))],
        compiler_params=pltpu.CompilerParams(dimension_semantics=("arbitrary",), vmem_limit_bytes=VMEM_LIMIT),
        name="combine",
    )(dest3, dest3, h, w0, w1, g_final, y)


def _tables(S):
    pos = jnp.arange(S, dtype=F32)
    inv_freq = 1.0 / (ROPE_THETA ** (jnp.arange(0, QK_ROPE_DIM, 2, dtype=F32) / QK_ROPE_DIM))
    ang = pos[:, None] * inv_freq[None, :]
    cos, sin = jnp.cos(ang), jnp.sin(ang)
    cos2 = jnp.concatenate([cos, cos], axis=-1)
    sin2 = jnp.concatenate([sin, sin], axis=-1)
    zp = jnp.zeros((S, LANES - QK_ROPE_DIM), F32)
    t = {
        'cos_t': cos2.T, 'sin_t': sin2.T,
        'cos_p': jnp.concatenate([cos2, zp], axis=-1), 'sin_p': jnp.concatenate([sin2, zp], axis=-1),
    }

    def phase(rows, cols, n):
        m = (rows[:, None] * cols[None, :]) % n
        a = m.astype(F32) * jnp.float32(2.0 * math.pi / n)
        return jnp.cos(a), jnp.sin(a)

    i1 = jnp.arange(DFT_N1, dtype=jnp.int32)
    c1, s1 = phase(i1, i1, DFT_N1)
    t['dft_m1'] = jnp.concatenate(
        [jnp.concatenate([c1, s1], axis=1), jnp.concatenate([-s1, c1], axis=1)], axis=0).astype(BF16)
    k = jnp.arange(S, dtype=jnp.int32)
    n2 = jnp.arange(DFT_N2, dtype=jnp.int32)
    cg, sg = phase(k, n2, S)
    g = jnp.concatenate([cg, sg], axis=1) * jnp.float32(S ** -0.5)
    t['dft_g'] = g.reshape(DFT_N2, DFT_N1, 2 * DFT_N2).transpose(1, 0, 2).astype(BF16)
    return t


def _layer_weights(g_attn_norm, w_in, g_q_latent, w_uq, g_kv_latent, w_ukv, g_out_fourier, g_out_mla, w_out,
                   g_ffn_norm, w_router_group, b_router_group, w_router_expert, b_router_expert,
                   w_gate, w_up, w_down):
    half = QK_ROPE_DIM // 2
    s3 = FOURIER_WIDTH + Q_LORA_RANK + KV_LORA_RANK
    w_kr = w_in[:, s3:s3 + QK_ROPE_DIM]
    w_kr_rot = jnp.concatenate([-w_kr[:, half:], w_kr[:, :half]], axis=1)
    zc = jnp.zeros((D_MODEL, LANES - QK_ROPE_DIM), F32)
    w_in_ext = jnp.concatenate([w_in[:, :s3], w_kr, zc, w_kr_rot, zc], axis=1)

    c = jnp.arange(FOURIER_GROUP_DIM, dtype=jnp.int32)
    m = (c[:, None] * c[None, :]) % FOURIER_GROUP_DIM
    a = m.astype(F32) * jnp.float32(2.0 * math.pi / FOURIER_GROUP_DIM)
    cs = jnp.concatenate([jnp.cos(a), -jnp.sin(a)], axis=1) * jnp.float32(FOURIER_GROUP_DIM ** -0.5)

    qscale = jnp.float32(QK_HEAD_DIM ** -0.5 * math.log2(math.e))
    wq = (w_uq * qscale).reshape(Q_LORA_RANK, N_HEADS, QK_HEAD_DIM)
    wq_n = wq[:, :, :QK_NOPE_DIM]
    wq_r = wq[:, :, QK_NOPE_DIM:]
    wq_rot = jnp.concatenate([-wq_r[:, :, half:], wq_r[:, :, :half]], axis=2)
    to_t = lambda t: t.reshape(Q_LORA_RANK, -1).T
    wkv = w_ukv.reshape(KV_LORA_RANK, N_HEADS, QK_NOPE_DIM + V_HEAD_DIM)
    w_r = jnp.concatenate(
        [w_router_group, w_router_expert,
         jnp.zeros((D_MODEL, LANES - N_EXPERT_GROUPS - N_EXPERTS), F32)], axis=1)
    wr_hi = w_r.astype(BF16)
    b_r = jnp.concatenate([b_router_group, b_router_expert,
                           jnp.zeros((LANES - N_EXPERT_GROUPS - N_EXPERTS,), F32)])
    return {
        'g_attn': g_attn_norm[None, :], 'w_in': w_in_ext.astype(BF16), 'cs': cs.astype(BF16),
        'g_q': g_q_latent[None, :], 'g_kv': g_kv_latent[None, :],
        'wqn_t': to_t(wq_n).astype(BF16), 'wqr_t': to_t(wq_r).astype(BF16),
        'wuk': wkv[:, :, :QK_NOPE_DIM].reshape(KV_LORA_RANK, -1).astype(BF16),
        'wuv_t': wkv[:, :, QK_NOPE_DIM:].reshape(KV_LORA_RANK, -1).T.astype(BF16),
        'g_f': g_out_fourier[None, :], 'g_a': g_out_mla[None, :], 'w_out': w_out.astype(BF16),
        'g_ffn': g_ffn_norm[None, :],
        'wr': jnp.concatenate([wr_hi, (w_r - wr_hi.astype(F32)).astype(BF16)], axis=1),
        'b_r': b_r[None, :],
        'w_gate': w_gate, 'w_up': w_up, 'w_down': w_down,
    }


def _block_plan(counts, n_assign):
    counts = counts[0, N_EXPERT_GROUPS:N_EXPERT_GROUPS + N_EXPERTS].astype(jnp.int32)
    nblk = (counts + EXPERT_TILE - 1) // EXPERT_TILE
    blk_end = jnp.cumsum(nblk)
    blk_start = blk_end - nblk
    n_blocks = (n_assign + N_EXPERTS * (EXPERT_TILE - 1) + EXPERT_TILE - 1) // EXPERT_TILE
    n_used = blk_end[-1]
    j = jnp.minimum(jnp.arange(n_blocks, dtype=jnp.int32), n_used - 1)
    block_expert = jnp.minimum(jnp.sum((blk_end[None, :] <= j[:, None]).astype(jnp.int32), axis=1), N_EXPERTS - 1)
    return blk_start, blk_end, block_expert, n_used[None], n_blocks * EXPERT_TILE


def _dest_rows(route, blk_start):
    B, _, S = route.shape
    e = route[:, :TOP_K].astype(jnp.int32)
    rank = route[:, TOP_K:2 * TOP_K].astype(jnp.int32)
    first = jnp.zeros_like(e)
    for j in range(N_EXPERTS):
        first = jnp.where(e == j, blk_start[j], first)
    dest = (first * EXPERT_TILE + rank).reshape(B, TOP_K, S // TOKEN_TILE, TOKEN_TILE)
    return dest.transpose(0, 2, 1, 3).reshape(B * (S // TOKEN_TILE), 1, TOP_K * TOKEN_TILE)


def _mixers(x, wt, counts0):
    B, S, _ = x.shape
    T = B * S
    xr, xi, qt, kx, vt = _proj(x, wt)
    f_out = _seq_dft(xr, xi, wt)
    a_out = _attention(qt, kx, vt)
    h, hn3, route, w0, w1, counts = _merge(x, f_out, a_out, wt, counts0)
    flat = lambda t: t.reshape((T,) + t.shape[2:])
    return flat(h), flat(hn3), route, flat(w0), flat(w1), counts


def kernel(x_prompt, x_sample, g_attn_norm, w_in, g_q_latent, w_uq, g_kv_latent, w_ukv, g_out_fourier, g_out_mla, w_out, g_ffn_norm, w_router_group, b_router_group, w_router_expert, b_router_expert, w_gate, w_up, w_down, g_final):
    assert g_attn_norm.shape[0] == 1, "single-layer configuration"
    w = _layer_weights(g_attn_norm[0], w_in[0], g_q_latent[0], w_uq[0], g_kv_latent[0], w_ukv[0],
                       g_out_fourier[0], g_out_mla[0], w_out[0], g_ffn_norm[0], w_router_group[0],
                       b_router_group[0], w_router_expert[0], b_router_expert[0], w_gate[0], w_up[0], w_down[0])
    assert x_prompt.shape[1] == x_sample.shape[1]
    w.update(_tables(x_prompt.shape[1]))

    counts = jnp.zeros((1, LANES), F32)
    batches = []
    for x in (x_prompt, x_sample):
        h, hn3, route, w0, w1, counts = _mixers(x, w, counts)
        batches.append((x.shape, h, hn3, route, w0, w1))

    n_assign = sum(b[1].shape[0] for b in batches) * TOP_K
    blk_start, blk_end, block_expert, n_used, n_rows = _block_plan(counts, n_assign)
    dests = [_dest_rows(b[3], blk_start) for b in batches]
    xs = _dispatch(blk_end, n_used, jnp.concatenate(dests, axis=0), batches[0][2], batches[1][2], n_rows)
    y = _experts(block_expert, n_used, blk_end, xs, w)
    outs = []
    for (shape, h, _, _, w0, w1), dest in zip(batches, dests):
        outs.append(_combine(dest, h, w0, w1, g_final[None, :], y).reshape(shape))
    return tuple(outs)
```

```python
import functools
import math

import jax
import jax.numpy as jnp
from jax import lax
from jax.experimental import pallas as pl
from jax.experimental.pallas import tpu as pltpu

D_MODEL = 2048
FOURIER_WIDTH = 1024
N_FOURIER_GROUPS = 4
FOURIER_GROUP_DIM = 256
MLA_WIDTH = 1024
V_HEAD_DIM = 128
N_HEADS = 8
QK_NOPE_DIM = 128
QK_ROPE_DIM = 64
QK_HEAD_DIM = 192
Q_LORA_RANK = 512
KV_LORA_RANK = 256
ROPE_THETA = 10000.0
N_EXPERT_GROUPS = 4
EXPERTS_PER_GROUP = 8
N_EXPERTS = 32
TOP_K = 2
EXPERT_DIM = 512
EPS = 1e-6

LANES = 128
QK_PAD = 256
ROW_TILES = D_MODEL // LANES
ROUTE_ROWS = 8

DFT_N1 = 64
DFT_N2 = 128

TOKEN_TILE = 512
ATTN_TQ = 1024
ATTN_TK = 512
ATTN_SUB = 8
ATTN_QBLOCKS = 4
DFT_ROWS = 16
EXPERT_TILE = 256
VMEM_LIMIT = 56 * 1024 * 1024

F32 = jnp.float32
BF16 = jnp.bfloat16


def _const_spec(shape):
    nd = len(shape)
    return pl.BlockSpec(shape, lambda *_: (0,) * nd, pipeline_mode=pl.Buffered(1))


def _rms(x, g):
    return x * lax.rsqrt(jnp.mean(x * x, axis=-1, keepdims=True) + EPS) * g


def _proj_kernel(x_ref, g_attn_ref, w_in_ref, cs_ref, g_q_ref, g_kv_ref, wqn_ref, wqr_ref,
                 wuk_ref, wuvt_ref, cos_t_ref, sin_t_ref, cos_p_ref, sin_p_ref,
                 xr_ref, xi_ref, qt_ref, kx_ref, vt_ref):
    tm = x_ref.shape[1]
    x = x_ref[0]
    u = _rms(x, g_attn_ref[...]).astype(BF16)
    z = jnp.dot(u, w_in_ref[...], preferred_element_type=F32)

    zf = z[:, :FOURIER_WIDTH].astype(BF16)
    xr, xi = [], []
    for g in range(N_FOURIER_GROUPS):
        zg = zf[:, g * FOURIER_GROUP_DIM:(g + 1) * FOURIER_GROUP_DIM]
        xg = jnp.dot(zg, cs_ref[...], preferred_element_type=F32)
        xr.append(xg[:, :FOURIER_GROUP_DIM])
        xi.append(xg[:, FOURIER_GROUP_DIM:])
    xr_ref[0] = jnp.concatenate(xr, axis=-1).astype(BF16)
    xi_ref[0] = jnp.concatenate(xi, axis=-1).astype(BF16)

    o = FOURIER_WIDTH
    q_lat = z[:, o:o + Q_LORA_RANK]
    o += Q_LORA_RANK
    kv_lat = z[:, o:o + KV_LORA_RANK]
    o += KV_LORA_RANK
    kr = z[:, o:o + LANES]
    kr_rot = z[:, o + LANES:o + 2 * LANES]

    qn = _rms(q_lat, g_q_ref[...]).astype(BF16)
    kvn = _rms(kv_lat, g_kv_ref[...]).astype(BF16)

    nt = (((1,), (1,)), ((), ()))
    qn_t = lax.dot_general(wqn_ref[...], qn, nt, preferred_element_type=F32)
    qr_t = lax.dot_general(wqr_ref[...], qn, nt, preferred_element_type=F32)
    cos_t = cos_t_ref[...]
    sin_t = sin_t_ref[...]
    zpad = jnp.zeros((QK_PAD - QK_HEAD_DIM, tm), BF16)
    half = QK_ROPE_DIM // 2
    for h in range(N_HEADS):
        r0 = h * QK_PAD
        qt_ref[0, 0, r0:r0 + QK_NOPE_DIM, :] = qn_t[h * QK_NOPE_DIM:(h + 1) * QK_NOPE_DIM].astype(BF16)
        qr = qr_t[h * QK_ROPE_DIM:(h + 1) * QK_ROPE_DIM]
        rot = jnp.concatenate([-qr[half:], qr[:half]], axis=0)
        roped = qr * cos_t + rot * sin_t
        qt_ref[0, 0, r0 + QK_NOPE_DIM:r0 + QK_HEAD_DIM, :] = roped.astype(BF16)
        qt_ref[0, 0, r0 + QK_HEAD_DIM:r0 + QK_PAD, :] = zpad

    k_nope = jnp.dot(kvn, wuk_ref[...], preferred_element_type=F32)
    k_rope = (kr * cos_p_ref[...] + kr_rot * sin_p_ref[...]).astype(BF16)
    for h in range(N_HEADS):
        kx_ref[0, h, :, :QK_NOPE_DIM] = k_nope[:, h * QK_NOPE_DIM:(h + 1) * QK_NOPE_DIM].astype(BF16)
        kx_ref[0, h, :, QK_NOPE_DIM:] = k_rope

    v_t = lax.dot_general(wuvt_ref[...], kvn, nt, preferred_element_type=F32)
    vt_ref[0, 0] = v_t.astype(BF16).reshape(N_HEADS, V_HEAD_DIM, tm)


def _proj(x, w):
    B, S, _ = x.shape
    tm = ATTN_TK
    q_split = ATTN_TQ // tm
    grid = (B, S // tm)
    tok = lambda b, i: (b, i, 0)
    out_shape = [
        jax.ShapeDtypeStruct((B, S, FOURIER_WIDTH), BF16),
        jax.ShapeDtypeStruct((B, S, FOURIER_WIDTH), BF16),
        jax.ShapeDtypeStruct((B, S // ATTN_TQ, N_HEADS * QK_PAD, ATTN_TQ), BF16),
        jax.ShapeDtypeStruct((B, N_HEADS, S, QK_PAD), BF16),
        jax.ShapeDtypeStruct((B, S // tm, N_HEADS, V_HEAD_DIM, tm), BF16),
    ]
    in_specs = [
        pl.BlockSpec((1, tm, D_MODEL), tok),
        _const_spec((1, D_MODEL)),
        _const_spec(w['w_in'].shape),
        _const_spec(w['cs'].shape),
        _const_spec((1, Q_LORA_RANK)),
        _const_spec((1, KV_LORA_RANK)),
        _const_spec(w['wqn_t'].shape),
        _const_spec(w['wqr_t'].shape),
        _const_spec(w['wuk'].shape),
        _const_spec(w['wuv_t'].shape),
        pl.BlockSpec((QK_ROPE_DIM, tm), lambda b, i: (0, i)),
        pl.BlockSpec((QK_ROPE_DIM, tm), lambda b, i: (0, i)),
        pl.BlockSpec((tm, LANES), lambda b, i: (i, 0)),
        pl.BlockSpec((tm, LANES), lambda b, i: (i, 0)),
    ]
    out_specs = [
        pl.BlockSpec((1, tm, FOURIER_WIDTH), tok),
        pl.BlockSpec((1, tm, FOURIER_WIDTH), tok),
        pl.BlockSpec((1, 1, N_HEADS * QK_PAD, tm), lambda b, i: (b, i // q_split, 0, i % q_split)),
        pl.BlockSpec((1, N_HEADS, tm, QK_PAD), lambda b, i: (b, 0, i, 0)),
        pl.BlockSpec((1, 1, N_HEADS, V_HEAD_DIM, tm), lambda b, i: (b, i, 0, 0, 0)),
    ]
    return pl.pallas_call(
        _proj_kernel, grid=grid, in_specs=in_specs, out_specs=out_specs, out_shape=out_shape,
        compiler_params=pltpu.CompilerParams(
            dimension_semantics=("arbitrary", "arbitrary"), vmem_limit_bytes=VMEM_LIMIT),
        name="proj",
    )(x, w['g_attn'], w['w_in'], w['cs'], w['g_q'], w['g_kv'], w['wqn_t'], w['wqr_t'],
      w['wuk'], w['wuv_t'], w['cos_t'], w['sin_t'], w['cos_p'], w['sin_p'])


def _dft1_kernel(m1_ref, xr_ref, xi_ref, zr_ref, zi_ref):
    nb, c = xr_ref.shape[2], xr_ref.shape[3]
    xr = xr_ref[0].reshape(DFT_N1, nb * c)
    xi = xi_ref[0].reshape(DFT_N1, nb * c)
    xc = jnp.concatenate([xr, xi], axis=0)
    z = jnp.dot(m1_ref[...], xc, preferred_element_type=F32).astype(BF16)
    zr_ref[0] = z[:DFT_N1].reshape(DFT_N1, nb, c)
    zi_ref[0] = z[DFT_N1:].reshape(DFT_N1, nb, c)


def _dft2_kernel(g_ref, zr_ref, zi_ref, o_ref):
    kb, c = zr_ref.shape[1], zr_ref.shape[3]
    ys = []
    for j in range(kb):
        zc = jnp.concatenate([zr_ref[0, j], zi_ref[0, j]], axis=0)
        ys.append(jnp.dot(g_ref[j], zc, preferred_element_type=F32).astype(BF16))
    o_ref[0] = jnp.concatenate(ys, axis=-1).reshape(DFT_N2, kb, c)


def _seq_dft(xr, xi, w):
    B, S, C = xr.shape
    assert S == DFT_N1 * DFT_N2
    nb = DFT_ROWS
    xr4 = xr.reshape(B, DFT_N1, DFT_N2, C)
    xi4 = xi.reshape(B, DFT_N1, DFT_N2, C)
    rows = pl.BlockSpec((1, DFT_N1, nb, C), lambda b, j: (b, 0, j, 0))
    zr, zi = pl.pallas_call(
        _dft1_kernel, grid=(B, DFT_N2 // nb),
        in_specs=[_const_spec((2 * DFT_N1, 2 * DFT_N1)), rows, rows],
        out_specs=[rows, rows],
        out_shape=[jax.ShapeDtypeStruct((B, DFT_N1, DFT_N2, C), BF16)] * 2,
        compiler_params=pltpu.CompilerParams(
            dimension_semantics=("arbitrary", "arbitrary"), vmem_limit_bytes=VMEM_LIMIT),
        name="dft1",
    )(w['dft_m1'], xr4, xi4)
    kb = DFT_ROWS
    blk = pl.BlockSpec((1, kb, DFT_N2, C), lambda b, k: (b, k, 0, 0))
    y = pl.pallas_call(
        _dft2_kernel, grid=(B, DFT_N1 // kb),
        in_specs=[pl.BlockSpec((kb, DFT_N2, 2 * DFT_N2), lambda b, k: (k, 0, 0)), blk, blk],
        out_specs=pl.BlockSpec((1, DFT_N2, kb, C), lambda b, k: (b, 0, k, 0)),
        out_shape=jax.ShapeDtypeStruct((B, DFT_N2, DFT_N1, C), BF16),
        compiler_params=pltpu.CompilerParams(
            dimension_semantics=("arbitrary", "arbitrary"), vmem_limit_bytes=VMEM_LIMIT),
        name="dft2",
    )(w['dft_g'], zr, zi)
    return y.reshape(B, S, C)


def _attn_kernel(q_ref, k_ref, v_ref, o_ref, s_scr, acc_ref):
    tk = v_ref.shape[-1]
    nch = v_ref.shape[1]
    nq, tq = q_ref.shape[1], q_ref.shape[-1]
    groups = nch // ATTN_SUB

    def scores(f):
        j = jnp.minimum(f // nch, nq - 1)
        c = lax.rem(f, nch)
        k_c = k_ref[0, 0, pl.ds(pl.multiple_of(c * tk, tk), tk), :]
        return jnp.dot(k_c, q_ref[0, j], preferred_element_type=F32)

    def step(f, slot, m, l, s_max):
        s_next = scores(f + 1)
        s_scr[1 - slot] = s_next
        next_max = jnp.max(s_next, axis=0, keepdims=True)
        s = s_scr[slot]
        m_new = jnp.maximum(m, s_max)
        alpha = jnp.exp2(m - m_new)
        p = jnp.exp2(s - m_new)
        l = alpha * l + jnp.sum(p, axis=0, keepdims=True)
        pv = jnp.dot(v_ref[0, lax.rem(f, nch), 0], p.astype(BF16), preferred_element_type=F32)
        acc_ref[...] = alpha * acc_ref[...] + pv
        return m_new, l, next_max

    acc_ref[...] = jnp.zeros(acc_ref.shape, F32)
    s_first = scores(0)
    s_scr[0] = s_first
    m0 = jnp.full((1, tq), -1e30, F32)
    l0 = jnp.zeros((1, tq), F32)

    def body(g, carry):
        m, l, s_max = carry
        for si in range(ATTN_SUB):
            m, l, s_max = step(g * ATTN_SUB + si, si % 2, m, l, s_max)
        block_done = lax.rem(g, groups) == groups - 1

        @pl.when(block_done)
        def _():
            j = g // groups
            o_t = acc_ref[...] * (1.0 / l)
            o_ref[0, pl.ds(pl.multiple_of(j * tq, tq), tq), :] = o_t.T.astype(BF16)
            acc_ref[...] = jnp.zeros(acc_ref.shape, F32)

        return jnp.where(block_done, m0, m), jnp.where(block_done, l0, l), s_max

    lax.fori_loop(0, nq * groups, body, (m0, l0, jnp.max(s_first, axis=0, keepdims=True)))


def _attention(qt, kx, vt):
    B, nqb, _, tq = qt.shape
    S = nqb * tq
    nch, tk = vt.shape[1], vt.shape[-1]
    nq = ATTN_QBLOCKS
    assert ATTN_SUB % 2 == 0 and nch % ATTN_SUB == 0 and nqb % nq == 0
    return pl.pallas_call(
        _attn_kernel, grid=(B, N_HEADS, nqb // nq),
        in_specs=[pl.BlockSpec((1, nq, QK_PAD, tq), lambda b, h, i: (b, i, h, 0)),
                  pl.BlockSpec((1, 1, S, QK_PAD), lambda b, h, i: (b, h, 0, 0)),
                  pl.BlockSpec((1, nch, 1, V_HEAD_DIM, tk), lambda b, h, i: (b, 0, h, 0, 0))],
        out_specs=pl.BlockSpec((1, nq * tq, V_HEAD_DIM), lambda b, h, i: (b, i, h)),
        out_shape=jax.ShapeDtypeStruct((B, S, MLA_WIDTH), BF16),
        scratch_shapes=[pltpu.VMEM((2, tk, tq), F32), pltpu.VMEM((V_HEAD_DIM, tq), F32)],
        compiler_params=pltpu.CompilerParams(
            dimension_semantics=("arbitrary", "arbitrary", "arbitrary"), vmem_limit_bytes=VMEM_LIMIT),
        name="attn",
    )(qt, kx, vt)


def _merge_kernel(x_ref, f_ref, a_ref, g_f_ref, g_a_ref, w_out_ref, g_ffn_ref, wr_ref, b_r_ref, tri_ref, cnt0_ref,
                  h_ref, hn_ref, route_ref, w0_ref, w1_ref, cnt_ref, carry_ref):
    tm = x_ref.shape[0]

    @pl.when(pl.program_id(0) == 0)
    def _():
        carry_ref[...] = cnt0_ref[...]

    fn = _rms(f_ref[...].astype(F32), g_f_ref[...]).astype(BF16)
    an = _rms(a_ref[...].astype(F32), g_a_ref[...]).astype(BF16)
    merged = jnp.concatenate([fn, an], axis=-1)
    h = x_ref[...] + jnp.dot(merged, w_out_ref[...], preferred_element_type=F32)
    h_ref[...] = h
    hn = _rms(h, g_ffn_ref[...])
    hn_ref[...] = hn.reshape(tm, ROW_TILES, LANES).astype(BF16)
    carry = carry_ref[...]

    p_r = jnp.dot(hn.astype(BF16), wr_ref[...], preferred_element_type=F32)
    logits = p_r[:, :LANES] + p_r[:, LANES:] + b_r_ref[...]

    lane = lax.broadcasted_iota(jnp.int32, logits.shape, 1)
    neg = jnp.float32(-1e30)
    big = jnp.int32(LANES)
    is_g = lane < N_EXPERT_GROUPS
    gl = jnp.where(is_g, logits, neg)
    gmax = jnp.max(gl, axis=-1, keepdims=True)
    g_sel = jnp.min(jnp.where(gl == gmax, lane, big), axis=-1, keepdims=True)
    g_p = 1.0 / jnp.sum(jnp.where(is_g, jnp.exp(gl - gmax), 0.0), axis=-1, keepdims=True)
    e_lo = N_EXPERT_GROUPS + EXPERTS_PER_GROUP * g_sel
    in_grp = jnp.where(lane >= e_lo, jnp.where(lane < e_lo + EXPERTS_PER_GROUP, 1, 0), 0) == 1
    el = jnp.where(in_grp, logits, neg)
    m1 = jnp.max(el, axis=-1, keepdims=True)
    i1 = jnp.min(jnp.where(el == m1, lane, big), axis=-1, keepdims=True)
    el2 = jnp.where(lane == i1, neg, el)
    m2 = jnp.max(el2, axis=-1, keepdims=True)
    i2 = jnp.min(jnp.where(el2 == m2, lane, big), axis=-1, keepdims=True)
    t = jnp.exp(m2 - m1)
    inv = 1.0 / (1.0 + t)
    w0 = g_p * inv
    w1 = g_p * t * inv
    e0 = (i1 - N_EXPERT_GROUPS).astype(F32)
    e1 = (i2 - N_EXPERT_GROUPS).astype(F32)

    oh0 = jnp.where(lane == i1, 1.0, 0.0)
    oh1 = jnp.where(lane == i2, 1.0, 0.0)
    oh = oh0 + oh1
    before = jnp.dot(tri_ref[...], oh.astype(BF16), preferred_element_type=F32) + carry
    r0 = jnp.sum(before * oh0, axis=-1, keepdims=True)
    r1 = jnp.sum(before * oh1, axis=-1, keepdims=True)

    route = jnp.where(lane == 0, e0, jnp.where(lane == 1, e1, jnp.where(lane == 2, r0, jnp.where(lane == 3, r1, 0.0))))
    route_ref[...] = route.T[:ROUTE_ROWS]
    w0_ref[...] = jnp.broadcast_to(w0, logits.shape)
    w1_ref[...] = jnp.broadcast_to(w1, logits.shape)
    total = carry + jnp.sum(oh, axis=0, keepdims=True)
    carry_ref[...] = total
    cnt_ref[...] = total


def _merge(x, f_out, a_out, w, counts0):
    T = x.shape[0]
    tm = TOKEN_TILE
    n = T // tm
    cur = lambda t: (t, 0)
    out_shape = [
        jax.ShapeDtypeStruct((T, D_MODEL), F32),
        jax.ShapeDtypeStruct((T, ROW_TILES, LANES), BF16),
        jax.ShapeDtypeStruct((ROUTE_ROWS, T), F32),
        jax.ShapeDtypeStruct((T, LANES), F32),
        jax.ShapeDtypeStruct((T, LANES), F32),
        jax.ShapeDtypeStruct((1, LANES), F32),
    ]
    in_specs = [
        pl.BlockSpec((tm, D_MODEL), cur),
        pl.BlockSpec((tm, FOURIER_WIDTH), cur),
        pl.BlockSpec((tm, MLA_WIDTH), cur),
        _const_spec((1, FOURIER_WIDTH)),
        _const_spec((1, MLA_WIDTH)),
        _const_spec((D_MODEL, D_MODEL)),
        _const_spec((1, D_MODEL)),
        _const_spec((D_MODEL, 2 * LANES)),
        _const_spec((1, LANES)),
        _const_spec((tm, tm)),
        _const_spec((1, LANES)),
    ]
    out_specs = [
        pl.BlockSpec((tm, D_MODEL), cur),
        pl.BlockSpec((tm, ROW_TILES, LANES), lambda t: (t, 0, 0)),
        pl.BlockSpec((ROUTE_ROWS, tm), lambda t: (0, t)),
        pl.BlockSpec((tm, LANES), cur),
        pl.BlockSpec((tm, LANES), cur),
        pl.BlockSpec((1, LANES), lambda t: (0, 0)),
    ]
    tri = (lax.broadcasted_iota(jnp.int32, (tm, tm), 1) < lax.broadcasted_iota(jnp.int32, (tm, tm), 0)).astype(BF16)
    return pl.pallas_call(
        _merge_kernel, grid=(n,), in_specs=in_specs, out_specs=out_specs, out_shape=out_shape,
        scratch_shapes=[pltpu.VMEM((1, LANES), F32)],
        compiler_params=pltpu.CompilerParams(dimension_semantics=("arbitrary",), vmem_limit_bytes=VMEM_LIMIT),
        name="merge",
    )(x, f_out, a_out, w['g_f'], w['g_a'], w['w_out'], w['g_ffn'], w['wr'], w['b_r'], tri, counts0)


def _dispatch_kernel(blk_end_ref, nused_ref, dest_ref, hn_a_ref, hn_b_ref, xs_ref, zero_buf, sem, zsem, *, n_a):
    i = pl.program_id(0)
    tm = hn_a_ref.shape[0]
    nb = xs_ref.shape[0] // EXPERT_TILE

    @pl.when(i == 0)
    def _():
        zero_buf[...] = jnp.zeros(zero_buf.shape, BF16)

        def zero_block(j):
            return pltpu.make_async_copy(zero_buf, xs_ref.at[pl.ds(j * EXPERT_TILE, EXPERT_TILE)], zsem)

        def each_group(fn):
            for e in range(N_EXPERTS):
                end = blk_end_ref[e]
                start = blk_end_ref[e - 1] if e else 0

                @pl.when(end > start)
                def _():
                    fn(end - 1)

        def each_tail(fn):
            def body(j, carry):
                fn(j)
                return carry

            lax.fori_loop(nused_ref[0], nb, body, 0)

        each_group(lambda j: zero_block(j).start())
        each_tail(lambda j: zero_block(j).start())
        each_group(lambda j: zero_block(j).wait())
        each_tail(lambda j: zero_block(j).wait())

    def scatter(src_ref):
        def body(r, carry):
            for k in range(TOP_K):
                d = dest_ref[0, 0, k * tm + r]
                pltpu.make_async_copy(src_ref.at[r], xs_ref.at[d], sem).start(priority=k)
            return carry

        lax.fori_loop(0, tm, body, 0)
        for _ in range(TOP_K):
            pltpu.make_async_copy(src_ref, xs_ref.at[pl.ds(0, tm)], sem).wait()

    @pl.when(i < n_a)
    def _():
        scatter(hn_a_ref)

    @pl.when(i >= n_a)
    def _():
        scatter(hn_b_ref)


def _dispatch(blk_end, n_used, dest, hn_a, hn_b, n_rows):
    tm = TOKEN_TILE
    n_a, n_b = hn_a.shape[0] // tm, hn_b.shape[0] // tm
    tile = (tm, ROW_TILES, LANES)
    grid_spec = pltpu.PrefetchScalarGridSpec(
        num_scalar_prefetch=2, grid=(n_a + n_b,),
        in_specs=[pl.BlockSpec((1, 1, TOP_K * tm), lambda i, be, nu: (i, 0, 0), memory_space=pltpu.SMEM),
                  pl.BlockSpec(tile, lambda i, be, nu: (jnp.minimum(i, n_a - 1), 0, 0)),
                  pl.BlockSpec(tile, lambda i, be, nu: (jnp.maximum(i - n_a, 0), 0, 0))],
        out_specs=pl.BlockSpec(memory_space=pl.ANY),
        scratch_shapes=[pltpu.VMEM((EXPERT_TILE, ROW_TILES, LANES), BF16),
                        pltpu.SemaphoreType.DMA(()), pltpu.SemaphoreType.DMA(())],
    )
    return pl.pallas_call(
        functools.partial(_dispatch_kernel, n_a=n_a), grid_spec=grid_spec,
        out_shape=jax.ShapeDtypeStruct((n_rows, ROW_TILES, LANES), BF16),
        compiler_params=pltpu.CompilerParams(dimension_semantics=("arbitrary",), vmem_limit_bytes=VMEM_LIMIT),
        name="dispatch",
    )(blk_end, n_used, dest, hn_a, hn_b)


def _expert_kernel(be_ref, nused_ref, blk_end_ref, xs_ref, wg_ref, wu_ref, wd_ref, y_ref,
                   wg_f, wu_f, wd_f, wgu_s, wd_s, group_ref, sems):
    i = pl.program_id(0)
    e = be_ref[i]

    def weight_copies(expert, slot):
        return [pltpu.make_async_copy(src.at[expert], dst.at[slot], sems.at[slot])
                for src, dst in ((wg_ref, wg_f), (wu_ref, wu_f), (wd_ref, wd_f))]

    @pl.when(i == 0)
    def _():
        group_ref[0] = 0
        for cp in weight_copies(e, 0):
            cp.start()

    @pl.when((i == 0) | (e != be_ref[jnp.maximum(i - 1, 0)]))
    def _():
        slot = lax.rem(group_ref[0], 2)
        for cp in weight_copies(e, slot):
            cp.wait()
        wgu_s[:, :EXPERT_DIM] = wg_f[slot].astype(BF16)
        wgu_s[:, EXPERT_DIM:] = wu_f[slot].astype(BF16)
        wd_s[...] = wd_f[slot].astype(BF16)
        nxt = blk_end_ref[e]

        @pl.when(nxt < nused_ref[0])
        def _():
            for cp in weight_copies(be_ref[nxt], 1 - slot):
                cp.start()

        group_ref[0] = group_ref[0] + 1

    @pl.when(i < nused_ref[0])
    def _():
        tm = xs_ref.shape[0]
        x = xs_ref[...].reshape(tm, D_MODEL)
        gu = jnp.dot(x, wgu_s[...], preferred_element_type=F32)
        g = gu[:, :EXPERT_DIM]
        u = gu[:, EXPERT_DIM:]
        hmid = (g * jax.nn.sigmoid(g) * u).astype(BF16)
        y = jnp.dot(hmid, wd_s[...], preferred_element_type=F32)
        y_ref[...] = y.reshape(tm, ROW_TILES, LANES).astype(BF16)

    @pl.when(i >= nused_ref[0])
    def _():
        y_ref[...] = jnp.zeros(y_ref.shape, BF16)


def _experts(block_expert, n_used, blk_end, xs, w):
    n_rows = xs.shape[0]
    tm = EXPERT_TILE
    nb = n_rows // tm
    row = lambda i, be, nu, bend: (jnp.minimum(i, nu[0] - 1), 0, 0)
    out_row = lambda i, be, nu, bend: (i, 0, 0)
    hbm = pl.BlockSpec(memory_space=pl.ANY)
    grid_spec = pltpu.PrefetchScalarGridSpec(
        num_scalar_prefetch=3, grid=(nb,),
        in_specs=[pl.BlockSpec((tm, ROW_TILES, LANES), row), hbm, hbm, hbm],
        out_specs=pl.BlockSpec((tm, ROW_TILES, LANES), out_row),
        scratch_shapes=[pltpu.VMEM((2, D_MODEL, EXPERT_DIM), F32), pltpu.VMEM((2, D_MODEL, EXPERT_DIM), F32),
                        pltpu.VMEM((2, EXPERT_DIM, D_MODEL), F32),
                        pltpu.VMEM((D_MODEL, 2 * EXPERT_DIM), BF16), pltpu.VMEM((EXPERT_DIM, D_MODEL), BF16),
                        pltpu.SMEM((1,), jnp.int32), pltpu.SemaphoreType.DMA((2,))],
    )
    return pl.pallas_call(
        _expert_kernel, grid_spec=grid_spec,
        out_shape=jax.ShapeDtypeStruct((n_rows, ROW_TILES, LANES), BF16),
        compiler_params=pltpu.CompilerParams(dimension_semantics=("arbitrary",), vmem_limit_bytes=VMEM_LIMIT),
        name="experts",
    )(block_expert, n_used, blk_end, xs, w['w_gate'], w['w_up'], w['w_down'])


def _combine_kernel(dest_ref, dest_next_ref, h_ref, w0_ref, w1_ref, g_ref, y_ref, o_ref, ybuf, sems, *, n_steps):
    tm = h_ref.shape[0]
    i = pl.program_id(0)
    slot = lax.rem(i, 2)

    def start_gather(d_ref, s):
        def body(r, carry):
            for k in range(TOP_K):
                d = d_ref[0, 0, k * tm + r]
                pltpu.make_async_copy(y_ref.at[d], ybuf.at[s, k, r], sems.at[s]).start(priority=k)
            return carry

        lax.fori_loop(0, tm, body, 0)

    @pl.when(i == 0)
    def _():
        start_gather(dest_ref, 0)

    @pl.when(i + 1 < n_steps)
    def _():
        start_gather(dest_next_ref, 1 - slot)

    for k in range(TOP_K):
        pltpu.make_async_copy(y_ref.at[pl.ds(0, tm)], ybuf.at[slot, k], sems.at[slot]).wait()

    y0 = ybuf[slot, 0].reshape(tm, D_MODEL)
    y1 = ybuf[slot, 1].reshape(tm, D_MODEL)
    w0 = w0_ref[...]
    w1 = w1_ref[...]
    cols = []
    for j in range(ROW_TILES):
        sl = slice(j * LANES, (j + 1) * LANES)
        cols.append(h_ref[:, sl] + w0 * y0[:, sl].astype(F32) + w1 * y1[:, sl].astype(F32))
    hs = jnp.concatenate(cols, axis=-1)
    o_ref[...] = _rms(hs, g_ref[...])


def _combine(dest, h, w0, w1, g_final, y):
    T = h.shape[0]
    tm = TOKEN_TILE
    n = T // tm
    dest3 = dest
    return pl.pallas_call(
        functools.partial(_combine_kernel, n_steps=n), grid=(n,),
        in_specs=[pl.BlockSpec((1, 1, TOP_K * tm), lambda i: (i, 0, 0), memory_space=pltpu.SMEM),
                  pl.BlockSpec((1, 1, TOP_K * tm), lambda i: (jnp.minimum(i + 1, n - 1), 0, 0),
                               memory_space=pltpu.SMEM),
                  pl.BlockSpec((tm, D_MODEL), lambda i: (i, 0)),
                  pl.BlockSpec((tm, LANES), lambda i: (i, 0)),
                  pl.BlockSpec((tm, LANES), lambda i: (i, 0)),
                  _const_spec((1, D_MODEL)),
                  pl.BlockSpec(memory_space=pl.ANY)],
        out_specs=pl.BlockSpec((tm, D_MODEL), lambda i: (i, 0)),
        out_shape=jax.ShapeDtypeStruct((T, D_MODEL), F32),
        scratch_shapes=[pltpu.VMEM((2, TOP_K, tm, ROW_TILES, LANES), BF16), pltpu.SemaphoreType.DMA((2,))],
        compiler_params=pltpu.CompilerParams(dimension_semantics=("arbitrary",), vmem_limit_bytes=VMEM_LIMIT),
        name="combine",
    )(dest3, dest3, h, w0, w1, g_final, y)


def _tables(S):
    pos = jnp.arange(S, dtype=F32)
    inv_freq = 1.0 / (ROPE_THETA ** (jnp.arange(0, QK_ROPE_DIM, 2, dtype=F32) / QK_ROPE_DIM))
    ang = pos[:, None] * inv_freq[None, :]
    cos, sin = jnp.cos(ang), jnp.sin(ang)
    cos2 = jnp.concatenate([cos, cos], axis=-1)
    sin2 = jnp.concatenate([sin, sin], axis=-1)
    zp = jnp.zeros((S, LANES - QK_ROPE_DIM), F32)
    t = {
        'cos_t': cos2.T, 'sin_t': sin2.T,
        'cos_p': jnp.concatenate([cos2, zp], axis=-1), 'sin_p': jnp.concatenate([sin2, zp], axis=-1),
    }

    def phase(rows, cols, n):
        m = (rows[:, None] * cols[None, :]) % n
        a = m.astype(F32) * jnp.float32(2.0 * math.pi / n)
        return jnp.cos(a), jnp.sin(a)

    i1 = jnp.arange(DFT_N1, dtype=jnp.int32)
    c1, s1 = phase(i1, i1, DFT_N1)
    t['dft_m1'] = jnp.concatenate(
        [jnp.concatenate([c1, s1], axis=1), jnp.concatenate([-s1, c1], axis=1)], axis=0).astype(BF16)
    k = jnp.arange(S, dtype=jnp.int32)
    n2 = jnp.arange(DFT_N2, dtype=jnp.int32)
    cg, sg = phase(k, n2, S)
    g = jnp.concatenate([cg, sg], axis=1) * jnp.float32(S ** -0.5)
    t['dft_g'] = g.reshape(DFT_N2, DFT_N1, 2 * DFT_N2).transpose(1, 0, 2).astype(BF16)
    return t


def _layer_weights(g_attn_norm, w_in, g_q_latent, w_uq, g_kv_latent, w_ukv, g_out_fourier, g_out_mla, w_out,
                   g_ffn_norm, w_router_group, b_router_group, w_router_expert, b_router_expert,
                   w_gate, w_up, w_down):
    half = QK_ROPE_DIM // 2
    s3 = FOURIER_WIDTH + Q_LORA_RANK + KV_LORA_RANK
    w_kr = w_in[:, s3:s3 + QK_ROPE_DIM]
    w_kr_rot = jnp.concatenate([-w_kr[:, half:], w_kr[:, :half]], axis=1)
    zc = jnp.zeros((D_MODEL, LANES - QK_ROPE_DIM), F32)
    w_in_ext = jnp.concatenate([w_in[:, :s3], w_kr, zc, w_kr_rot, zc], axis=1)

    c = jnp.arange(FOURIER_GROUP_DIM, dtype=jnp.int32)
    m = (c[:, None] * c[None, :]) % FOURIER_GROUP_DIM
    a = m.astype(F32) * jnp.float32(2.0 * math.pi / FOURIER_GROUP_DIM)
    cs = jnp.concatenate([jnp.cos(a), -jnp.sin(a)], axis=1) * jnp.float32(FOURIER_GROUP_DIM ** -0.5)

    qscale = jnp.float32(QK_HEAD_DIM ** -0.5 * math.log2(math.e))
    wq = (w_uq * qscale).reshape(Q_LORA_RANK, N_HEADS, QK_HEAD_DIM)
    wq_n = wq[:, :, :QK_NOPE_DIM]
    wq_r = wq[:, :, QK_NOPE_DIM:]
    to_t = lambda t: t.reshape(Q_LORA_RANK, -1).T
    wkv = w_ukv.reshape(KV_LORA_RANK, N_HEADS, QK_NOPE_DIM + V_HEAD_DIM)
    w_r = jnp.concatenate(
        [w_router_group, w_router_expert,
         jnp.zeros((D_MODEL, LANES - N_EXPERT_GROUPS - N_EXPERTS), F32)], axis=1)
    wr_hi = w_r.astype(BF16)
    b_r = jnp.concatenate([b_router_group, b_router_expert,
                           jnp.zeros((LANES - N_EXPERT_GROUPS - N_EXPERTS,), F32)])
    return {
        'g_attn': g_attn_norm[None, :], 'w_in': w_in_ext.astype(BF16), 'cs': cs.astype(BF16),
        'g_q': g_q_latent[None, :], 'g_kv': g_kv_latent[None, :],
        'wqn_t': to_t(wq_n).astype(BF16), 'wqr_t': to_t(wq_r).astype(BF16),
        'wuk': wkv[:, :, :QK_NOPE_DIM].reshape(KV_LORA_RANK, -1).astype(BF16),
        'wuv_t': wkv[:, :, QK_NOPE_DIM:].reshape(KV_LORA_RANK, -1).T.astype(BF16),
        'g_f': g_out_fourier[None, :], 'g_a': g_out_mla[None, :], 'w_out': w_out.astype(BF16),
        'g_ffn': g_ffn_norm[None, :],
        'wr': jnp.concatenate([wr_hi, (w_r - wr_hi.astype(F32)).astype(BF16)], axis=1),
        'b_r': b_r[None, :],
        'w_gate': w_gate, 'w_up': w_up, 'w_down': w_down,
    }


def _block_plan(counts, n_assign):
    counts = counts[0, N_EXPERT_GROUPS:N_EXPERT_GROUPS + N_EXPERTS].astype(jnp.int32)
    nblk = (counts + EXPERT_TILE - 1) // EXPERT_TILE
    blk_end = jnp.cumsum(nblk)
    blk_start = blk_end - nblk
    n_blocks = (n_assign + N_EXPERTS * (EXPERT_TILE - 1) + EXPERT_TILE - 1) // EXPERT_TILE
    n_used = blk_end[-1]
    j = jnp.minimum(jnp.arange(n_blocks, dtype=jnp.int32), n_used - 1)
    block_expert = jnp.minimum(jnp.sum((blk_end[None, :] <= j[:, None]).astype(jnp.int32), axis=1), N_EXPERTS - 1)
    return blk_start, blk_end, block_expert, n_used[None], n_blocks * EXPERT_TILE


def _dest_rows(route, blk_start):
    T = route.shape[1]
    e = route[:TOP_K].astype(jnp.int32)
    rank = route[TOP_K:2 * TOP_K].astype(jnp.int32)
    first = jnp.zeros_like(e)
    for j in range(N_EXPERTS):
        first = jnp.where(e == j, blk_start[j], first)
    dest = (first * EXPERT_TILE + rank).reshape(TOP_K, T // TOKEN_TILE, TOKEN_TILE)
    return dest.transpose(1, 0, 2).reshape(T // TOKEN_TILE, 1, TOP_K * TOKEN_TILE)


def _mixers(x, wt, counts0):
    B, S, _ = x.shape
    T = B * S
    xr, xi, qt, kx, vt = _proj(x, wt)
    f_out = _seq_dft(xr, xi, wt)
    a_out = _attention(qt, kx, vt)
    flat = lambda t: t.reshape((T,) + t.shape[2:])
    return _merge(flat(x), flat(f_out), flat(a_out), wt, counts0)


def kernel(x_prompt, x_sample, g_attn_norm, w_in, g_q_latent, w_uq, g_kv_latent, w_ukv, g_out_fourier, g_out_mla, w_out, g_ffn_norm, w_router_group, b_router_group, w_router_expert, b_router_expert, w_gate, w_up, w_down, g_final):
    assert g_attn_norm.shape[0] == 1, "single-layer configuration"
    w = _layer_weights(g_attn_norm[0], w_in[0], g_q_latent[0], w_uq[0], g_kv_latent[0], w_ukv[0],
                       g_out_fourier[0], g_out_mla[0], w_out[0], g_ffn_norm[0], w_router_group[0],
                       b_router_group[0], w_router_expert[0], b_router_expert[0], w_gate[0], w_up[0], w_down[0])
    assert x_prompt.shape[1] == x_sample.shape[1]
    w.update(_tables(x_prompt.shape[1]))

    counts = jnp.zeros((1, LANES), F32)
    batches = []
    for x in (x_prompt, x_sample):
        h, hn3, route, w0, w1, counts = _mixers(x, w, counts)
        batches.append((x.shape, h, hn3, route, w0, w1))

    n_assign = sum(b[1].shape[0] for b in batches) * TOP_K
    blk_start, blk_end, block_expert, n_used, n_rows = _block_plan(counts, n_assign)
    dests = [_dest_rows(b[3], blk_start) for b in batches]
    xs = _dispatch(blk_end, n_used, jnp.concatenate(dests, axis=0), batches[0][2], batches[1][2], n_rows)
    y = _experts(block_expert, n_used, blk_end, xs, w)
    outs = []
    for (shape, h, _, _, w0, w1), dest in zip(batches, dests):
        outs.append(_combine(dest, h, w0, w1, g_final[None, :], y).reshape(shape))
    return tuple(outs)
```

```python
import functools
import math

import jax
import jax.numpy as jnp
from jax import lax
from jax.experimental import pallas as pl
from jax.experimental.pallas import tpu as pltpu

D_MODEL = 2048
FOURIER_WIDTH = 1024
N_FOURIER_GROUPS = 4
FOURIER_GROUP_DIM = 256
MLA_WIDTH = 1024
V_HEAD_DIM = 128
N_HEADS = 8
QK_NOPE_DIM = 128
QK_ROPE_DIM = 64
QK_HEAD_DIM = 192
Q_LORA_RANK = 512
KV_LORA_RANK = 256
ROPE_THETA = 10000.0
N_EXPERT_GROUPS = 4
EXPERTS_PER_GROUP = 8
N_EXPERTS = 32
TOP_K = 2
EXPERT_DIM = 512
EPS = 1e-6

LANES = 128
QK_PAD = 256
V_ROWS = V_HEAD_DIM + 16
ROW_TILES = D_MODEL // LANES
ROUTE_ROWS = 8

DFT_N1 = 64
DFT_N2 = 128

TOKEN_TILE = 512
ATTN_TQ = 1024
ATTN_TK = 512
ATTN_SUB = 8
ATTN_QBLOCKS = 4
DFT_ROWS = 16
EXPERT_TILE = 256
VMEM_LIMIT = 56 * 1024 * 1024

F32 = jnp.float32
BF16 = jnp.bfloat16


def _const_spec(shape):
    nd = len(shape)
    return pl.BlockSpec(shape, lambda *_: (0,) * nd, pipeline_mode=pl.Buffered(1))


def _rms(x, g):
    return x * lax.rsqrt(jnp.mean(x * x, axis=-1, keepdims=True) + EPS) * g


def _proj_kernel(x_ref, g_attn_ref, w_in_ref, cs_ref, g_q_ref, g_kv_ref, wqn_ref, wqr_ref,
                 wuk_ref, wuvt_ref, cos_t_ref, sin_t_ref, cos_p_ref, sin_p_ref,
                 xr_ref, xi_ref, qt_ref, kx_ref, vt_ref):
    tm = x_ref.shape[1]
    x = x_ref[0]
    u = _rms(x, g_attn_ref[...]).astype(BF16)
    z = jnp.dot(u, w_in_ref[...], preferred_element_type=F32)

    zf = z[:, :FOURIER_WIDTH].astype(BF16)
    xr, xi = [], []
    for g in range(N_FOURIER_GROUPS):
        zg = zf[:, g * FOURIER_GROUP_DIM:(g + 1) * FOURIER_GROUP_DIM]
        xg = jnp.dot(zg, cs_ref[...], preferred_element_type=F32)
        xr.append(xg[:, :FOURIER_GROUP_DIM])
        xi.append(xg[:, FOURIER_GROUP_DIM:])
    xr_ref[0] = jnp.concatenate(xr, axis=-1).astype(BF16)
    xi_ref[0] = jnp.concatenate(xi, axis=-1).astype(BF16)

    o = FOURIER_WIDTH
    q_lat = z[:, o:o + Q_LORA_RANK]
    o += Q_LORA_RANK
    kv_lat = z[:, o:o + KV_LORA_RANK]
    o += KV_LORA_RANK
    kr = z[:, o:o + LANES]
    kr_rot = z[:, o + LANES:o + 2 * LANES]

    qn = _rms(q_lat, g_q_ref[...]).astype(BF16)
    kvn = _rms(kv_lat, g_kv_ref[...]).astype(BF16)

    nt = (((1,), (1,)), ((), ()))
    qn_t = lax.dot_general(wqn_ref[...], qn, nt, preferred_element_type=F32)
    qr_t = lax.dot_general(wqr_ref[...], qn, nt, preferred_element_type=F32)
    cos_t = cos_t_ref[...]
    sin_t = sin_t_ref[...]
    zpad = jnp.zeros((QK_PAD - QK_HEAD_DIM, tm), BF16)
    half = QK_ROPE_DIM // 2
    for h in range(N_HEADS):
        r0 = h * QK_PAD
        qt_ref[0, 0, r0:r0 + QK_NOPE_DIM, :] = qn_t[h * QK_NOPE_DIM:(h + 1) * QK_NOPE_DIM].astype(BF16)
        qr = qr_t[h * QK_ROPE_DIM:(h + 1) * QK_ROPE_DIM]
        rot = jnp.concatenate([-qr[half:], qr[:half]], axis=0)
        roped = qr * cos_t + rot * sin_t
        qt_ref[0, 0, r0 + QK_NOPE_DIM:r0 + QK_HEAD_DIM, :] = roped.astype(BF16)
        qt_ref[0, 0, r0 + QK_HEAD_DIM:r0 + QK_PAD, :] = zpad

    k_nope = jnp.dot(kvn, wuk_ref[...], preferred_element_type=F32)
    k_rope = (kr * cos_p_ref[...] + kr_rot * sin_p_ref[...]).astype(BF16)
    for h in range(N_HEADS):
        kx_ref[0, h, :, :QK_NOPE_DIM] = k_nope[:, h * QK_NOPE_DIM:(h + 1) * QK_NOPE_DIM].astype(BF16)
        kx_ref[0, h, :, QK_NOPE_DIM:] = k_rope

    v_t = lax.dot_general(wuvt_ref[...], kvn, nt, preferred_element_type=F32)
    vt_ref[0, 0, :, :V_HEAD_DIM, :] = v_t.astype(BF16).reshape(N_HEADS, V_HEAD_DIM, tm)
    vt_ref[0, 0, :, V_HEAD_DIM:, :] = jnp.ones((N_HEADS, V_ROWS - V_HEAD_DIM, tm), BF16)


def _proj(x, w):
    B, S, _ = x.shape
    tm = ATTN_TK
    q_split = ATTN_TQ // tm
    grid = (B, S // tm)
    tok = lambda b, i: (b, i, 0)
    out_shape = [
        jax.ShapeDtypeStruct((B, S, FOURIER_WIDTH), BF16),
        jax.ShapeDtypeStruct((B, S, FOURIER_WIDTH), BF16),
        jax.ShapeDtypeStruct((B, S // ATTN_TQ, N_HEADS * QK_PAD, ATTN_TQ), BF16),
        jax.ShapeDtypeStruct((B, N_HEADS, S, QK_PAD), BF16),
        jax.ShapeDtypeStruct((B, S // tm, N_HEADS, V_ROWS, tm), BF16),
    ]
    in_specs = [
        pl.BlockSpec((1, tm, D_MODEL), tok),
        _const_spec((1, D_MODEL)),
        _const_spec(w['w_in'].shape),
        _const_spec(w['cs'].shape),
        _const_spec((1, Q_LORA_RANK)),
        _const_spec((1, KV_LORA_RANK)),
        _const_spec(w['wqn_t'].shape),
        _const_spec(w['wqr_t'].shape),
        _const_spec(w['wuk'].shape),
        _const_spec(w['wuv_t'].shape),
        pl.BlockSpec((QK_ROPE_DIM, tm), lambda b, i: (0, i)),
        pl.BlockSpec((QK_ROPE_DIM, tm), lambda b, i: (0, i)),
        pl.BlockSpec((tm, LANES), lambda b, i: (i, 0)),
        pl.BlockSpec((tm, LANES), lambda b, i: (i, 0)),
    ]
    out_specs = [
        pl.BlockSpec((1, tm, FOURIER_WIDTH), tok),
        pl.BlockSpec((1, tm, FOURIER_WIDTH), tok),
        pl.BlockSpec((1, 1, N_HEADS * QK_PAD, tm), lambda b, i: (b, i // q_split, 0, i % q_split)),
        pl.BlockSpec((1, N_HEADS, tm, QK_PAD), lambda b, i: (b, 0, i, 0)),
        pl.BlockSpec((1, 1, N_HEADS, V_ROWS, tm), lambda b, i: (b, i, 0, 0, 0)),
    ]
    return pl.pallas_call(
        _proj_kernel, grid=grid, in_specs=in_specs, out_specs=out_specs, out_shape=out_shape,
        compiler_params=pltpu.CompilerParams(
            dimension_semantics=("arbitrary", "arbitrary"), vmem_limit_bytes=VMEM_LIMIT),
        name="proj",
    )(x, w['g_attn'], w['w_in'], w['cs'], w['g_q'], w['g_kv'], w['wqn_t'], w['wqr_t'],
      w['wuk'], w['wuv_t'], w['cos_t'], w['sin_t'], w['cos_p'], w['sin_p'])


def _dft1_kernel(m1_ref, xr_ref, xi_ref, zr_ref, zi_ref):
    nb, c = xr_ref.shape[2], xr_ref.shape[3]
    xr = xr_ref[0].reshape(DFT_N1, nb * c)
    xi = xi_ref[0].reshape(DFT_N1, nb * c)
    xc = jnp.concatenate([xr, xi], axis=0)
    z = jnp.dot(m1_ref[...], xc, preferred_element_type=F32).astype(BF16)
    zr_ref[0] = z[:DFT_N1].reshape(DFT_N1, nb, c)
    zi_ref[0] = z[DFT_N1:].reshape(DFT_N1, nb, c)


def _dft2_kernel(g_ref, zr_ref, zi_ref, o_ref):
    kb, c = zr_ref.shape[1], zr_ref.shape[3]
    ys = []
    for j in range(kb):
        zc = jnp.concatenate([zr_ref[0, j], zi_ref[0, j]], axis=0)
        ys.append(jnp.dot(g_ref[j], zc, preferred_element_type=F32).astype(BF16))
    o_ref[0] = jnp.concatenate(ys, axis=-1).reshape(DFT_N2, kb, c)


def _seq_dft(xr, xi, w):
    B, S, C = xr.shape
    assert S == DFT_N1 * DFT_N2
    nb = DFT_ROWS
    xr4 = xr.reshape(B, DFT_N1, DFT_N2, C)
    xi4 = xi.reshape(B, DFT_N1, DFT_N2, C)
    rows = pl.BlockSpec((1, DFT_N1, nb, C), lambda b, j: (b, 0, j, 0))
    zr, zi = pl.pallas_call(
        _dft1_kernel, grid=(B, DFT_N2 // nb),
        in_specs=[_const_spec((2 * DFT_N1, 2 * DFT_N1)), rows, rows],
        out_specs=[rows, rows],
        out_shape=[jax.ShapeDtypeStruct((B, DFT_N1, DFT_N2, C), BF16)] * 2,
        compiler_params=pltpu.CompilerParams(
            dimension_semantics=("arbitrary", "arbitrary"), vmem_limit_bytes=VMEM_LIMIT),
        name="dft1",
    )(w['dft_m1'], xr4, xi4)
    kb = DFT_ROWS
    blk = pl.BlockSpec((1, kb, DFT_N2, C), lambda b, k: (b, k, 0, 0))
    y = pl.pallas_call(
        _dft2_kernel, grid=(B, DFT_N1 // kb),
        in_specs=[pl.BlockSpec((kb, DFT_N2, 2 * DFT_N2), lambda b, k: (k, 0, 0)), blk, blk],
        out_specs=pl.BlockSpec((1, DFT_N2, kb, C), lambda b, k: (b, 0, k, 0)),
        out_shape=jax.ShapeDtypeStruct((B, DFT_N2, DFT_N1, C), BF16),
        compiler_params=pltpu.CompilerParams(
            dimension_semantics=("arbitrary", "arbitrary"), vmem_limit_bytes=VMEM_LIMIT),
        name="dft2",
    )(w['dft_g'], zr, zi)
    return y.reshape(B, S, C)


def _attn_kernel(q_ref, k_ref, v_ref, o_ref, s_scr, acc_ref):
    tk = v_ref.shape[-1]
    nch = v_ref.shape[1]
    nq, tq = q_ref.shape[1], q_ref.shape[-1]
    groups = nch // ATTN_SUB

    def scores(f):
        j = jnp.minimum(f // nch, nq - 1)
        c = lax.rem(f, nch)
        k_c = k_ref[0, 0, pl.ds(pl.multiple_of(c * tk, tk), tk), :]
        return jnp.dot(k_c, q_ref[0, j], preferred_element_type=F32)

    def step(f, slot, m, s_max):
        s_next = scores(f + 1)
        s_scr[1 - slot] = s_next
        next_max = jnp.max(s_next, axis=0, keepdims=True)
        s = s_scr[slot]
        m_new = jnp.maximum(m, s_max)
        alpha = jnp.exp2(m - m_new)
        p = jnp.exp2(s - m_new)
        pv = jnp.dot(v_ref[0, lax.rem(f, nch), 0], p.astype(BF16), preferred_element_type=F32)
        acc_ref[...] = alpha * acc_ref[...] + pv
        return m_new, next_max

    acc_ref[...] = jnp.zeros(acc_ref.shape, F32)
    s_first = scores(0)
    s_scr[0] = s_first
    m0 = jnp.full((1, tq), -1e30, F32)

    def body(g, carry):
        m, s_max = carry
        for si in range(ATTN_SUB):
            m, s_max = step(g * ATTN_SUB + si, si % 2, m, s_max)
        block_done = lax.rem(g, groups) == groups - 1

        @pl.when(block_done)
        def _():
            j = g // groups
            o_t = acc_ref[:V_HEAD_DIM] * (1.0 / acc_ref[V_HEAD_DIM:V_HEAD_DIM + 1])
            o_ref[0, pl.ds(pl.multiple_of(j * tq, tq), tq), :] = o_t.T.astype(BF16)
            acc_ref[...] = jnp.zeros(acc_ref.shape, F32)

        return jnp.where(block_done, m0, m), s_max

    lax.fori_loop(0, nq * groups, body, (m0, jnp.max(s_first, axis=0, keepdims=True)))


def _attention(qt, kx, vt):
    B, nqb, _, tq = qt.shape
    S = nqb * tq
    nch, tk = vt.shape[1], vt.shape[-1]
    nq = ATTN_QBLOCKS
    assert ATTN_SUB % 2 == 0 and nch % ATTN_SUB == 0 and nqb % nq == 0
    return pl.pallas_call(
        _attn_kernel, grid=(B, N_HEADS, nqb // nq),
        in_specs=[pl.BlockSpec((1, nq, QK_PAD, tq), lambda b, h, i: (b, i, h, 0)),
                  pl.BlockSpec((1, 1, S, QK_PAD), lambda b, h, i: (b, h, 0, 0)),
                  pl.BlockSpec((1, nch, 1, V_ROWS, tk), lambda b, h, i: (b, 0, h, 0, 0))],
        out_specs=pl.BlockSpec((1, nq * tq, V_HEAD_DIM), lambda b, h, i: (b, i, h)),
        out_shape=jax.ShapeDtypeStruct((B, S, MLA_WIDTH), BF16),
        scratch_shapes=[pltpu.VMEM((2, tk, tq), F32), pltpu.VMEM((V_ROWS, tq), F32)],
        compiler_params=pltpu.CompilerParams(
            dimension_semantics=("arbitrary", "arbitrary", "arbitrary"), vmem_limit_bytes=VMEM_LIMIT),
        name="attn",
    )(qt, kx, vt)


def _merge_kernel(x_ref, f_ref, a_ref, g_f_ref, g_a_ref, w_out_ref, g_ffn_ref, wr_ref, b_r_ref, tri_ref, cnt0_ref,
                  h_ref, hn_ref, route_ref, w0_ref, w1_ref, cnt_ref, carry_ref):
    tm = x_ref.shape[0]

    @pl.when(pl.program_id(0) == 0)
    def _():
        carry_ref[...] = cnt0_ref[...]

    fn = _rms(f_ref[...].astype(F32), g_f_ref[...]).astype(BF16)
    an = _rms(a_ref[...].astype(F32), g_a_ref[...]).astype(BF16)
    merged = jnp.concatenate([fn, an], axis=-1)
    h = x_ref[...] + jnp.dot(merged, w_out_ref[...], preferred_element_type=F32)
    h_ref[...] = h
    hn = _rms(h, g_ffn_ref[...])
    hn_ref[...] = hn.reshape(tm, ROW_TILES, LANES).astype(BF16)
    carry = carry_ref[...]

    hn_hi = hn.astype(BF16)
    hn_lo = (hn - hn_hi.astype(F32)).astype(BF16)
    p_hi = jnp.dot(hn_hi, wr_ref[...], preferred_element_type=F32)
    p_lo = jnp.dot(hn_lo, wr_ref[...], preferred_element_type=F32)
    logits = ((p_hi[:, :LANES] + p_hi[:, LANES:]) + (p_lo[:, :LANES] + p_lo[:, LANES:])
              + b_r_ref[...])

    lane = lax.broadcasted_iota(jnp.int32, logits.shape, 1)
    neg = jnp.float32(-1e30)
    big = jnp.int32(LANES)
    is_g = lane < N_EXPERT_GROUPS
    gl = jnp.where(is_g, logits, neg)
    gmax = jnp.max(gl, axis=-1, keepdims=True)
    g_sel = jnp.min(jnp.where(gl == gmax, lane, big), axis=-1, keepdims=True)
    g_p = 1.0 / jnp.sum(jnp.where(is_g, jnp.exp(gl - gmax), 0.0), axis=-1, keepdims=True)
    e_lo = N_EXPERT_GROUPS + EXPERTS_PER_GROUP * g_sel
    in_grp = jnp.where(lane >= e_lo, jnp.where(lane < e_lo + EXPERTS_PER_GROUP, 1, 0), 0) == 1
    el = jnp.where(in_grp, logits, neg)
    m1 = jnp.max(el, axis=-1, keepdims=True)
    i1 = jnp.min(jnp.where(el == m1, lane, big), axis=-1, keepdims=True)
    el2 = jnp.where(lane == i1, neg, el)
    m2 = jnp.max(el2, axis=-1, keepdims=True)
    i2 = jnp.min(jnp.where(el2 == m2, lane, big), axis=-1, keepdims=True)
    t = jnp.exp(m2 - m1)
    inv = 1.0 / (1.0 + t)
    w0 = g_p * inv
    w1 = g_p * t * inv
    e0 = (i1 - N_EXPERT_GROUPS).astype(F32)
    e1 = (i2 - N_EXPERT_GROUPS).astype(F32)

    oh0 = jnp.where(lane == i1, 1.0, 0.0)
    oh1 = jnp.where(lane == i2, 1.0, 0.0)
    oh = oh0 + oh1
    before = jnp.dot(tri_ref[...], oh.astype(BF16), preferred_element_type=F32) + carry
    r0 = jnp.sum(before * oh0, axis=-1, keepdims=True)
    r1 = jnp.sum(before * oh1, axis=-1, keepdims=True)

    route = jnp.where(lane == 0, e0, jnp.where(lane == 1, e1, jnp.where(lane == 2, r0, jnp.where(lane == 3, r1, 0.0))))
    route_ref[...] = route.T[:ROUTE_ROWS]
    w0_ref[...] = jnp.broadcast_to(w0, logits.shape)
    w1_ref[...] = jnp.broadcast_to(w1, logits.shape)
    total = carry + jnp.sum(oh, axis=0, keepdims=True)
    carry_ref[...] = total
    cnt_ref[...] = total


def _merge(x, f_out, a_out, w, counts0):
    T = x.shape[0]
    tm = TOKEN_TILE
    n = T // tm
    cur = lambda t: (t, 0)
    out_shape = [
        jax.ShapeDtypeStruct((T, D_MODEL), F32),
        jax.ShapeDtypeStruct((T, ROW_TILES, LANES), BF16),
        jax.ShapeDtypeStruct((ROUTE_ROWS, T), F32),
        jax.ShapeDtypeStruct((T, LANES), F32),
        jax.ShapeDtypeStruct((T, LANES), F32),
        jax.ShapeDtypeStruct((1, LANES), F32),
    ]
    in_specs = [
        pl.BlockSpec((tm, D_MODEL), cur),
        pl.BlockSpec((tm, FOURIER_WIDTH), cur),
        pl.BlockSpec((tm, MLA_WIDTH), cur),
        _const_spec((1, FOURIER_WIDTH)),
        _const_spec((1, MLA_WIDTH)),
        _const_spec((D_MODEL, D_MODEL)),
        _const_spec((1, D_MODEL)),
        _const_spec((D_MODEL, 2 * LANES)),
        _const_spec((1, LANES)),
        _const_spec((tm, tm)),
        _const_spec((1, LANES)),
    ]
    out_specs = [
        pl.BlockSpec((tm, D_MODEL), cur),
        pl.BlockSpec((tm, ROW_TILES, LANES), lambda t: (t, 0, 0)),
        pl.BlockSpec((ROUTE_ROWS, tm), lambda t: (0, t)),
        pl.BlockSpec((tm, LANES), cur),
        pl.BlockSpec((tm, LANES), cur),
        pl.BlockSpec((1, LANES), lambda t: (0, 0)),
    ]
    tri = (lax.broadcasted_iota(jnp.int32, (tm, tm), 1) < lax.broadcasted_iota(jnp.int32, (tm, tm), 0)).astype(BF16)
    return pl.pallas_call(
        _merge_kernel, grid=(n,), in_specs=in_specs, out_specs=out_specs, out_shape=out_shape,
        scratch_shapes=[pltpu.VMEM((1, LANES), F32)],
        compiler_params=pltpu.CompilerParams(dimension_semantics=("arbitrary",), vmem_limit_bytes=VMEM_LIMIT),
        name="merge",
    )(x, f_out, a_out, w['g_f'], w['g_a'], w['w_out'], w['g_ffn'], w['wr'], w['b_r'], tri, counts0)


def _dispatch_kernel(blk_end_ref, nused_ref, dest_ref, hn_a_ref, hn_b_ref, xs_ref, zero_buf, sem, zsem, *, n_a):
    i = pl.program_id(0)
    tm = hn_a_ref.shape[0]
    nb = xs_ref.shape[0] // EXPERT_TILE

    @pl.when(i == 0)
    def _():
        zero_buf[...] = jnp.zeros(zero_buf.shape, BF16)

        def zero_block(j):
            return pltpu.make_async_copy(zero_buf, xs_ref.at[pl.ds(j * EXPERT_TILE, EXPERT_TILE)], zsem)

        def each_group(fn):
            for e in range(N_EXPERTS):
                end = blk_end_ref[e]
                start = blk_end_ref[e - 1] if e else 0

                @pl.when(end > start)
                def _():
                    fn(end - 1)

        def each_tail(fn):
            def body(j, carry):
                fn(j)
                return carry

            lax.fori_loop(nused_ref[0], nb, body, 0)

        each_group(lambda j: zero_block(j).start())
        each_tail(lambda j: zero_block(j).start())
        each_group(lambda j: zero_block(j).wait())
        each_tail(lambda j: zero_block(j).wait())

    def scatter(src_ref):
        def body(r, carry):
            for k in range(TOP_K):
                d = dest_ref[0, 0, k * tm + r]
                pltpu.make_async_copy(src_ref.at[r], xs_ref.at[d], sem).start(priority=k)
            return carry

        lax.fori_loop(0, tm, body, 0)
        for _ in range(TOP_K):
            pltpu.make_async_copy(src_ref, xs_ref.at[pl.ds(0, tm)], sem).wait()

    @pl.when(i < n_a)
    def _():
        scatter(hn_a_ref)

    @pl.when(i >= n_a)
    def _():
        scatter(hn_b_ref)


def _dispatch(blk_end, n_used, dest, hn_a, hn_b, n_rows):
    tm = TOKEN_TILE
    n_a, n_b = hn_a.shape[0] // tm, hn_b.shape[0] // tm
    tile = (tm, ROW_TILES, LANES)
    grid_spec = pltpu.PrefetchScalarGridSpec(
        num_scalar_prefetch=2, grid=(n_a + n_b,),
        in_specs=[pl.BlockSpec((1, 1, TOP_K * tm), lambda i, be, nu: (i, 0, 0), memory_space=pltpu.SMEM),
                  pl.BlockSpec(tile, lambda i, be, nu: (jnp.minimum(i, n_a - 1), 0, 0)),
                  pl.BlockSpec(tile, lambda i, be, nu: (jnp.maximum(i - n_a, 0), 0, 0))],
        out_specs=pl.BlockSpec(memory_space=pl.ANY),
        scratch_shapes=[pltpu.VMEM((EXPERT_TILE, ROW_TILES, LANES), BF16),
                        pltpu.SemaphoreType.DMA(()), pltpu.SemaphoreType.DMA(())],
    )
    return pl.pallas_call(
        functools.partial(_dispatch_kernel, n_a=n_a), grid_spec=grid_spec,
        out_shape=jax.ShapeDtypeStruct((n_rows, ROW_TILES, LANES), BF16),
        compiler_params=pltpu.CompilerParams(dimension_semantics=("arbitrary",), vmem_limit_bytes=VMEM_LIMIT),
        name="dispatch",
    )(blk_end, n_used, dest, hn_a, hn_b)


def _expert_kernel(be_ref, nused_ref, blk_end_ref, xs_ref, wg_ref, wu_ref, wd_ref, y_ref,
                   wg_f, wu_f, wd_f, wgu_s, wd_s, group_ref, sems):
    i = pl.program_id(0)
    e = be_ref[i]

    def weight_copies(expert, slot):
        return [pltpu.make_async_copy(src.at[expert], dst.at[slot], sems.at[slot])
                for src, dst in ((wg_ref, wg_f), (wu_ref, wu_f), (wd_ref, wd_f))]

    @pl.when(i == 0)
    def _():
        group_ref[0] = 0
        for cp in weight_copies(e, 0):
            cp.start()

    @pl.when((i == 0) | (e != be_ref[jnp.maximum(i - 1, 0)]))
    def _():
        slot = lax.rem(group_ref[0], 2)
        for cp in weight_copies(e, slot):
            cp.wait()
        wgu_s[:, :EXPERT_DIM] = wg_f[slot].astype(BF16)
        wgu_s[:, EXPERT_DIM:] = wu_f[slot].astype(BF16)
        wd_s[...] = wd_f[slot].astype(BF16)
        nxt = blk_end_ref[e]

        @pl.when(nxt < nused_ref[0])
        def _():
            for cp in weight_copies(be_ref[nxt], 1 - slot):
                cp.start()

        group_ref[0] = group_ref[0] + 1

    @pl.when(i < nused_ref[0])
    def _():
        tm = xs_ref.shape[0]
        x = xs_ref[...].reshape(tm, D_MODEL)
        gu = jnp.dot(x, wgu_s[...], preferred_element_type=F32)
        g = gu[:, :EXPERT_DIM]
        u = gu[:, EXPERT_DIM:]
        hmid = (g * jax.nn.sigmoid(g) * u).astype(BF16)
        y = jnp.dot(hmid, wd_s[...], preferred_element_type=F32)
        y_ref[...] = y.reshape(tm, ROW_TILES, LANES).astype(BF16)

    @pl.when(i >= nused_ref[0])
    def _():
        y_ref[...] = jnp.zeros(y_ref.shape, BF16)


def _experts(block_expert, n_used, blk_end, xs, w):
    n_rows = xs.shape[0]
    tm = EXPERT_TILE
    nb = n_rows // tm
    row = lambda i, be, nu, bend: (jnp.minimum(i, nu[0] - 1), 0, 0)
    out_row = lambda i, be, nu, bend: (i, 0, 0)
    hbm = pl.BlockSpec(memory_space=pl.ANY)
    grid_spec = pltpu.PrefetchScalarGridSpec(
        num_scalar_prefetch=3, grid=(nb,),
        in_specs=[pl.BlockSpec((tm, ROW_TILES, LANES), row), hbm, hbm, hbm],
        out_specs=pl.BlockSpec((tm, ROW_TILES, LANES), out_row),
        scratch_shapes=[pltpu.VMEM((2, D_MODEL, EXPERT_DIM), F32), pltpu.VMEM((2, D_MODEL, EXPERT_DIM), F32),
                        pltpu.VMEM((2, EXPERT_DIM, D_MODEL), F32),
                        pltpu.VMEM((D_MODEL, 2 * EXPERT_DIM), BF16), pltpu.VMEM((EXPERT_DIM, D_MODEL), BF16),
                        pltpu.SMEM((1,), jnp.int32), pltpu.SemaphoreType.DMA((2,))],
    )
    return pl.pallas_call(
        _expert_kernel, grid_spec=grid_spec,
        out_shape=jax.ShapeDtypeStruct((n_rows, ROW_TILES, LANES), BF16),
        compiler_params=pltpu.CompilerParams(dimension_semantics=("arbitrary",), vmem_limit_bytes=VMEM_LIMIT),
        name="experts",
    )(block_expert, n_used, blk_end, xs, w['w_gate'], w['w_up'], w['w_down'])


def _combine_kernel(dest_ref, dest_next_ref, h_ref, w0_ref, w1_ref, g_ref, y_ref, o_ref, ybuf, sems, *, n_steps):
    tm = h_ref.shape[0]
    i = pl.program_id(0)
    slot = lax.rem(i, 2)

    def start_gather(d_ref, s):
        def body(r, carry):
            for k in range(TOP_K):
                d = d_ref[0, 0, k * tm + r]
                pltpu.make_async_copy(y_ref.at[d], ybuf.at[s, k, r], sems.at[s]).start(priority=k)
            return carry

        lax.fori_loop(0, tm, body, 0)

    @pl.when(i == 0)
    def _():
        start_gather(dest_ref, 0)

    @pl.when(i + 1 < n_steps)
    def _():
        start_gather(dest_next_ref, 1 - slot)

    for k in range(TOP_K):
        pltpu.make_async_copy(y_ref.at[pl.ds(0, tm)], ybuf.at[slot, k], sems.at[slot]).wait()

    y0 = ybuf[slot, 0].reshape(tm, D_MODEL)
    y1 = ybuf[slot, 1].reshape(tm, D_MODEL)
    w0 = w0_ref[...]
    w1 = w1_ref[...]
    cols = []
    for j in range(ROW_TILES):
        sl = slice(j * LANES, (j + 1) * LANES)
        cols.append(h_ref[:, sl] + w0 * y0[:, sl].astype(F32) + w1 * y1[:, sl].astype(F32))
    hs = jnp.concatenate(cols, axis=-1)
    o_ref[...] = _rms(hs, g_ref[...])


def _combine(dest, h, w0, w1, g_final, y):
    T = h.shape[0]
    tm = TOKEN_TILE
    n = T // tm
    dest3 = dest
    return pl.pallas_call(
        functools.partial(_combine_kernel, n_steps=n), grid=(n,),
        in_specs=[pl.BlockSpec((1, 1, TOP_K * tm), lambda i: (i, 0, 0), memory_space=pltpu.SMEM),
                  pl.BlockSpec((1, 1, TOP_K * tm), lambda i: (jnp.minimum(i + 1, n - 1), 0, 0),
                               memory_space=pltpu.SMEM),
                  pl.BlockSpec((tm, D_MODEL), lambda i: (i, 0)),
                  pl.BlockSpec((tm, LANES), lambda i: (i, 0)),
                  pl.BlockSpec((tm, LANES), lambda i: (i, 0)),
                  _const_spec((1, D_MODEL)),
                  pl.BlockSpec(memory_space=pl.ANY)],
        out_specs=pl.BlockSpec((tm, D_MODEL), lambda i: (i, 0)),
        out_shape=jax.ShapeDtypeStruct((T, D_MODEL), F32),
        scratch_shapes=[pltpu.VMEM((2, TOP_K, tm, ROW_TILES, LANES), BF16), pltpu.SemaphoreType.DMA((2,))],
        compiler_params=pltpu.CompilerParams(dimension_semantics=("arbitrary",), vmem_limit_bytes=VMEM_LIMIT),
        name="combine",
    )(dest3, dest3, h, w0, w1, g_final, y)


def _tables(S):
    pos = jnp.arange(S, dtype=F32)
    inv_freq = 1.0 / (ROPE_THETA ** (jnp.arange(0, QK_ROPE_DIM, 2, dtype=F32) / QK_ROPE_DIM))
    ang = pos[:, None] * inv_freq[None, :]
    cos, sin = jnp.cos(ang), jnp.sin(ang)
    cos2 = jnp.concatenate([cos, cos], axis=-1)
    sin2 = jnp.concatenate([sin, sin], axis=-1)
    zp = jnp.zeros((S, LANES - QK_ROPE_DIM), F32)
    t = {
        'cos_t': cos2.T, 'sin_t': sin2.T,
        'cos_p': jnp.concatenate([cos2, zp], axis=-1), 'sin_p': jnp.concatenate([sin2, zp], axis=-1),
    }

    def phase(rows, cols, n):
        m = (rows[:, None] * cols[None, :]) % n
        a = m.astype(F32) * jnp.float32(2.0 * math.pi / n)
        return jnp.cos(a), jnp.sin(a)

    i1 = jnp.arange(DFT_N1, dtype=jnp.int32)
    c1, s1 = phase(i1, i1, DFT_N1)
    t['dft_m1'] = jnp.concatenate(
        [jnp.concatenate([c1, s1], axis=1), jnp.concatenate([-s1, c1], axis=1)], axis=0).astype(BF16)
    k = jnp.arange(S, dtype=jnp.int32)
    n2 = jnp.arange(DFT_N2, dtype=jnp.int32)
    cg, sg = phase(k, n2, S)
    g = jnp.concatenate([cg, sg], axis=1) * jnp.float32(S ** -0.5)
    t['dft_g'] = g.reshape(DFT_N2, DFT_N1, 2 * DFT_N2).transpose(1, 0, 2).astype(BF16)
    return t


def _layer_weights(g_attn_norm, w_in, g_q_latent, w_uq, g_kv_latent, w_ukv, g_out_fourier, g_out_mla, w_out,
                   g_ffn_norm, w_router_group, b_router_group, w_router_expert, b_router_expert,
                   w_gate, w_up, w_down):
    half = QK_ROPE_DIM // 2
    s3 = FOURIER_WIDTH + Q_LORA_RANK + KV_LORA_RANK
    w_kr = w_in[:, s3:s3 + QK_ROPE_DIM]
    w_kr_rot = jnp.concatenate([-w_kr[:, half:], w_kr[:, :half]], axis=1)
    zc = jnp.zeros((D_MODEL, LANES - QK_ROPE_DIM), F32)
    w_in_ext = jnp.concatenate([w_in[:, :s3], w_kr, zc, w_kr_rot, zc], axis=1)

    c = jnp.arange(FOURIER_GROUP_DIM, dtype=jnp.int32)
    m = (c[:, None] * c[None, :]) % FOURIER_GROUP_DIM
    a = m.astype(F32) * jnp.float32(2.0 * math.pi / FOURIER_GROUP_DIM)
    cs = jnp.concatenate([jnp.cos(a), -jnp.sin(a)], axis=1) * jnp.float32(FOURIER_GROUP_DIM ** -0.5)

    qscale = jnp.float32(QK_HEAD_DIM ** -0.5 * math.log2(math.e))
    wq = (w_uq * qscale).reshape(Q_LORA_RANK, N_HEADS, QK_HEAD_DIM)
    wq_n = wq[:, :, :QK_NOPE_DIM]
    wq_r = wq[:, :, QK_NOPE_DIM:]
    to_t = lambda t: t.reshape(Q_LORA_RANK, -1).T
    wkv = w_ukv.reshape(KV_LORA_RANK, N_HEADS, QK_NOPE_DIM + V_HEAD_DIM)
    w_r = jnp.concatenate(
        [w_router_group, w_router_expert,
         jnp.zeros((D_MODEL, LANES - N_EXPERT_GROUPS - N_EXPERTS), F32)], axis=1)
    wr_hi = w_r.astype(BF16)
    b_r = jnp.concatenate([b_router_group, b_router_expert,
                           jnp.zeros((LANES - N_EXPERT_GROUPS - N_EXPERTS,), F32)])
    return {
        'g_attn': g_attn_norm[None, :], 'w_in': w_in_ext.astype(BF16), 'cs': cs.astype(BF16),
        'g_q': g_q_latent[None, :], 'g_kv': g_kv_latent[None, :],
        'wqn_t': to_t(wq_n).astype(BF16), 'wqr_t': to_t(wq_r).astype(BF16),
        'wuk': wkv[:, :, :QK_NOPE_DIM].reshape(KV_LORA_RANK, -1).astype(BF16),
        'wuv_t': wkv[:, :, QK_NOPE_DIM:].reshape(KV_LORA_RANK, -1).T.astype(BF16),
        'g_f': g_out_fourier[None, :], 'g_a': g_out_mla[None, :], 'w_out': w_out.astype(BF16),
        'g_ffn': g_ffn_norm[None, :],
        'wr': jnp.concatenate([wr_hi, (w_r - wr_hi.astype(F32)).astype(BF16)], axis=1),
        'b_r': b_r[None, :],
        'w_gate': w_gate, 'w_up': w_up, 'w_down': w_down,
    }


def _block_plan(counts, n_assign):
    counts = counts[0, N_EXPERT_GROUPS:N_EXPERT_GROUPS + N_EXPERTS].astype(jnp.int32)
    nblk = (counts + EXPERT_TILE - 1) // EXPERT_TILE
    blk_end = jnp.cumsum(nblk)
    blk_start = blk_end - nblk
    n_blocks = (n_assign + N_EXPERTS * (EXPERT_TILE - 1) + EXPERT_TILE - 1) // EXPERT_TILE
    n_used = blk_end[-1]
    j = jnp.minimum(jnp.arange(n_blocks, dtype=jnp.int32), n_used - 1)
    block_expert = jnp.minimum(jnp.sum((blk_end[None, :] <= j[:, None]).astype(jnp.int32), axis=1), N_EXPERTS - 1)
    return blk_start, blk_end, block_expert, n_used[None], n_blocks * EXPERT_TILE


def _dest_rows(route, blk_start):
    T = route.shape[1]
    e = route[:TOP_K].astype(jnp.int32)
    rank = route[TOP_K:2 * TOP_K].astype(jnp.int32)
    first = jnp.zeros_like(e)
    for j in range(N_EXPERTS):
        first = jnp.where(e == j, blk_start[j], first)
    dest = (first * EXPERT_TILE + rank).reshape(TOP_K, T // TOKEN_TILE, TOKEN_TILE)
    return dest.transpose(1, 0, 2).reshape(T // TOKEN_TILE, 1, TOP_K * TOKEN_TILE)


def _mixers(x, wt, counts0):
    B, S, _ = x.shape
    T = B * S
    xr, xi, qt, kx, vt = _proj(x, wt)
    f_out = _seq_dft(xr, xi, wt)
    a_out = _attention(qt, kx, vt)
    flat = lambda t: t.reshape((T,) + t.shape[2:])
    return _merge(flat(x), flat(f_out), flat(a_out), wt, counts0)


def kernel(x_prompt, x_sample, g_attn_norm, w_in, g_q_latent, w_uq, g_kv_latent, w_ukv, g_out_fourier, g_out_mla, w_out, g_ffn_norm, w_router_group, b_router_group, w_router_expert, b_router_expert, w_gate, w_up, w_down, g_final):
    assert g_attn_norm.shape[0] == 1, "single-layer configuration"
    w = _layer_weights(g_attn_norm[0], w_in[0], g_q_latent[0], w_uq[0], g_kv_latent[0], w_ukv[0],
                       g_out_fourier[0], g_out_mla[0], w_out[0], g_ffn_norm[0], w_router_group[0],
                       b_router_group[0], w_router_expert[0], b_router_expert[0], w_gate[0], w_up[0], w_down[0])
    assert x_prompt.shape[1] == x_sample.shape[1]
    w.update(_tables(x_prompt.shape[1]))

    counts = jnp.zeros((1, LANES), F32)
    batches = []
    for x in (x_prompt, x_sample):
        h, hn3, route, w0, w1, counts = _mixers(x, w, counts)
        batches.append((x.shape, h, hn3, route, w0, w1))

    n_assign = sum(b[1].shape[0] for b in batches) * TOP_K
    blk_start, blk_end, block_expert, n_used, n_rows = _block_plan(counts, n_assign)
    dests = [_dest_rows(b[3], blk_start) for b in batches]
    xs = _dispatch(blk_end, n_used, jnp.concatenate(dests, axis=0), batches[0][2], batches[1][2], n_rows)
    y = _experts(block_expert, n_used, blk_end, xs, w)
    outs = []
    for (shape, h, _, _, w0, w1), dest in zip(batches, dests):
        outs.append(_combine(dest, h, w0, w1, g_final[None, :], y).reshape(shape))
    return tuple(outs)
```

```python
import functools
import math

import jax
import jax.numpy as jnp
from jax import lax
from jax.experimental import pallas as pl
from jax.experimental.pallas import tpu as pltpu

D_MODEL = 2048
FOURIER_WIDTH = 1024
N_FOURIER_GROUPS = 4
FOURIER_GROUP_DIM = 256
MLA_WIDTH = 1024
V_HEAD_DIM = 128
N_HEADS = 8
QK_NOPE_DIM = 128
QK_ROPE_DIM = 64
QK_HEAD_DIM = 192
Q_LORA_RANK = 512
KV_LORA_RANK = 256
ROPE_THETA = 10000.0
N_EXPERT_GROUPS = 4
EXPERTS_PER_GROUP = 8
N_EXPERTS = 32
TOP_K = 2
EXPERT_DIM = 512
EPS = 1e-6

LANES = 128
QK_PAD = 256
V_ROWS = V_HEAD_DIM + 16
ROW_TILES = D_MODEL // LANES
ROUTE_ROWS = 8

DFT_N1 = 64
DFT_N2 = 128

TOKEN_TILE = 512
ATTN_TQ = 1024
ATTN_TK = 512
ATTN_SUB = 8
ATTN_QBLOCKS = 4
DFT_ROWS = 16
EXPERT_TILE = 256
VMEM_LIMIT = 56 * 1024 * 1024

F32 = jnp.float32
BF16 = jnp.bfloat16


def _const_spec(shape):
    nd = len(shape)
    return pl.BlockSpec(shape, lambda *_: (0,) * nd, pipeline_mode=pl.Buffered(1))


def _rms(x, g):
    return x * lax.rsqrt(jnp.mean(x * x, axis=-1, keepdims=True) + EPS) * g


def _proj_kernel(x_ref, g_attn_ref, w_in_ref, cs_ref, g_q_ref, g_kv_ref, wqn_ref, wqr_ref,
                 wuk_ref, wuvt_ref, cos_t_ref, sin_t_ref, cos_p_ref, sin_p_ref,
                 xr_ref, xi_ref, qt_ref, kx_ref, vt_ref):
    tm = x_ref.shape[1]
    x = x_ref[0]
    u = _rms(x, g_attn_ref[...]).astype(BF16)
    z = jnp.dot(u, w_in_ref[...], preferred_element_type=F32)

    zf = z[:, :FOURIER_WIDTH].astype(BF16)
    xr, xi = [], []
    for g in range(N_FOURIER_GROUPS):
        zg = zf[:, g * FOURIER_GROUP_DIM:(g + 1) * FOURIER_GROUP_DIM]
        xg = jnp.dot(zg, cs_ref[...], preferred_element_type=F32)
        xr.append(xg[:, :FOURIER_GROUP_DIM])
        xi.append(xg[:, FOURIER_GROUP_DIM:])
    xr_ref[0] = jnp.concatenate(xr, axis=-1).astype(BF16)
    xi_ref[0] = jnp.concatenate(xi, axis=-1).astype(BF16)

    o = FOURIER_WIDTH
    q_lat = z[:, o:o + Q_LORA_RANK]
    o += Q_LORA_RANK
    kv_lat = z[:, o:o + KV_LORA_RANK]
    o += KV_LORA_RANK
    kr = z[:, o:o + LANES]
    kr_rot = z[:, o + LANES:o + 2 * LANES]

    qn = _rms(q_lat, g_q_ref[...]).astype(BF16)
    kvn = _rms(kv_lat, g_kv_ref[...]).astype(BF16)

    nt = (((1,), (1,)), ((), ()))
    qn_t = lax.dot_general(wqn_ref[...], qn, nt, preferred_element_type=F32)
    qr_t = lax.dot_general(wqr_ref[...], qn, nt, preferred_element_type=F32)
    cos_t = cos_t_ref[...]
    sin_t = sin_t_ref[...]
    zpad = jnp.zeros((QK_PAD - QK_HEAD_DIM, tm), BF16)
    half = QK_ROPE_DIM // 2
    for h in range(N_HEADS):
        r0 = h * QK_PAD
        qt_ref[0, 0, r0:r0 + QK_NOPE_DIM, :] = qn_t[h * QK_NOPE_DIM:(h + 1) * QK_NOPE_DIM].astype(BF16)
        qr = qr_t[h * QK_ROPE_DIM:(h + 1) * QK_ROPE_DIM]
        rot = jnp.concatenate([-qr[half:], qr[:half]], axis=0)
        roped = qr * cos_t + rot * sin_t
        qt_ref[0, 0, r0 + QK_NOPE_DIM:r0 + QK_HEAD_DIM, :] = roped.astype(BF16)
        qt_ref[0, 0, r0 + QK_HEAD_DIM:r0 + QK_PAD, :] = zpad

    k_nope = jnp.dot(kvn, wuk_ref[...], preferred_element_type=F32)
    k_rope = (kr * cos_p_ref[...] + kr_rot * sin_p_ref[...]).astype(BF16)
    for h in range(N_HEADS):
        kx_ref[0, h, :, :QK_NOPE_DIM] = k_nope[:, h * QK_NOPE_DIM:(h + 1) * QK_NOPE_DIM].astype(BF16)
        kx_ref[0, h, :, QK_NOPE_DIM:] = k_rope

    v_t = lax.dot_general(wuvt_ref[...], kvn, nt, preferred_element_type=F32)
    vt_ref[0, 0, :, :V_HEAD_DIM, :] = v_t.astype(BF16).reshape(N_HEADS, V_HEAD_DIM, tm)
    vt_ref[0, 0, :, V_HEAD_DIM:, :] = jnp.ones((N_HEADS, V_ROWS - V_HEAD_DIM, tm), BF16)


def _proj(x, w):
    B, S, _ = x.shape
    tm = ATTN_TK
    q_split = ATTN_TQ // tm
    grid = (B, S // tm)
    tok = lambda b, i: (b, i, 0)
    out_shape = [
        jax.ShapeDtypeStruct((B, S, FOURIER_WIDTH), BF16),
        jax.ShapeDtypeStruct((B, S, FOURIER_WIDTH), BF16),
        jax.ShapeDtypeStruct((B, S // ATTN_TQ, N_HEADS * QK_PAD, ATTN_TQ), BF16),
        jax.ShapeDtypeStruct((B, N_HEADS, S, QK_PAD), BF16),
        jax.ShapeDtypeStruct((B, S // tm, N_HEADS, V_ROWS, tm), BF16),
    ]
    in_specs = [
        pl.BlockSpec((1, tm, D_MODEL), tok),
        _const_spec((1, D_MODEL)),
        _const_spec(w['w_in'].shape),
        _const_spec(w['cs'].shape),
        _const_spec((1, Q_LORA_RANK)),
        _const_spec((1, KV_LORA_RANK)),
        _const_spec(w['wqn_t'].shape),
        _const_spec(w['wqr_t'].shape),
        _const_spec(w['wuk'].shape),
        _const_spec(w['wuv_t'].shape),
        pl.BlockSpec((QK_ROPE_DIM, tm), lambda b, i: (0, i)),
        pl.BlockSpec((QK_ROPE_DIM, tm), lambda b, i: (0, i)),
        pl.BlockSpec((tm, LANES), lambda b, i: (i, 0)),
        pl.BlockSpec((tm, LANES), lambda b, i: (i, 0)),
    ]
    out_specs = [
        pl.BlockSpec((1, tm, FOURIER_WIDTH), tok),
        pl.BlockSpec((1, tm, FOURIER_WIDTH), tok),
        pl.BlockSpec((1, 1, N_HEADS * QK_PAD, tm), lambda b, i: (b, i // q_split, 0, i % q_split)),
        pl.BlockSpec((1, N_HEADS, tm, QK_PAD), lambda b, i: (b, 0, i, 0)),
        pl.BlockSpec((1, 1, N_HEADS, V_ROWS, tm), lambda b, i: (b, i, 0, 0, 0)),
    ]
    return pl.pallas_call(
        _proj_kernel, grid=grid, in_specs=in_specs, out_specs=out_specs, out_shape=out_shape,
        compiler_params=pltpu.CompilerParams(
            dimension_semantics=("arbitrary", "arbitrary"), vmem_limit_bytes=VMEM_LIMIT),
        name="proj",
    )(x, w['g_attn'], w['w_in'], w['cs'], w['g_q'], w['g_kv'], w['wqn_t'], w['wqr_t'],
      w['wuk'], w['wuv_t'], w['cos_t'], w['sin_t'], w['cos_p'], w['sin_p'])


def _dft1_kernel(m1_ref, xr_ref, xi_ref, zr_ref, zi_ref):
    nb, c = xr_ref.shape[2], xr_ref.shape[3]
    xr = xr_ref[0].reshape(DFT_N1, nb * c)
    xi = xi_ref[0].reshape(DFT_N1, nb * c)
    xc = jnp.concatenate([xr, xi], axis=0)
    z = jnp.dot(m1_ref[...], xc, preferred_element_type=F32).astype(BF16)
    zr_ref[0] = z[:DFT_N1].reshape(DFT_N1, nb, c)
    zi_ref[0] = z[DFT_N1:].reshape(DFT_N1, nb, c)


def _dft2_kernel(g_ref, zr_ref, zi_ref, o_ref):
    kb, c = zr_ref.shape[1], zr_ref.shape[3]
    ys = []
    for j in range(kb):
        zc = jnp.concatenate([zr_ref[0, j], zi_ref[0, j]], axis=0)
        ys.append(jnp.dot(g_ref[j], zc, preferred_element_type=F32).astype(BF16))
    o_ref[0] = jnp.concatenate(ys, axis=-1).reshape(DFT_N2, kb, c)


def _seq_dft(xr, xi, w):
    B, S, C = xr.shape
    assert S == DFT_N1 * DFT_N2
    nb = DFT_ROWS
    xr4 = xr.reshape(B, DFT_N1, DFT_N2, C)
    xi4 = xi.reshape(B, DFT_N1, DFT_N2, C)
    rows = pl.BlockSpec((1, DFT_N1, nb, C), lambda b, j: (b, 0, j, 0))
    zr, zi = pl.pallas_call(
        _dft1_kernel, grid=(B, DFT_N2 // nb),
        in_specs=[_const_spec((2 * DFT_N1, 2 * DFT_N1)), rows, rows],
        out_specs=[rows, rows],
        out_shape=[jax.ShapeDtypeStruct((B, DFT_N1, DFT_N2, C), BF16)] * 2,
        compiler_params=pltpu.CompilerParams(
            dimension_semantics=("arbitrary", "arbitrary"), vmem_limit_bytes=VMEM_LIMIT),
        name="dft1",
    )(w['dft_m1'], xr4, xi4)
    kb = DFT_ROWS
    blk = pl.BlockSpec((1, kb, DFT_N2, C), lambda b, k: (b, k, 0, 0))
    y = pl.pallas_call(
        _dft2_kernel, grid=(B, DFT_N1 // kb),
        in_specs=[pl.BlockSpec((kb, DFT_N2, 2 * DFT_N2), lambda b, k: (k, 0, 0)), blk, blk],
        out_specs=pl.BlockSpec((1, DFT_N2, kb, C), lambda b, k: (b, 0, k, 0)),
        out_shape=jax.ShapeDtypeStruct((B, DFT_N2, DFT_N1, C), BF16),
        compiler_params=pltpu.CompilerParams(
            dimension_semantics=("arbitrary", "arbitrary"), vmem_limit_bytes=VMEM_LIMIT),
        name="dft2",
    )(w['dft_g'], zr, zi)
    return y.reshape(B, S, C)


def _attn_kernel(q_ref, k_ref, v_ref, o_ref, s_scr, acc_ref):
    tk = v_ref.shape[-1]
    nch = v_ref.shape[1]
    nq, tq = q_ref.shape[1], q_ref.shape[-1]
    groups = nch // ATTN_SUB

    def scores(f):
        j = jnp.minimum(f // nch, nq - 1)
        c = lax.rem(f, nch)
        k_c = k_ref[0, 0, pl.ds(pl.multiple_of(c * tk, tk), tk), :]
        return jnp.dot(k_c, q_ref[0, j], preferred_element_type=F32)

    def step(f, slot, m, s_max):
        s_next = scores(f + 1)
        s_scr[1 - slot] = s_next
        next_max = jnp.max(s_next, axis=0, keepdims=True)
        s = s_scr[slot]
        m_new = jnp.maximum(m, s_max)
        alpha = jnp.exp2(m - m_new)
        p = jnp.exp2(s - m_new)
        pv = jnp.dot(v_ref[0, lax.rem(f, nch), 0], p.astype(BF16), preferred_element_type=F32)
        acc_ref[...] = alpha * acc_ref[...] + pv
        return m_new, next_max

    acc_ref[...] = jnp.zeros(acc_ref.shape, F32)
    s_first = scores(0)
    s_scr[0] = s_first
    m0 = jnp.full((1, tq), -1e30, F32)

    def body(g, carry):
        m, s_max = carry
        for si in range(ATTN_SUB):
            m, s_max = step(g * ATTN_SUB + si, si % 2, m, s_max)
        block_done = lax.rem(g, groups) == groups - 1

        @pl.when(block_done)
        def _():
            j = g // groups
            o_t = acc_ref[:V_HEAD_DIM] * (1.0 / acc_ref[V_HEAD_DIM:V_HEAD_DIM + 1])
            o_ref[0, pl.ds(pl.multiple_of(j * tq, tq), tq), :] = o_t.T.astype(BF16)
            acc_ref[...] = jnp.zeros(acc_ref.shape, F32)

        return jnp.where(block_done, m0, m), s_max

    lax.fori_loop(0, nq * groups, body, (m0, jnp.max(s_first, axis=0, keepdims=True)))


def _attention(qt, kx, vt):
    B, nqb, _, tq = qt.shape
    S = nqb * tq
    nch, tk = vt.shape[1], vt.shape[-1]
    nq = ATTN_QBLOCKS
    assert ATTN_SUB % 2 == 0 and nch % ATTN_SUB == 0 and nqb % nq == 0
    return pl.pallas_call(
        _attn_kernel, grid=(B, N_HEADS, nqb // nq),
        in_specs=[pl.BlockSpec((1, nq, QK_PAD, tq), lambda b, h, i: (b, i, h, 0)),
                  pl.BlockSpec((1, 1, S, QK_PAD), lambda b, h, i: (b, h, 0, 0)),
                  pl.BlockSpec((1, nch, 1, V_ROWS, tk), lambda b, h, i: (b, 0, h, 0, 0))],
        out_specs=pl.BlockSpec((1, nq * tq, V_HEAD_DIM), lambda b, h, i: (b, i, h)),
        out_shape=jax.ShapeDtypeStruct((B, S, MLA_WIDTH), BF16),
        scratch_shapes=[pltpu.VMEM((2, tk, tq), F32), pltpu.VMEM((V_ROWS, tq), F32)],
        compiler_params=pltpu.CompilerParams(
            dimension_semantics=("arbitrary", "arbitrary", "arbitrary"), vmem_limit_bytes=VMEM_LIMIT),
        name="attn",
    )(qt, kx, vt)


def _merge_kernel(x_ref, f_ref, a_ref, g_f_ref, g_a_ref, w_out_ref, g_ffn_ref, wr_ref, b_r_ref, tri_ref, cnt0_ref,
                  h_ref, hn_ref, route_ref, w0_ref, w1_ref, cnt_ref, carry_ref):
    tm = x_ref.shape[0]

    @pl.when(pl.program_id(0) == 0)
    def _():
        carry_ref[...] = cnt0_ref[...]

    fn = _rms(f_ref[...].astype(F32), g_f_ref[...]).astype(BF16)
    an = _rms(a_ref[...].astype(F32), g_a_ref[...]).astype(BF16)
    merged = jnp.concatenate([fn, an], axis=-1)
    h = x_ref[...] + jnp.dot(merged, w_out_ref[...], preferred_element_type=F32)
    h_ref[...] = h
    hn = _rms(h, g_ffn_ref[...])
    hn_ref[...] = hn.astype(BF16).reshape(tm, ROW_TILES, LANES)
    carry = carry_ref[...]

    hn_hi = hn.astype(BF16)
    hn_lo = (hn - hn_hi.astype(F32)).astype(BF16)
    p_hi = jnp.dot(hn_hi, wr_ref[...], preferred_element_type=F32)
    p_lo = jnp.dot(hn_lo, wr_ref[...], preferred_element_type=F32)
    logits = ((p_hi[:, :LANES] + p_hi[:, LANES:]) + (p_lo[:, :LANES] + p_lo[:, LANES:])
              + b_r_ref[...])

    lane = lax.broadcasted_iota(jnp.int32, logits.shape, 1)
    neg = jnp.float32(-1e30)
    big = jnp.int32(LANES)
    is_g = lane < N_EXPERT_GROUPS
    gl = jnp.where(is_g, logits, neg)
    gmax = jnp.max(gl, axis=-1, keepdims=True)
    g_sel = jnp.min(jnp.where(gl == gmax, lane, big), axis=-1, keepdims=True)
    g_p = 1.0 / jnp.sum(jnp.where(is_g, jnp.exp(gl - gmax), 0.0), axis=-1, keepdims=True)
    e_lo = N_EXPERT_GROUPS + EXPERTS_PER_GROUP * g_sel
    in_grp = jnp.where(lane >= e_lo, jnp.where(lane < e_lo + EXPERTS_PER_GROUP, 1, 0), 0) == 1
    el = jnp.where(in_grp, logits, neg)
    m1 = jnp.max(el, axis=-1, keepdims=True)
    i1 = jnp.min(jnp.where(el == m1, lane, big), axis=-1, keepdims=True)
    el2 = jnp.where(lane == i1, neg, el)
    m2 = jnp.max(el2, axis=-1, keepdims=True)
    i2 = jnp.min(jnp.where(el2 == m2, lane, big), axis=-1, keepdims=True)
    t = jnp.exp(m2 - m1)
    inv = 1.0 / (1.0 + t)
    w0 = g_p * inv
    w1 = g_p * t * inv
    e0 = (i1 - N_EXPERT_GROUPS).astype(F32)
    e1 = (i2 - N_EXPERT_GROUPS).astype(F32)

    oh0 = jnp.where(lane == i1, 1.0, 0.0)
    oh1 = jnp.where(lane == i2, 1.0, 0.0)
    oh = oh0 + oh1
    before = jnp.dot(tri_ref[...], oh.astype(BF16), preferred_element_type=F32) + carry
    r0 = jnp.sum(before * oh0, axis=-1, keepdims=True)
    r1 = jnp.sum(before * oh1, axis=-1, keepdims=True)

    route = jnp.where(lane == 0, e0, jnp.where(lane == 1, e1, jnp.where(lane == 2, r0, jnp.where(lane == 3, r1, 0.0))))
    route_ref[...] = route.T[:ROUTE_ROWS]
    w0_ref[...] = jnp.broadcast_to(w0, logits.shape)
    w1_ref[...] = jnp.broadcast_to(w1, logits.shape)
    total = carry + jnp.sum(oh, axis=0, keepdims=True)
    carry_ref[...] = total
    cnt_ref[...] = total


def _merge(x, f_out, a_out, w, counts0):
    T = x.shape[0]
    tm = TOKEN_TILE
    n = T // tm
    cur = lambda t: (t, 0)
    out_shape = [
        jax.ShapeDtypeStruct((T, D_MODEL), F32),
        jax.ShapeDtypeStruct((T, ROW_TILES, LANES), BF16),
        jax.ShapeDtypeStruct((ROUTE_ROWS, T), F32),
        jax.ShapeDtypeStruct((T, LANES), F32),
        jax.ShapeDtypeStruct((T, LANES), F32),
        jax.ShapeDtypeStruct((1, LANES), F32),
    ]
    in_specs = [
        pl.BlockSpec((tm, D_MODEL), cur),
        pl.BlockSpec((tm, FOURIER_WIDTH), cur),
        pl.BlockSpec((tm, MLA_WIDTH), cur),
        _const_spec((1, FOURIER_WIDTH)),
        _const_spec((1, MLA_WIDTH)),
        _const_spec((D_MODEL, D_MODEL)),
        _const_spec((1, D_MODEL)),
        _const_spec((D_MODEL, 2 * LANES)),
        _const_spec((1, LANES)),
        _const_spec((tm, tm)),
        _const_spec((1, LANES)),
    ]
    out_specs = [
        pl.BlockSpec((tm, D_MODEL), cur),
        pl.BlockSpec((tm, ROW_TILES, LANES), lambda t: (t, 0, 0)),
        pl.BlockSpec((ROUTE_ROWS, tm), lambda t: (0, t)),
        pl.BlockSpec((tm, LANES), cur),
        pl.BlockSpec((tm, LANES), cur),
        pl.BlockSpec((1, LANES), lambda t: (0, 0)),
    ]
    tri = (lax.broadcasted_iota(jnp.int32, (tm, tm), 1) < lax.broadcasted_iota(jnp.int32, (tm, tm), 0)).astype(BF16)
    return pl.pallas_call(
        _merge_kernel, grid=(n,), in_specs=in_specs, out_specs=out_specs, out_shape=out_shape,
        scratch_shapes=[pltpu.VMEM((1, LANES), F32)],
        compiler_params=pltpu.CompilerParams(dimension_semantics=("arbitrary",), vmem_limit_bytes=VMEM_LIMIT),
        name="merge",
    )(x, f_out, a_out, w['g_f'], w['g_a'], w['w_out'], w['g_ffn'], w['wr'], w['b_r'], tri, counts0)


def _dispatch_kernel(blk_end_ref, nused_ref, dest_ref, hn_a_ref, hn_b_ref, xs_ref, zero_buf, sem, zsem, *, n_a):
    i = pl.program_id(0)
    tm = hn_a_ref.shape[0]
    nb = xs_ref.shape[0] // EXPERT_TILE

    @pl.when(i == 0)
    def _():
        zero_buf[...] = jnp.zeros(zero_buf.shape, BF16)

        def zero_block(j):
            return pltpu.make_async_copy(zero_buf, xs_ref.at[pl.ds(j * EXPERT_TILE, EXPERT_TILE)], zsem)

        def each_group(fn):
            for e in range(N_EXPERTS):
                end = blk_end_ref[e]
                start = blk_end_ref[e - 1] if e else 0

                @pl.when(end > start)
                def _():
                    fn(end - 1)

        def each_tail(fn):
            def body(j, carry):
                fn(j)
                return carry

            lax.fori_loop(nused_ref[0], nb, body, 0)

        each_group(lambda j: zero_block(j).start())
        each_tail(lambda j: zero_block(j).start())
        each_group(lambda j: zero_block(j).wait())
        each_tail(lambda j: zero_block(j).wait())

    def scatter(src_ref):
        def body(r, carry):
            for k in range(TOP_K):
                d = dest_ref[0, 0, k * tm + r]
                pltpu.make_async_copy(src_ref.at[r], xs_ref.at[d], sem).start(priority=k)
            return carry

        lax.fori_loop(0, tm, body, 0)
        for _ in range(TOP_K):
            pltpu.make_async_copy(src_ref, xs_ref.at[pl.ds(0, tm)], sem).wait()

    @pl.when(i < n_a)
    def _():
        scatter(hn_a_ref)

    @pl.when(i >= n_a)
    def _():
        scatter(hn_b_ref)


def _dispatch(blk_end, n_used, dest, hn_a, hn_b, n_rows):
    tm = TOKEN_TILE
    n_a, n_b = hn_a.shape[0] // tm, hn_b.shape[0] // tm
    tile = (tm, ROW_TILES, LANES)
    grid_spec = pltpu.PrefetchScalarGridSpec(
        num_scalar_prefetch=2, grid=(n_a + n_b,),
        in_specs=[pl.BlockSpec((1, 1, TOP_K * tm), lambda i, be, nu: (i, 0, 0), memory_space=pltpu.SMEM),
                  pl.BlockSpec(tile, lambda i, be, nu: (jnp.minimum(i, n_a - 1), 0, 0)),
                  pl.BlockSpec(tile, lambda i, be, nu: (jnp.maximum(i - n_a, 0), 0, 0))],
        out_specs=pl.BlockSpec(memory_space=pl.ANY),
        scratch_shapes=[pltpu.VMEM((EXPERT_TILE, ROW_TILES, LANES), BF16),
                        pltpu.SemaphoreType.DMA(()), pltpu.SemaphoreType.DMA(())],
    )
    return pl.pallas_call(
        functools.partial(_dispatch_kernel, n_a=n_a), grid_spec=grid_spec,
        out_shape=jax.ShapeDtypeStruct((n_rows, ROW_TILES, LANES), BF16),
        compiler_params=pltpu.CompilerParams(dimension_semantics=("arbitrary",), vmem_limit_bytes=VMEM_LIMIT),
        name="dispatch",
    )(blk_end, n_used, dest, hn_a, hn_b)


def _expert_kernel(be_ref, nused_ref, blk_end_ref, xs_ref, wg_ref, wu_ref, wd_ref, y_ref,
                   wg_f, wu_f, wd_f, wgu_s, wd_s, group_ref, sems):
    i = pl.program_id(0)
    e = be_ref[i]

    def weight_copies(expert, slot):
        return [pltpu.make_async_copy(src.at[expert], dst.at[slot], sems.at[slot])
                for src, dst in ((wg_ref, wg_f), (wu_ref, wu_f), (wd_ref, wd_f))]

    @pl.when(i == 0)
    def _():
        group_ref[0] = 0
        for cp in weight_copies(e, 0):
            cp.start()

    @pl.when((i == 0) | (e != be_ref[jnp.maximum(i - 1, 0)]))
    def _():
        slot = lax.rem(group_ref[0], 2)
        for cp in weight_copies(e, slot):
            cp.wait()
        wgu_s[:, :EXPERT_DIM] = wg_f[slot].astype(BF16)
        wgu_s[:, EXPERT_DIM:] = wu_f[slot].astype(BF16)
        wd_s[...] = wd_f[slot].astype(BF16)
        nxt = blk_end_ref[e]

        @pl.when(nxt < nused_ref[0])
        def _():
            for cp in weight_copies(be_ref[nxt], 1 - slot):
                cp.start()

        group_ref[0] = group_ref[0] + 1

    @pl.when(i < nused_ref[0])
    def _():
        tm = xs_ref.shape[0]
        x = xs_ref[...].reshape(tm, D_MODEL)
        gu = jnp.dot(x, wgu_s[...], preferred_element_type=F32)
        g = gu[:, :EXPERT_DIM]
        u = gu[:, EXPERT_DIM:]
        hmid = (g * jax.nn.sigmoid(g) * u).astype(BF16)
        y = jnp.dot(hmid, wd_s[...], preferred_element_type=F32)
        y_ref[...] = y.astype(BF16).reshape(tm, ROW_TILES, LANES)

    @pl.when(i >= nused_ref[0])
    def _():
        y_ref[...] = jnp.zeros(y_ref.shape, BF16)


def _experts(block_expert, n_used, blk_end, xs, w):
    n_rows = xs.shape[0]
    tm = EXPERT_TILE
    nb = n_rows // tm
    row = lambda i, be, nu, bend: (jnp.minimum(i, nu[0] - 1), 0, 0)
    out_row = lambda i, be, nu, bend: (i, 0, 0)
    hbm = pl.BlockSpec(memory_space=pl.ANY)
    grid_spec = pltpu.PrefetchScalarGridSpec(
        num_scalar_prefetch=3, grid=(nb,),
        in_specs=[pl.BlockSpec((tm, ROW_TILES, LANES), row), hbm, hbm, hbm],
        out_specs=pl.BlockSpec((tm, ROW_TILES, LANES), out_row),
        scratch_shapes=[pltpu.VMEM((2, D_MODEL, EXPERT_DIM), F32), pltpu.VMEM((2, D_MODEL, EXPERT_DIM), F32),
                        pltpu.VMEM((2, EXPERT_DIM, D_MODEL), F32),
                        pltpu.VMEM((D_MODEL, 2 * EXPERT_DIM), BF16), pltpu.VMEM((EXPERT_DIM, D_MODEL), BF16),
                        pltpu.SMEM((1,), jnp.int32), pltpu.SemaphoreType.DMA((2,))],
    )
    return pl.pallas_call(
        _expert_kernel, grid_spec=grid_spec,
        out_shape=jax.ShapeDtypeStruct((n_rows, ROW_TILES, LANES), BF16),
        compiler_params=pltpu.CompilerParams(dimension_semantics=("arbitrary",), vmem_limit_bytes=VMEM_LIMIT),
        name="experts",
    )(block_expert, n_used, blk_end, xs, w['w_gate'], w['w_up'], w['w_down'])


def _combine_kernel(dest_ref, dest_next_ref, h_ref, w0_ref, w1_ref, g_ref, y_ref, o_ref, ybuf, sems, *, n_steps):
    tm = h_ref.shape[0]
    i = pl.program_id(0)
    slot = lax.rem(i, 2)

    def start_gather(d_ref, s):
        def body(r, carry):
            for k in range(TOP_K):
                d = d_ref[0, 0, k * tm + r]
                pltpu.make_async_copy(y_ref.at[d], ybuf.at[s, k, r], sems.at[s]).start(priority=k)
            return carry

        lax.fori_loop(0, tm, body, 0)

    @pl.when(i == 0)
    def _():
        start_gather(dest_ref, 0)

    @pl.when(i + 1 < n_steps)
    def _():
        start_gather(dest_next_ref, 1 - slot)

    for k in range(TOP_K):
        pltpu.make_async_copy(y_ref.at[pl.ds(0, tm)], ybuf.at[slot, k], sems.at[slot]).wait()

    y0 = ybuf[slot, 0].reshape(tm, D_MODEL)
    y1 = ybuf[slot, 1].reshape(tm, D_MODEL)
    w0 = w0_ref[...]
    w1 = w1_ref[...]
    cols = []
    for j in range(ROW_TILES):
        sl = slice(j * LANES, (j + 1) * LANES)
        cols.append(h_ref[:, sl] + w0 * y0[:, sl].astype(F32) + w1 * y1[:, sl].astype(F32))
    hs = jnp.concatenate(cols, axis=-1)
    o_ref[...] = _rms(hs, g_ref[...])


def _combine(dest, h, w0, w1, g_final, y):
    T = h.shape[0]
    tm = TOKEN_TILE
    n = T // tm
    dest3 = dest
    return pl.pallas_call(
        functools.partial(_combine_kernel, n_steps=n), grid=(n,),
        in_specs=[pl.BlockSpec((1, 1, TOP_K * tm), lambda i: (i, 0, 0), memory_space=pltpu.SMEM),
                  pl.BlockSpec((1, 1, TOP_K * tm), lambda i: (jnp.minimum(i + 1, n - 1), 0, 0),
                               memory_space=pltpu.SMEM),
                  pl.BlockSpec((tm, D_MODEL), lambda i: (i, 0)),
                  pl.BlockSpec((tm, LANES), lambda i: (i, 0)),
                  pl.BlockSpec((tm, LANES), lambda i: (i, 0)),
                  _const_spec((1, D_MODEL)),
                  pl.BlockSpec(memory_space=pl.ANY)],
        out_specs=pl.BlockSpec((tm, D_MODEL), lambda i: (i, 0)),
        out_shape=jax.ShapeDtypeStruct((T, D_MODEL), F32),
        scratch_shapes=[pltpu.VMEM((2, TOP_K, tm, ROW_TILES, LANES), BF16), pltpu.SemaphoreType.DMA((2,))],
        compiler_params=pltpu.CompilerParams(dimension_semantics=("arbitrary",), vmem_limit_bytes=VMEM_LIMIT),
        name="combine",
    )(dest3, dest3, h, w0, w1, g_final, y)


def _tables(S):
    pos = jnp.arange(S, dtype=F32)
    inv_freq = 1.0 / (ROPE_THETA ** (jnp.arange(0, QK_ROPE_DIM, 2, dtype=F32) / QK_ROPE_DIM))
    ang = pos[:, None] * inv_freq[None, :]
    cos, sin = jnp.cos(ang), jnp.sin(ang)
    cos2 = jnp.concatenate([cos, cos], axis=-1)
    sin2 = jnp.concatenate([sin, sin], axis=-1)
    zp = jnp.zeros((S, LANES - QK_ROPE_DIM), F32)
    t = {
        'cos_t': cos2.T, 'sin_t': sin2.T,
        'cos_p': jnp.concatenate([cos2, zp], axis=-1), 'sin_p': jnp.concatenate([sin2, zp], axis=-1),
    }

    def phase(rows, cols, n):
        m = (rows[:, None] * cols[None, :]) % n
        a = m.astype(F32) * jnp.float32(2.0 * math.pi / n)
        return jnp.cos(a), jnp.sin(a)

    i1 = jnp.arange(DFT_N1, dtype=jnp.int32)
    c1, s1 = phase(i1, i1, DFT_N1)
    t['dft_m1'] = jnp.concatenate(
        [jnp.concatenate([c1, s1], axis=1), jnp.concatenate([-s1, c1], axis=1)], axis=0).astype(BF16)
    k = jnp.arange(S, dtype=jnp.int32)
    n2 = jnp.arange(DFT_N2, dtype=jnp.int32)
    cg, sg = phase(k, n2, S)
    g = jnp.concatenate([cg, sg], axis=1) * jnp.float32(S ** -0.5)
    t['dft_g'] = g.reshape(DFT_N2, DFT_N1, 2 * DFT_N2).transpose(1, 0, 2).astype(BF16)
    return t


def _layer_weights(g_attn_norm, w_in, g_q_latent, w_uq, g_kv_latent, w_ukv, g_out_fourier, g_out_mla, w_out,
                   g_ffn_norm, w_router_group, b_router_group, w_router_expert, b_router_expert,
                   w_gate, w_up, w_down):
    half = QK_ROPE_DIM // 2
    s3 = FOURIER_WIDTH + Q_LORA_RANK + KV_LORA_RANK
    w_kr = w_in[:, s3:s3 + QK_ROPE_DIM]
    w_kr_rot = jnp.concatenate([-w_kr[:, half:], w_kr[:, :half]], axis=1)
    zc = jnp.zeros((D_MODEL, LANES - QK_ROPE_DIM), F32)
    w_in_ext = jnp.concatenate([w_in[:, :s3], w_kr, zc, w_kr_rot, zc], axis=1)

    c = jnp.arange(FOURIER_GROUP_DIM, dtype=jnp.int32)
    m = (c[:, None] * c[None, :]) % FOURIER_GROUP_DIM
    a = m.astype(F32) * jnp.float32(2.0 * math.pi / FOURIER_GROUP_DIM)
    cs = jnp.concatenate([jnp.cos(a), -jnp.sin(a)], axis=1) * jnp.float32(FOURIER_GROUP_DIM ** -0.5)

    qscale = jnp.float32(QK_HEAD_DIM ** -0.5 * math.log2(math.e))
    wq = (w_uq * qscale).reshape(Q_LORA_RANK, N_HEADS, QK_HEAD_DIM)
    wq_n = wq[:, :, :QK_NOPE_DIM]
    wq_r = wq[:, :, QK_NOPE_DIM:]
    to_t = lambda t: t.reshape(Q_LORA_RANK, -1).T
    wkv = w_ukv.reshape(KV_LORA_RANK, N_HEADS, QK_NOPE_DIM + V_HEAD_DIM)
    w_r = jnp.concatenate(
        [w_router_group, w_router_expert,
         jnp.zeros((D_MODEL, LANES - N_EXPERT_GROUPS - N_EXPERTS), F32)], axis=1)
    wr_hi = w_r.astype(BF16)
    b_r = jnp.concatenate([b_router_group, b_router_expert,
                           jnp.zeros((LANES - N_EXPERT_GROUPS - N_EXPERTS,), F32)])
    return {
        'g_attn': g_attn_norm[None, :], 'w_in': w_in_ext.astype(BF16), 'cs': cs.astype(BF16),
        'g_q': g_q_latent[None, :], 'g_kv': g_kv_latent[None, :],
        'wqn_t': to_t(wq_n).astype(BF16), 'wqr_t': to_t(wq_r).astype(BF16),
        'wuk': wkv[:, :, :QK_NOPE_DIM].reshape(KV_LORA_RANK, -1).astype(BF16),
        'wuv_t': wkv[:, :, QK_NOPE_DIM:].reshape(KV_LORA_RANK, -1).T.astype(BF16),
        'g_f': g_out_fourier[None, :], 'g_a': g_out_mla[None, :], 'w_out': w_out.astype(BF16),
        'g_ffn': g_ffn_norm[None, :],
        'wr': jnp.concatenate([wr_hi, (w_r - wr_hi.astype(F32)).astype(BF16)], axis=1),
        'b_r': b_r[None, :],
        'w_gate': w_gate, 'w_up': w_up, 'w_down': w_down,
    }


def _block_plan(counts, n_assign):
    counts = counts[0, N_EXPERT_GROUPS:N_EXPERT_GROUPS + N_EXPERTS].astype(jnp.int32)
    nblk = (counts + EXPERT_TILE - 1) // EXPERT_TILE
    blk_end = jnp.cumsum(nblk)
    blk_start = blk_end - nblk
    n_blocks = (n_assign + N_EXPERTS * (EXPERT_TILE - 1) + EXPERT_TILE - 1) // EXPERT_TILE
    n_used = blk_end[-1]
    j = jnp.minimum(jnp.arange(n_blocks, dtype=jnp.int32), n_used - 1)
    block_expert = jnp.minimum(jnp.sum((blk_end[None, :] <= j[:, None]).astype(jnp.int32), axis=1), N_EXPERTS - 1)
    return blk_start, blk_end, block_expert, n_used[None], n_blocks * EXPERT_TILE


def _dest_rows(route, blk_start):
    T = route.shape[1]
    e = route[:TOP_K].astype(jnp.int32)
    rank = route[TOP_K:2 * TOP_K].astype(jnp.int32)
    first = jnp.zeros_like(e)
    for j in range(N_EXPERTS):
        first = jnp.where(e == j, blk_start[j], first)
    dest = (first * EXPERT_TILE + rank).reshape(TOP_K, T // TOKEN_TILE, TOKEN_TILE)
    return dest.transpose(1, 0, 2).reshape(T // TOKEN_TILE, 1, TOP_K * TOKEN_TILE)


def _mixers(x, wt, counts0):
    B, S, _ = x.shape
    T = B * S
    xr, xi, qt, kx, vt = _proj(x, wt)
    f_out = _seq_dft(xr, xi, wt)
    a_out = _attention(qt, kx, vt)
    flat = lambda t: t.reshape((T,) + t.shape[2:])
    return _merge(flat(x), flat(f_out), flat(a_out), wt, counts0)


def kernel(x_prompt, x_sample, g_attn_norm, w_in, g_q_latent, w_uq, g_kv_latent, w_ukv, g_out_fourier, g_out_mla, w_out, g_ffn_norm, w_router_group, b_router_group, w_router_expert, b_router_expert, w_gate, w_up, w_down, g_final):
    assert g_attn_norm.shape[0] == 1, "single-layer configuration"
    w = _layer_weights(g_attn_norm[0], w_in[0], g_q_latent[0], w_uq[0], g_kv_latent[0], w_ukv[0],
                       g_out_fourier[0], g_out_mla[0], w_out[0], g_ffn_norm[0], w_router_group[0],
                       b_router_group[0], w_router_expert[0], b_router_expert[0], w_gate[0], w_up[0], w_down[0])
    assert x_prompt.shape[1] == x_sample.shape[1]
    w.update(_tables(x_prompt.shape[1]))

    counts = jnp.zeros((1, LANES), F32)
    batches = []
    for x in (x_prompt, x_sample):
        h, hn3, route, w0, w1, counts = _mixers(x, w, counts)
        batches.append((x.shape, h, hn3, route, w0, w1))

    n_assign = sum(b[1].shape[0] for b in batches) * TOP_K
    blk_start, blk_end, block_expert, n_used, n_rows = _block_plan(counts, n_assign)
    dests = [_dest_rows(b[3], blk_start) for b in batches]
    xs = _dispatch(blk_end, n_used, jnp.concatenate(dests, axis=0), batches[0][2], batches[1][2], n_rows)
    y = _experts(block_expert, n_used, blk_end, xs, w)
    outs = []
    for (shape, h, _, _, w0, w1), dest in zip(batches, dests):
        outs.append(_combine(dest, h, w0, w1, g_final[None, :], y).reshape(shape))
    return tuple(outs)
```

```python
import functools
import math

import jax
import jax.numpy as jnp
from jax import lax
from jax.experimental import pallas as pl
from jax.experimental.pallas import tpu as pltpu

D_MODEL = 2048
FOURIER_WIDTH = 1024
N_FOURIER_GROUPS = 4
FOURIER_GROUP_DIM = 256
MLA_WIDTH = 1024
V_HEAD_DIM = 128
N_HEADS = 8
QK_NOPE_DIM = 128
QK_ROPE_DIM = 64
QK_HEAD_DIM = 192
Q_LORA_RANK = 512
KV_LORA_RANK = 256
ROPE_THETA = 10000.0
N_EXPERT_GROUPS = 4
EXPERTS_PER_GROUP = 8
N_EXPERTS = 32
TOP_K = 2
EXPERT_DIM = 512
EPS = 1e-6

LANES = 128
QK_PAD = 256
V_ROWS = V_HEAD_DIM + 16
ROW_TILES = D_MODEL // LANES
ROUTE_ROWS = 8

DFT_N1 = 64
DFT_N2 = 128

TOKEN_TILE = 512
ATTN_TQ = 1024
ATTN_TK = 512
ATTN_SUB = 8
ATTN_QBLOCKS = 4
DFT_ROWS = 16
EXPERT_TILE = 256
VMEM_LIMIT = 56 * 1024 * 1024

F32 = jnp.float32
BF16 = jnp.bfloat16


def _const_spec(shape):
    nd = len(shape)
    return pl.BlockSpec(shape, lambda *_: (0,) * nd, pipeline_mode=pl.Buffered(1))


def _rms(x, g):
    return x * lax.rsqrt(jnp.mean(x * x, axis=-1, keepdims=True) + EPS) * g


def _proj_kernel(x_ref, g_attn_ref, w_in_ref, cs_ref, g_q_ref, g_kv_ref, wqn_ref, wqr_ref,
                 wuk_ref, wuvt_ref, cos_t_ref, sin_t_ref, cos_p_ref, sin_p_ref,
                 xr_ref, xi_ref, qt_ref, kx_ref, vt_ref):
    tm = x_ref.shape[1]
    x = x_ref[0]
    u = _rms(x, g_attn_ref[...]).astype(BF16)
    z = jnp.dot(u, w_in_ref[...], preferred_element_type=F32)

    zf = z[:, :FOURIER_WIDTH].astype(BF16)
    xr, xi = [], []
    for g in range(N_FOURIER_GROUPS):
        zg = zf[:, g * FOURIER_GROUP_DIM:(g + 1) * FOURIER_GROUP_DIM]
        xg = jnp.dot(zg, cs_ref[...], preferred_element_type=F32)
        xr.append(xg[:, :FOURIER_GROUP_DIM])
        xi.append(xg[:, FOURIER_GROUP_DIM:])
    xr_ref[0] = jnp.concatenate(xr, axis=-1).astype(BF16)
    xi_ref[0] = jnp.concatenate(xi, axis=-1).astype(BF16)

    o = FOURIER_WIDTH
    q_lat = z[:, o:o + Q_LORA_RANK]
    o += Q_LORA_RANK
    kv_lat = z[:, o:o + KV_LORA_RANK]
    o += KV_LORA_RANK
    kr = z[:, o:o + LANES]
    kr_rot = z[:, o + LANES:o + 2 * LANES]

    qn = _rms(q_lat, g_q_ref[...]).astype(BF16)
    kvn = _rms(kv_lat, g_kv_ref[...]).astype(BF16)

    nt = (((1,), (1,)), ((), ()))
    qn_t = lax.dot_general(wqn_ref[...], qn, nt, preferred_element_type=F32)
    qr_t = lax.dot_general(wqr_ref[...], qn, nt, preferred_element_type=F32)
    cos_t = cos_t_ref[...]
    sin_t = sin_t_ref[...]
    zpad = jnp.zeros((QK_PAD - QK_HEAD_DIM, tm), BF16)
    half = QK_ROPE_DIM // 2
    for h in range(N_HEADS):
        r0 = h * QK_PAD
        qt_ref[0, 0, r0:r0 + QK_NOPE_DIM, :] = qn_t[h * QK_NOPE_DIM:(h + 1) * QK_NOPE_DIM].astype(BF16)
        qr = qr_t[h * QK_ROPE_DIM:(h + 1) * QK_ROPE_DIM]
        rot = jnp.concatenate([-qr[half:], qr[:half]], axis=0)
        roped = qr * cos_t + rot * sin_t
        qt_ref[0, 0, r0 + QK_NOPE_DIM:r0 + QK_HEAD_DIM, :] = roped.astype(BF16)
        qt_ref[0, 0, r0 + QK_HEAD_DIM:r0 + QK_PAD, :] = zpad

    k_nope = jnp.dot(kvn, wuk_ref[...], preferred_element_type=F32)
    k_rope = (kr * cos_p_ref[...] + kr_rot * sin_p_ref[...]).astype(BF16)
    for h in range(N_HEADS):
        kx_ref[0, h, :, :QK_NOPE_DIM] = k_nope[:, h * QK_NOPE_DIM:(h + 1) * QK_NOPE_DIM].astype(BF16)
        kx_ref[0, h, :, QK_NOPE_DIM:] = k_rope

    v_t = lax.dot_general(wuvt_ref[...], kvn, nt, preferred_element_type=F32)
    vt_ref[0, 0, :, :V_HEAD_DIM, :] = v_t.astype(BF16).reshape(N_HEADS, V_HEAD_DIM, tm)
    vt_ref[0, 0, :, V_HEAD_DIM:, :] = jnp.ones((N_HEADS, V_ROWS - V_HEAD_DIM, tm), BF16)


def _proj(x, w):
    B, S, _ = x.shape
    tm = ATTN_TK
    q_split = ATTN_TQ // tm
    grid = (B, S // tm)
    tok = lambda b, i: (b, i, 0)
    out_shape = [
        jax.ShapeDtypeStruct((B, S, FOURIER_WIDTH), BF16),
        jax.ShapeDtypeStruct((B, S, FOURIER_WIDTH), BF16),
        jax.ShapeDtypeStruct((B, S // ATTN_TQ, N_HEADS * QK_PAD, ATTN_TQ), BF16),
        jax.ShapeDtypeStruct((B, N_HEADS, S, QK_PAD), BF16),
        jax.ShapeDtypeStruct((B, S // tm, N_HEADS, V_ROWS, tm), BF16),
    ]
    in_specs = [
        pl.BlockSpec((1, tm, D_MODEL), tok),
        _const_spec((1, D_MODEL)),
        _const_spec(w['w_in'].shape),
        _const_spec(w['cs'].shape),
        _const_spec((1, Q_LORA_RANK)),
        _const_spec((1, KV_LORA_RANK)),
        _const_spec(w['wqn_t'].shape),
        _const_spec(w['wqr_t'].shape),
        _const_spec(w['wuk'].shape),
        _const_spec(w['wuv_t'].shape),
        pl.BlockSpec((QK_ROPE_DIM, tm), lambda b, i: (0, i)),
        pl.BlockSpec((QK_ROPE_DIM, tm), lambda b, i: (0, i)),
        pl.BlockSpec((tm, LANES), lambda b, i: (i, 0)),
        pl.BlockSpec((tm, LANES), lambda b, i: (i, 0)),
    ]
    out_specs = [
        pl.BlockSpec((1, tm, FOURIER_WIDTH), tok),
        pl.BlockSpec((1, tm, FOURIER_WIDTH), tok),
        pl.BlockSpec((1, 1, N_HEADS * QK_PAD, tm), lambda b, i: (b, i // q_split, 0, i % q_split)),
        pl.BlockSpec((1, N_HEADS, tm, QK_PAD), lambda b, i: (b, 0, i, 0)),
        pl.BlockSpec((1, 1, N_HEADS, V_ROWS, tm), lambda b, i: (b, i, 0, 0, 0)),
    ]
    return pl.pallas_call(
        _proj_kernel, grid=grid, in_specs=in_specs, out_specs=out_specs, out_shape=out_shape,
        compiler_params=pltpu.CompilerParams(
            dimension_semantics=("arbitrary", "arbitrary"), vmem_limit_bytes=VMEM_LIMIT),
        name="proj",
    )(x, w['g_attn'], w['w_in'], w['cs'], w['g_q'], w['g_kv'], w['wqn_t'], w['wqr_t'],
      w['wuk'], w['wuv_t'], w['cos_t'], w['sin_t'], w['cos_p'], w['sin_p'])


def _dft1_kernel(m1_ref, xr_ref, xi_ref, zr_ref, zi_ref):
    nb, c = xr_ref.shape[2], xr_ref.shape[3]
    xr = xr_ref[0].reshape(DFT_N1, nb * c)
    xi = xi_ref[0].reshape(DFT_N1, nb * c)
    xc = jnp.concatenate([xr, xi], axis=0)
    z = jnp.dot(m1_ref[...], xc, preferred_element_type=F32).astype(BF16)
    zr_ref[0] = z[:DFT_N1].reshape(DFT_N1, nb, c)
    zi_ref[0] = z[DFT_N1:].reshape(DFT_N1, nb, c)


def _dft2_kernel(g_ref, zr_ref, zi_ref, o_ref):
    kb, c = zr_ref.shape[1], zr_ref.shape[3]
    ys = []
    for j in range(kb):
        zc = jnp.concatenate([zr_ref[0, j], zi_ref[0, j]], axis=0)
        ys.append(jnp.dot(g_ref[j], zc, preferred_element_type=F32).astype(BF16))
    o_ref[0] = jnp.concatenate(ys, axis=-1).reshape(DFT_N2, kb, c)


def _seq_dft(xr, xi, w):
    B, S, C = xr.shape
    assert S == DFT_N1 * DFT_N2
    nb = DFT_ROWS
    xr4 = xr.reshape(B, DFT_N1, DFT_N2, C)
    xi4 = xi.reshape(B, DFT_N1, DFT_N2, C)
    rows = pl.BlockSpec((1, DFT_N1, nb, C), lambda b, j: (b, 0, j, 0))
    zr, zi = pl.pallas_call(
        _dft1_kernel, grid=(B, DFT_N2 // nb),
        in_specs=[_const_spec((2 * DFT_N1, 2 * DFT_N1)), rows, rows],
        out_specs=[rows, rows],
        out_shape=[jax.ShapeDtypeStruct((B, DFT_N1, DFT_N2, C), BF16)] * 2,
        compiler_params=pltpu.CompilerParams(
            dimension_semantics=("arbitrary", "arbitrary"), vmem_limit_bytes=VMEM_LIMIT),
        name="dft1",
    )(w['dft_m1'], xr4, xi4)
    kb = DFT_ROWS
    blk = pl.BlockSpec((1, kb, DFT_N2, C), lambda b, k: (b, k, 0, 0))
    y = pl.pallas_call(
        _dft2_kernel, grid=(B, DFT_N1 // kb),
        in_specs=[pl.BlockSpec((kb, DFT_N2, 2 * DFT_N2), lambda b, k: (k, 0, 0)), blk, blk],
        out_specs=pl.BlockSpec((1, DFT_N2, kb, C), lambda b, k: (b, 0, k, 0)),
        out_shape=jax.ShapeDtypeStruct((B, DFT_N2, DFT_N1, C), BF16),
        compiler_params=pltpu.CompilerParams(
            dimension_semantics=("arbitrary", "arbitrary"), vmem_limit_bytes=VMEM_LIMIT),
        name="dft2",
    )(w['dft_g'], zr, zi)
    return y.reshape(B, S, C)


def _attn_kernel(q_ref, k_ref, v_ref, o_ref, s_scr, acc_ref):
    tk = v_ref.shape[-1]
    nch = v_ref.shape[1]
    nq, tq = q_ref.shape[1], q_ref.shape[-1]
    groups = nch // ATTN_SUB

    def scores(f):
        j = jnp.minimum(f // nch, nq - 1)
        c = lax.rem(f, nch)
        k_c = k_ref[0, 0, pl.ds(pl.multiple_of(c * tk, tk), tk), :]
        return jnp.dot(k_c, q_ref[0, j], preferred_element_type=F32)

    def step(f, slot, m, s_max):
        s_next = scores(f + 1)
        s_scr[1 - slot] = s_next
        next_max = jnp.max(s_next, axis=0, keepdims=True)
        s = s_scr[slot]
        m_new = jnp.maximum(m, s_max)
        alpha = jnp.exp2(m - m_new)
        p = jnp.exp2(s - m_new)
        pv = jnp.dot(v_ref[0, lax.rem(f, nch), 0], p.astype(BF16), preferred_element_type=F32)
        acc_ref[...] = alpha * acc_ref[...] + pv
        return m_new, next_max

    acc_ref[...] = jnp.zeros(acc_ref.shape, F32)
    s_first = scores(0)
    s_scr[0] = s_first
    m0 = jnp.full((1, tq), -1e30, F32)

    def body(g, carry):
        m, s_max = carry
        for si in range(ATTN_SUB):
            m, s_max = step(g * ATTN_SUB + si, si % 2, m, s_max)
        block_done = lax.rem(g, groups) == groups - 1

        @pl.when(block_done)
        def _():
            j = g // groups
            o_t = acc_ref[:V_HEAD_DIM] * (1.0 / acc_ref[V_HEAD_DIM:V_HEAD_DIM + 1])
            o_ref[0, pl.ds(pl.multiple_of(j * tq, tq), tq), :] = o_t.T.astype(BF16)
            acc_ref[...] = jnp.zeros(acc_ref.shape, F32)

        return jnp.where(block_done, m0, m), s_max

    lax.fori_loop(0, nq * groups, body, (m0, jnp.max(s_first, axis=0, keepdims=True)))


def _attention(qt, kx, vt):
    B, nqb, _, tq = qt.shape
    S = nqb * tq
    nch, tk = vt.shape[1], vt.shape[-1]
    nq = ATTN_QBLOCKS
    assert ATTN_SUB % 2 == 0 and nch % ATTN_SUB == 0 and nqb % nq == 0
    return pl.pallas_call(
        _attn_kernel, grid=(B, N_HEADS, nqb // nq),
        in_specs=[pl.BlockSpec((1, nq, QK_PAD, tq), lambda b, h, i: (b, i, h, 0)),
                  pl.BlockSpec((1, 1, S, QK_PAD), lambda b, h, i: (b, h, 0, 0)),
                  pl.BlockSpec((1, nch, 1, V_ROWS, tk), lambda b, h, i: (b, 0, h, 0, 0))],
        out_specs=pl.BlockSpec((1, nq * tq, V_HEAD_DIM), lambda b, h, i: (b, i, h)),
        out_shape=jax.ShapeDtypeStruct((B, S, MLA_WIDTH), BF16),
        scratch_shapes=[pltpu.VMEM((2, tk, tq), F32), pltpu.VMEM((V_ROWS, tq), F32)],
        compiler_params=pltpu.CompilerParams(
            dimension_semantics=("arbitrary", "arbitrary", "arbitrary"), vmem_limit_bytes=VMEM_LIMIT),
        name="attn",
    )(qt, kx, vt)


def _merge_kernel(x_ref, f_ref, a_ref, g_f_ref, g_a_ref, w_out_ref, g_ffn_ref, wr_ref, b_r_ref, tri_ref, cnt0_ref,
                  h_ref, hn_ref, route_ref, w0_ref, w1_ref, cnt_ref, carry_ref):
    tm = x_ref.shape[0]

    @pl.when(pl.program_id(0) == 0)
    def _():
        carry_ref[...] = cnt0_ref[...]

    fn = _rms(f_ref[...].astype(F32), g_f_ref[...]).astype(BF16)
    an = _rms(a_ref[...].astype(F32), g_a_ref[...]).astype(BF16)
    merged = jnp.concatenate([fn, an], axis=-1)
    h = x_ref[...] + jnp.dot(merged, w_out_ref[...], preferred_element_type=F32)
    h_ref[...] = h
    hn = _rms(h, g_ffn_ref[...])
    hn_ref[...] = hn.astype(BF16).reshape(tm, ROW_TILES, LANES)

    hn_hi = hn.astype(BF16)
    hn_lo = (hn - hn_hi.astype(F32)).astype(BF16)
    nt = (((1,), (1,)), ((), ()))
    p_t = (lax.dot_general(wr_ref[...], hn_hi, nt, preferred_element_type=F32)
           + lax.dot_general(wr_ref[...], hn_lo, nt, preferred_element_type=F32))
    reps = tm // LANES
    logits = p_t[:LANES] + p_t[LANES:] + jnp.tile(b_r_ref[...], (1, reps))

    row = lax.broadcasted_iota(jnp.int32, (EXPERTS_PER_GROUP, tm), 0)
    neg = jnp.float32(-1e30)
    gl = logits[:EXPERTS_PER_GROUP]
    gmax = jnp.max(gl, axis=0, keepdims=True)
    g_sel = jnp.min(jnp.where(gl == gmax, row, EXPERTS_PER_GROUP), axis=0, keepdims=True)
    g_p = 1.0 / jnp.sum(jnp.exp(gl - gmax), axis=0, keepdims=True)
    el = logits[EXPERTS_PER_GROUP:2 * EXPERTS_PER_GROUP]
    for g in range(1, N_EXPERT_GROUPS):
        el = jnp.where(g_sel == g, logits[(g + 1) * EXPERTS_PER_GROUP:(g + 2) * EXPERTS_PER_GROUP], el)
    m1 = jnp.max(el, axis=0, keepdims=True)
    i1 = jnp.min(jnp.where(el == m1, row, EXPERTS_PER_GROUP), axis=0, keepdims=True)
    el2 = jnp.where(row == i1, neg, el)
    m2 = jnp.max(el2, axis=0, keepdims=True)
    i2 = jnp.min(jnp.where(el2 == m2, row, EXPERTS_PER_GROUP), axis=0, keepdims=True)
    t = jnp.exp(m2 - m1)
    inv = 1.0 / (1.0 + t)
    w0 = g_p * inv
    w1 = g_p * t * inv
    e0 = g_sel * EXPERTS_PER_GROUP + i1
    e1 = g_sel * EXPERTS_PER_GROUP + i2

    row_e = lax.broadcasted_iota(jnp.int32, (N_EXPERTS, tm), 0)
    oh0 = jnp.where(row_e == e0, 1.0, 0.0)
    oh1 = jnp.where(row_e == e1, 1.0, 0.0)
    oh = oh0 + oh1
    before = (jnp.dot(oh.astype(BF16), tri_ref[...], preferred_element_type=F32)
              + jnp.tile(carry_ref[...], (1, reps)))
    r0 = jnp.sum(before * oh0, axis=0, keepdims=True)
    r1 = jnp.sum(before * oh1, axis=0, keepdims=True)
    total = carry_ref[...] + jnp.broadcast_to(jnp.sum(oh, axis=1, keepdims=True), carry_ref.shape)
    carry_ref[...] = total
    cnt_ref[...] = total

    row_r = lax.broadcasted_iota(jnp.int32, (ROUTE_ROWS, tm), 0)
    route_ref[...] = jnp.where(row_r == 0, e0.astype(F32), jnp.where(row_r == 1, e1.astype(F32),
                               jnp.where(row_r == 2, r0, jnp.where(row_r == 3, r1, 0.0))))
    row_l = lax.broadcasted_iota(jnp.int32, (LANES, tm), 0)
    w_rows = jnp.where(row_l == 0, w0, jnp.where(row_l == 1, w1, 0.0)).T
    w0_ref[...] = jnp.broadcast_to(w_rows[:, 0:1], (tm, LANES))
    w1_ref[...] = jnp.broadcast_to(w_rows[:, 1:2], (tm, LANES))


def _merge(x, f_out, a_out, w, counts0):
    T = x.shape[0]
    tm = TOKEN_TILE
    n = T // tm
    cur = lambda t: (t, 0)
    out_shape = [
        jax.ShapeDtypeStruct((T, D_MODEL), F32),
        jax.ShapeDtypeStruct((T, ROW_TILES, LANES), BF16),
        jax.ShapeDtypeStruct((ROUTE_ROWS, T), F32),
        jax.ShapeDtypeStruct((T, LANES), F32),
        jax.ShapeDtypeStruct((T, LANES), F32),
        jax.ShapeDtypeStruct((N_EXPERTS, LANES), F32),
    ]
    in_specs = [
        pl.BlockSpec((tm, D_MODEL), cur),
        pl.BlockSpec((tm, FOURIER_WIDTH), cur),
        pl.BlockSpec((tm, MLA_WIDTH), cur),
        _const_spec((1, FOURIER_WIDTH)),
        _const_spec((1, MLA_WIDTH)),
        _const_spec((D_MODEL, D_MODEL)),
        _const_spec((1, D_MODEL)),
        _const_spec((2 * LANES, D_MODEL)),
        _const_spec((LANES, LANES)),
        _const_spec((tm, tm)),
        _const_spec((N_EXPERTS, LANES)),
    ]
    out_specs = [
        pl.BlockSpec((tm, D_MODEL), cur),
        pl.BlockSpec((tm, ROW_TILES, LANES), lambda t: (t, 0, 0)),
        pl.BlockSpec((ROUTE_ROWS, tm), lambda t: (0, t)),
        pl.BlockSpec((tm, LANES), cur),
        pl.BlockSpec((tm, LANES), cur),
        pl.BlockSpec((N_EXPERTS, LANES), lambda t: (0, 0)),
    ]
    tri = (lax.broadcasted_iota(jnp.int32, (tm, tm), 0) < lax.broadcasted_iota(jnp.int32, (tm, tm), 1)).astype(BF16)
    return pl.pallas_call(
        _merge_kernel, grid=(n,), in_specs=in_specs, out_specs=out_specs, out_shape=out_shape,
        scratch_shapes=[pltpu.VMEM((N_EXPERTS, LANES), F32)],
        compiler_params=pltpu.CompilerParams(dimension_semantics=("arbitrary",), vmem_limit_bytes=VMEM_LIMIT),
        name="merge",
    )(x, f_out, a_out, w['g_f'], w['g_a'], w['w_out'], w['g_ffn'], w['wr_t'], w['b_rt'], tri, counts0)


def _dispatch_kernel(blk_end_ref, nused_ref, dest_ref, hn_a_ref, hn_b_ref, xs_ref, zero_buf, sem, zsem, *, n_a):
    i = pl.program_id(0)
    tm = hn_a_ref.shape[0]
    nb = xs_ref.shape[0] // EXPERT_TILE

    @pl.when(i == 0)
    def _():
        zero_buf[...] = jnp.zeros(zero_buf.shape, BF16)

        def zero_block(j):
            return pltpu.make_async_copy(zero_buf, xs_ref.at[pl.ds(j * EXPERT_TILE, EXPERT_TILE)], zsem)

        def each_group(fn):
            for e in range(N_EXPERTS):
                end = blk_end_ref[e]
                start = blk_end_ref[e - 1] if e else 0

                @pl.when(end > start)
                def _():
                    fn(end - 1)

        def each_tail(fn):
            def body(j, carry):
                fn(j)
                return carry

            lax.fori_loop(nused_ref[0], nb, body, 0)

        each_group(lambda j: zero_block(j).start())
        each_tail(lambda j: zero_block(j).start())
        each_group(lambda j: zero_block(j).wait())
        each_tail(lambda j: zero_block(j).wait())

    def scatter(src_ref):
        def body(r, carry):
            for k in range(TOP_K):
                d = dest_ref[0, 0, k * tm + r]
                pltpu.make_async_copy(src_ref.at[r], xs_ref.at[d], sem).start(priority=k)
            return carry

        lax.fori_loop(0, tm, body, 0)
        for _ in range(TOP_K):
            pltpu.make_async_copy(src_ref, xs_ref.at[pl.ds(0, tm)], sem).wait()

    @pl.when(i < n_a)
    def _():
        scatter(hn_a_ref)

    @pl.when(i >= n_a)
    def _():
        scatter(hn_b_ref)


def _dispatch(blk_end, n_used, dest, hn_a, hn_b, n_rows):
    tm = TOKEN_TILE
    n_a, n_b = hn_a.shape[0] // tm, hn_b.shape[0] // tm
    tile = (tm, ROW_TILES, LANES)
    grid_spec = pltpu.PrefetchScalarGridSpec(
        num_scalar_prefetch=2, grid=(n_a + n_b,),
        in_specs=[pl.BlockSpec((1, 1, TOP_K * tm), lambda i, be, nu: (i, 0, 0), memory_space=pltpu.SMEM),
                  pl.BlockSpec(tile, lambda i, be, nu: (jnp.minimum(i, n_a - 1), 0, 0)),
                  pl.BlockSpec(tile, lambda i, be, nu: (jnp.maximum(i - n_a, 0), 0, 0))],
        out_specs=pl.BlockSpec(memory_space=pl.ANY),
        scratch_shapes=[pltpu.VMEM((EXPERT_TILE, ROW_TILES, LANES), BF16),
                        pltpu.SemaphoreType.DMA(()), pltpu.SemaphoreType.DMA(())],
    )
    return pl.pallas_call(
        functools.partial(_dispatch_kernel, n_a=n_a), grid_spec=grid_spec,
        out_shape=jax.ShapeDtypeStruct((n_rows, ROW_TILES, LANES), BF16),
        compiler_params=pltpu.CompilerParams(dimension_semantics=("arbitrary",), vmem_limit_bytes=VMEM_LIMIT),
        name="dispatch",
    )(blk_end, n_used, dest, hn_a, hn_b)


def _expert_kernel(be_ref, nused_ref, blk_end_ref, xs_ref, wg_ref, wu_ref, wd_ref, y_ref,
                   wg_f, wu_f, wd_f, wgu_s, wd_s, group_ref, sems):
    i = pl.program_id(0)
    e = be_ref[i]

    def weight_copies(expert, slot):
        return [pltpu.make_async_copy(src.at[expert], dst.at[slot], sems.at[slot])
                for src, dst in ((wg_ref, wg_f), (wu_ref, wu_f), (wd_ref, wd_f))]

    @pl.when(i == 0)
    def _():
        group_ref[0] = 0
        for cp in weight_copies(e, 0):
            cp.start()

    @pl.when((i == 0) | (e != be_ref[jnp.maximum(i - 1, 0)]))
    def _():
        slot = lax.rem(group_ref[0], 2)
        for cp in weight_copies(e, slot):
            cp.wait()
        wgu_s[:, :EXPERT_DIM] = wg_f[slot].astype(BF16)
        wgu_s[:, EXPERT_DIM:] = wu_f[slot].astype(BF16)
        wd_s[...] = wd_f[slot].astype(BF16)
        nxt = blk_end_ref[e]

        @pl.when(nxt < nused_ref[0])
        def _():
            for cp in weight_copies(be_ref[nxt], 1 - slot):
                cp.start()

        group_ref[0] = group_ref[0] + 1

    @pl.when(i < nused_ref[0])
    def _():
        tm = xs_ref.shape[0]
        x = xs_ref[...].reshape(tm, D_MODEL)
        gu = jnp.dot(x, wgu_s[...], preferred_element_type=F32)
        g = gu[:, :EXPERT_DIM]
        u = gu[:, EXPERT_DIM:]
        hmid = (g * jax.nn.sigmoid(g) * u).astype(BF16)
        y = jnp.dot(hmid, wd_s[...], preferred_element_type=F32)
        y_ref[...] = y.astype(BF16).reshape(tm, ROW_TILES, LANES)

    @pl.when(i >= nused_ref[0])
    def _():
        y_ref[...] = jnp.zeros(y_ref.shape, BF16)


def _experts(block_expert, n_used, blk_end, xs, w):
    n_rows = xs.shape[0]
    tm = EXPERT_TILE
    nb = n_rows // tm
    row = lambda i, be, nu, bend: (jnp.minimum(i, nu[0] - 1), 0, 0)
    out_row = lambda i, be, nu, bend: (i, 0, 0)
    hbm = pl.BlockSpec(memory_space=pl.ANY)
    grid_spec = pltpu.PrefetchScalarGridSpec(
        num_scalar_prefetch=3, grid=(nb,),
        in_specs=[pl.BlockSpec((tm, ROW_TILES, LANES), row), hbm, hbm, hbm],
        out_specs=pl.BlockSpec((tm, ROW_TILES, LANES), out_row),
        scratch_shapes=[pltpu.VMEM((2, D_MODEL, EXPERT_DIM), F32), pltpu.VMEM((2, D_MODEL, EXPERT_DIM), F32),
                        pltpu.VMEM((2, EXPERT_DIM, D_MODEL), F32),
                        pltpu.VMEM((D_MODEL, 2 * EXPERT_DIM), BF16), pltpu.VMEM((EXPERT_DIM, D_MODEL), BF16),
                        pltpu.SMEM((1,), jnp.int32), pltpu.SemaphoreType.DMA((2,))],
    )
    return pl.pallas_call(
        _expert_kernel, grid_spec=grid_spec,
        out_shape=jax.ShapeDtypeStruct((n_rows, ROW_TILES, LANES), BF16),
        compiler_params=pltpu.CompilerParams(dimension_semantics=("arbitrary",), vmem_limit_bytes=VMEM_LIMIT),
        name="experts",
    )(block_expert, n_used, blk_end, xs, w['w_gate'], w['w_up'], w['w_down'])


def _combine_kernel(dest_ref, dest_next_ref, h_ref, w0_ref, w1_ref, g_ref, y_ref, o_ref, ybuf, sems, *, n_steps):
    tm = h_ref.shape[0]
    i = pl.program_id(0)
    slot = lax.rem(i, 2)

    def start_gather(d_ref, s):
        def body(r, carry):
            for k in range(TOP_K):
                d = d_ref[0, 0, k * tm + r]
                pltpu.make_async_copy(y_ref.at[d], ybuf.at[s, k, r], sems.at[s]).start(priority=k)
            return carry

        lax.fori_loop(0, tm, body, 0)

    @pl.when(i == 0)
    def _():
        start_gather(dest_ref, 0)

    @pl.when(i + 1 < n_steps)
    def _():
        start_gather(dest_next_ref, 1 - slot)

    for k in range(TOP_K):
        pltpu.make_async_copy(y_ref.at[pl.ds(0, tm)], ybuf.at[slot, k], sems.at[slot]).wait()

    y0 = ybuf[slot, 0].reshape(tm, D_MODEL)
    y1 = ybuf[slot, 1].reshape(tm, D_MODEL)
    w0 = w0_ref[...]
    w1 = w1_ref[...]
    cols = []
    for j in range(ROW_TILES):
        sl = slice(j * LANES, (j + 1) * LANES)
        cols.append(h_ref[:, sl] + w0 * y0[:, sl].astype(F32) + w1 * y1[:, sl].astype(F32))
    hs = jnp.concatenate(cols, axis=-1)
    o_ref[...] = _rms(hs, g_ref[...])


def _combine(dest, h, w0, w1, g_final, y):
    T = h.shape[0]
    tm = TOKEN_TILE
    n = T // tm
    dest3 = dest
    return pl.pallas_call(
        functools.partial(_combine_kernel, n_steps=n), grid=(n,),
        in_specs=[pl.BlockSpec((1, 1, TOP_K * tm), lambda i: (i, 0, 0), memory_space=pltpu.SMEM),
                  pl.BlockSpec((1, 1, TOP_K * tm), lambda i: (jnp.minimum(i + 1, n - 1), 0, 0),
                               memory_space=pltpu.SMEM),
                  pl.BlockSpec((tm, D_MODEL), lambda i: (i, 0)),
                  pl.BlockSpec((tm, LANES), lambda i: (i, 0)),
                  pl.BlockSpec((tm, LANES), lambda i: (i, 0)),
                  _const_spec((1, D_MODEL)),
                  pl.BlockSpec(memory_space=pl.ANY)],
        out_specs=pl.BlockSpec((tm, D_MODEL), lambda i: (i, 0)),
        out_shape=jax.ShapeDtypeStruct((T, D_MODEL), F32),
        scratch_shapes=[pltpu.VMEM((2, TOP_K, tm, ROW_TILES, LANES), BF16), pltpu.SemaphoreType.DMA((2,))],
        compiler_params=pltpu.CompilerParams(dimension_semantics=("arbitrary",), vmem_limit_bytes=VMEM_LIMIT),
        name="combine",
    )(dest3, dest3, h, w0, w1, g_final, y)


def _tables(S):
    pos = jnp.arange(S, dtype=F32)
    inv_freq = 1.0 / (ROPE_THETA ** (jnp.arange(0, QK_ROPE_DIM, 2, dtype=F32) / QK_ROPE_DIM))
    ang = pos[:, None] * inv_freq[None, :]
    cos, sin = jnp.cos(ang), jnp.sin(ang)
    cos2 = jnp.concatenate([cos, cos], axis=-1)
    sin2 = jnp.concatenate([sin, sin], axis=-1)
    zp = jnp.zeros((S, LANES - QK_ROPE_DIM), F32)
    t = {
        'cos_t': cos2.T, 'sin_t': sin2.T,
        'cos_p': jnp.concatenate([cos2, zp], axis=-1), 'sin_p': jnp.concatenate([sin2, zp], axis=-1),
    }

    def phase(rows, cols, n):
        m = (rows[:, None] * cols[None, :]) % n
        a = m.astype(F32) * jnp.float32(2.0 * math.pi / n)
        return jnp.cos(a), jnp.sin(a)

    i1 = jnp.arange(DFT_N1, dtype=jnp.int32)
    c1, s1 = phase(i1, i1, DFT_N1)
    t['dft_m1'] = jnp.concatenate(
        [jnp.concatenate([c1, s1], axis=1), jnp.concatenate([-s1, c1], axis=1)], axis=0).astype(BF16)
    k = jnp.arange(S, dtype=jnp.int32)
    n2 = jnp.arange(DFT_N2, dtype=jnp.int32)
    cg, sg = phase(k, n2, S)
    g = jnp.concatenate([cg, sg], axis=1) * jnp.float32(S ** -0.5)
    t['dft_g'] = g.reshape(DFT_N2, DFT_N1, 2 * DFT_N2).transpose(1, 0, 2).astype(BF16)
    return t


def _layer_weights(g_attn_norm, w_in, g_q_latent, w_uq, g_kv_latent, w_ukv, g_out_fourier, g_out_mla, w_out,
                   g_ffn_norm, w_router_group, b_router_group, w_router_expert, b_router_expert,
                   w_gate, w_up, w_down):
    half = QK_ROPE_DIM // 2
    s3 = FOURIER_WIDTH + Q_LORA_RANK + KV_LORA_RANK
    w_kr = w_in[:, s3:s3 + QK_ROPE_DIM]
    w_kr_rot = jnp.concatenate([-w_kr[:, half:], w_kr[:, :half]], axis=1)
    zc = jnp.zeros((D_MODEL, LANES - QK_ROPE_DIM), F32)
    w_in_ext = jnp.concatenate([w_in[:, :s3], w_kr, zc, w_kr_rot, zc], axis=1)

    c = jnp.arange(FOURIER_GROUP_DIM, dtype=jnp.int32)
    m = (c[:, None] * c[None, :]) % FOURIER_GROUP_DIM
    a = m.astype(F32) * jnp.float32(2.0 * math.pi / FOURIER_GROUP_DIM)
    cs = jnp.concatenate([jnp.cos(a), -jnp.sin(a)], axis=1) * jnp.float32(FOURIER_GROUP_DIM ** -0.5)

    qscale = jnp.float32(QK_HEAD_DIM ** -0.5 * math.log2(math.e))
    wq = (w_uq * qscale).reshape(Q_LORA_RANK, N_HEADS, QK_HEAD_DIM)
    wq_n = wq[:, :, :QK_NOPE_DIM]
    wq_r = wq[:, :, QK_NOPE_DIM:]
    to_t = lambda t: t.reshape(Q_LORA_RANK, -1).T
    wkv = w_ukv.reshape(KV_LORA_RANK, N_HEADS, QK_NOPE_DIM + V_HEAD_DIM)
    gpad = EXPERTS_PER_GROUP - N_EXPERT_GROUPS
    tail = LANES - EXPERTS_PER_GROUP - N_EXPERTS
    w_r = jnp.concatenate([w_router_group.T, jnp.zeros((gpad, D_MODEL), F32), w_router_expert.T,
                           jnp.zeros((tail, D_MODEL), F32)], axis=0)
    wr_hi = w_r.astype(BF16)
    b_r = jnp.concatenate([b_router_group, jnp.full((gpad,), -1e30, F32), b_router_expert,
                           jnp.zeros((tail,), F32)])
    return {
        'g_attn': g_attn_norm[None, :], 'w_in': w_in_ext.astype(BF16), 'cs': cs.astype(BF16),
        'g_q': g_q_latent[None, :], 'g_kv': g_kv_latent[None, :],
        'wqn_t': to_t(wq_n).astype(BF16), 'wqr_t': to_t(wq_r).astype(BF16),
        'wuk': wkv[:, :, :QK_NOPE_DIM].reshape(KV_LORA_RANK, -1).astype(BF16),
        'wuv_t': wkv[:, :, QK_NOPE_DIM:].reshape(KV_LORA_RANK, -1).T.astype(BF16),
        'g_f': g_out_fourier[None, :], 'g_a': g_out_mla[None, :], 'w_out': w_out.astype(BF16),
        'g_ffn': g_ffn_norm[None, :],
        'wr_t': jnp.concatenate([wr_hi, (w_r - wr_hi.astype(F32)).astype(BF16)], axis=0),
        'b_rt': jnp.broadcast_to(b_r[:, None], (LANES, LANES)),
        'w_gate': w_gate, 'w_up': w_up, 'w_down': w_down,
    }


def _block_plan(counts, n_assign):
    counts = counts[:, 0].astype(jnp.int32)
    nblk = (counts + EXPERT_TILE - 1) // EXPERT_TILE
    blk_end = jnp.cumsum(nblk)
    blk_start = blk_end - nblk
    n_blocks = (n_assign + N_EXPERTS * (EXPERT_TILE - 1) + EXPERT_TILE - 1) // EXPERT_TILE
    n_used = blk_end[-1]
    j = jnp.minimum(jnp.arange(n_blocks, dtype=jnp.int32), n_used - 1)
    block_expert = jnp.minimum(jnp.sum((blk_end[None, :] <= j[:, None]).astype(jnp.int32), axis=1), N_EXPERTS - 1)
    return blk_start, blk_end, block_expert, n_used[None], n_blocks * EXPERT_TILE


def _dest_rows(route, blk_start):
    T = route.shape[1]
    e = route[:TOP_K].astype(jnp.int32)
    rank = route[TOP_K:2 * TOP_K].astype(jnp.int32)
    first = jnp.zeros_like(e)
    for j in range(N_EXPERTS):
        first = jnp.where(e == j, blk_start[j], first)
    dest = (first * EXPERT_TILE + rank).reshape(TOP_K, T // TOKEN_TILE, TOKEN_TILE)
    return dest.transpose(1, 0, 2).reshape(T // TOKEN_TILE, 1, TOP_K * TOKEN_TILE)


def _mixers(x, wt, counts0):
    B, S, _ = x.shape
    T = B * S
    xr, xi, qt, kx, vt = _proj(x, wt)
    f_out = _seq_dft(xr, xi, wt)
    a_out = _attention(qt, kx, vt)
    flat = lambda t: t.reshape((T,) + t.shape[2:])
    return _merge(flat(x), flat(f_out), flat(a_out), wt, counts0)


def kernel(x_prompt, x_sample, g_attn_norm, w_in, g_q_latent, w_uq, g_kv_latent, w_ukv, g_out_fourier, g_out_mla, w_out, g_ffn_norm, w_router_group, b_router_group, w_router_expert, b_router_expert, w_gate, w_up, w_down, g_final):
    assert g_attn_norm.shape[0] == 1, "single-layer configuration"
    w = _layer_weights(g_attn_norm[0], w_in[0], g_q_latent[0], w_uq[0], g_kv_latent[0], w_ukv[0],
                       g_out_fourier[0], g_out_mla[0], w_out[0], g_ffn_norm[0], w_router_group[0],
                       b_router_group[0], w_router_expert[0], b_router_expert[0], w_gate[0], w_up[0], w_down[0])
    assert x_prompt.shape[1] == x_sample.shape[1]
    w.update(_tables(x_prompt.shape[1]))

    counts = jnp.zeros((N_EXPERTS, LANES), F32)
    batches = []
    for x in (x_prompt, x_sample):
        h, hn3, route, w0, w1, counts = _mixers(x, w, counts)
        batches.append((x.shape, h, hn3, route, w0, w1))

    n_assign = sum(b[1].shape[0] for b in batches) * TOP_K
    blk_start, blk_end, block_expert, n_used, n_rows = _block_plan(counts, n_assign)
    dests = [_dest_rows(b[3], blk_start) for b in batches]
    xs = _dispatch(blk_end, n_used, jnp.concatenate(dests, axis=0), batches[0][2], batches[1][2], n_rows)
    y = _experts(block_expert, n_used, blk_end, xs, w)
    outs = []
    for (shape, h, _, _, w0, w1), dest in zip(batches, dests):
        outs.append(_combine(dest, h, w0, w1, g_final[None, :], y).reshape(shape))
    return tuple(outs)
```

```python
import functools
import math

import jax
import jax.numpy as jnp
from jax import lax
from jax.experimental import pallas as pl
from jax.experimental.pallas import tpu as pltpu

D_MODEL = 2048
FOURIER_WIDTH = 1024
N_FOURIER_GROUPS = 4
FOURIER_GROUP_DIM = 256
MLA_WIDTH = 1024
V_HEAD_DIM = 128
N_HEADS = 8
QK_NOPE_DIM = 128
QK_ROPE_DIM = 64
QK_HEAD_DIM = 192
Q_LORA_RANK = 512
KV_LORA_RANK = 256
ROPE_THETA = 10000.0
N_EXPERT_GROUPS = 4
EXPERTS_PER_GROUP = 8
N_EXPERTS = 32
TOP_K = 2
EXPERT_DIM = 512
EPS = 1e-6

LANES = 128
QK_PAD = 256
V_ROWS = V_HEAD_DIM + 16
ROW_TILES = D_MODEL // LANES
ROUTE_ROWS = 8

DFT_N1 = 64
DFT_N2 = 128

TOKEN_TILE = 512
ATTN_TQ = 1024
ATTN_TK = 512
ATTN_SUB = 8
ATTN_QBLOCKS = 8
DFT_ROWS = 16
EXPERT_TILE = 256
VMEM_LIMIT = 56 * 1024 * 1024

F32 = jnp.float32
BF16 = jnp.bfloat16


def _const_spec(shape):
    nd = len(shape)
    return pl.BlockSpec(shape, lambda *_: (0,) * nd, pipeline_mode=pl.Buffered(1))


def _rms(x, g):
    return x * lax.rsqrt(jnp.mean(x * x, axis=-1, keepdims=True) + EPS) * g


def _proj_kernel(x_ref, g_attn_ref, w_in_ref, cs_ref, g_q_ref, g_kv_ref, wqn_ref, wqr_ref,
                 wuk_ref, wuvt_ref, cos_t_ref, sin_t_ref, cos_p_ref, sin_p_ref,
                 xr_ref, xi_ref, qt_ref, kx_ref, vt_ref):
    tm = x_ref.shape[1]
    x = x_ref[0]
    u = _rms(x, g_attn_ref[...]).astype(BF16)
    z = jnp.dot(u, w_in_ref[...], preferred_element_type=F32)

    zf = z[:, :FOURIER_WIDTH].astype(BF16)
    xr, xi = [], []
    for g in range(N_FOURIER_GROUPS):
        zg = zf[:, g * FOURIER_GROUP_DIM:(g + 1) * FOURIER_GROUP_DIM]
        xg = jnp.dot(zg, cs_ref[...], preferred_element_type=F32)
        xr.append(xg[:, :FOURIER_GROUP_DIM])
        xi.append(xg[:, FOURIER_GROUP_DIM:])
    xr_ref[0] = jnp.concatenate(xr, axis=-1).astype(BF16)
    xi_ref[0] = jnp.concatenate(xi, axis=-1).astype(BF16)

    o = FOURIER_WIDTH
    q_lat = z[:, o:o + Q_LORA_RANK]
    o += Q_LORA_RANK
    kv_lat = z[:, o:o + KV_LORA_RANK]
    o += KV_LORA_RANK
    kr = z[:, o:o + LANES]
    kr_rot = z[:, o + LANES:o + 2 * LANES]

    qn = _rms(q_lat, g_q_ref[...]).astype(BF16)
    kvn = _rms(kv_lat, g_kv_ref[...]).astype(BF16)

    nt = (((1,), (1,)), ((), ()))
    qn_t = lax.dot_general(wqn_ref[...], qn, nt, preferred_element_type=F32)
    qr_t = lax.dot_general(wqr_ref[...], qn, nt, preferred_element_type=F32)
    cos_t = cos_t_ref[...]
    sin_t = sin_t_ref[...]
    zpad = jnp.zeros((QK_PAD - QK_HEAD_DIM, tm), BF16)
    half = QK_ROPE_DIM // 2
    for h in range(N_HEADS):
        r0 = h * QK_PAD
        qt_ref[0, 0, r0:r0 + QK_NOPE_DIM, :] = qn_t[h * QK_NOPE_DIM:(h + 1) * QK_NOPE_DIM].astype(BF16)
        qr = qr_t[h * QK_ROPE_DIM:(h + 1) * QK_ROPE_DIM]
        rot = jnp.concatenate([-qr[half:], qr[:half]], axis=0)
        roped = qr * cos_t + rot * sin_t
        qt_ref[0, 0, r0 + QK_NOPE_DIM:r0 + QK_HEAD_DIM, :] = roped.astype(BF16)
        qt_ref[0, 0, r0 + QK_HEAD_DIM:r0 + QK_PAD, :] = zpad

    k_nope = jnp.dot(kvn, wuk_ref[...], preferred_element_type=F32)
    k_rope = (kr * cos_p_ref[...] + kr_rot * sin_p_ref[...]).astype(BF16)
    for h in range(N_HEADS):
        kx_ref[0, h, :, :QK_NOPE_DIM] = k_nope[:, h * QK_NOPE_DIM:(h + 1) * QK_NOPE_DIM].astype(BF16)
        kx_ref[0, h, :, QK_NOPE_DIM:] = k_rope

    v_t = lax.dot_general(wuvt_ref[...], kvn, nt, preferred_element_type=F32)
    vt_ref[0, 0, :, :V_HEAD_DIM, :] = v_t.astype(BF16).reshape(N_HEADS, V_HEAD_DIM, tm)
    vt_ref[0, 0, :, V_HEAD_DIM:, :] = jnp.ones((N_HEADS, V_ROWS - V_HEAD_DIM, tm), BF16)


def _proj(x, w):
    B, S, _ = x.shape
    tm = ATTN_TK
    q_split = ATTN_TQ // tm
    grid = (B, S // tm)
    tok = lambda b, i: (b, i, 0)
    out_shape = [
        jax.ShapeDtypeStruct((B, S, FOURIER_WIDTH), BF16),
        jax.ShapeDtypeStruct((B, S, FOURIER_WIDTH), BF16),
        jax.ShapeDtypeStruct((B, S // ATTN_TQ, N_HEADS * QK_PAD, ATTN_TQ), BF16),
        jax.ShapeDtypeStruct((B, N_HEADS, S, QK_PAD), BF16),
        jax.ShapeDtypeStruct((B, S // tm, N_HEADS, V_ROWS, tm), BF16),
    ]
    in_specs = [
        pl.BlockSpec((1, tm, D_MODEL), tok),
        _const_spec((1, D_MODEL)),
        _const_spec(w['w_in'].shape),
        _const_spec(w['cs'].shape),
        _const_spec((1, Q_LORA_RANK)),
        _const_spec((1, KV_LORA_RANK)),
        _const_spec(w['wqn_t'].shape),
        _const_spec(w['wqr_t'].shape),
        _const_spec(w['wuk'].shape),
        _const_spec(w['wuv_t'].shape),
        pl.BlockSpec((QK_ROPE_DIM, tm), lambda b, i: (0, i)),
        pl.BlockSpec((QK_ROPE_DIM, tm), lambda b, i: (0, i)),
        pl.BlockSpec((tm, LANES), lambda b, i: (i, 0)),
        pl.BlockSpec((tm, LANES), lambda b, i: (i, 0)),
    ]
    out_specs = [
        pl.BlockSpec((1, tm, FOURIER_WIDTH), tok),
        pl.BlockSpec((1, tm, FOURIER_WIDTH), tok),
        pl.BlockSpec((1, 1, N_HEADS * QK_PAD, tm), lambda b, i: (b, i // q_split, 0, i % q_split)),
        pl.BlockSpec((1, N_HEADS, tm, QK_PAD), lambda b, i: (b, 0, i, 0)),
        pl.BlockSpec((1, 1, N_HEADS, V_ROWS, tm), lambda b, i: (b, i, 0, 0, 0)),
    ]
    return pl.pallas_call(
        _proj_kernel, grid=grid, in_specs=in_specs, out_specs=out_specs, out_shape=out_shape,
        compiler_params=pltpu.CompilerParams(
            dimension_semantics=("arbitrary", "arbitrary"), vmem_limit_bytes=VMEM_LIMIT),
        name="proj",
    )(x, w['g_attn'], w['w_in'], w['cs'], w['g_q'], w['g_kv'], w['wqn_t'], w['wqr_t'],
      w['wuk'], w['wuv_t'], w['cos_t'], w['sin_t'], w['cos_p'], w['sin_p'])


def _dft1_kernel(m1_ref, xr_ref, xi_ref, zr_ref, zi_ref):
    nb, c = xr_ref.shape[2], xr_ref.shape[3]
    xr = xr_ref[0].reshape(DFT_N1, nb * c)
    xi = xi_ref[0].reshape(DFT_N1, nb * c)
    xc = jnp.concatenate([xr, xi], axis=0)
    z = jnp.dot(m1_ref[...], xc, preferred_element_type=F32).astype(BF16)
    zr_ref[0] = z[:DFT_N1].reshape(DFT_N1, nb, c)
    zi_ref[0] = z[DFT_N1:].reshape(DFT_N1, nb, c)


def _dft2_kernel(g_ref, zr_ref, zi_ref, o_ref):
    kb, c = zr_ref.shape[1], zr_ref.shape[3]
    ys = []
    for j in range(kb):
        zc = jnp.concatenate([zr_ref[0, j], zi_ref[0, j]], axis=0)
        ys.append(jnp.dot(g_ref[j], zc, preferred_element_type=F32).astype(BF16))
    o_ref[0] = jnp.concatenate(ys, axis=-1).reshape(DFT_N2, kb, c)


def _seq_dft(xr, xi, w):
    B, S, C = xr.shape
    assert S == DFT_N1 * DFT_N2
    nb = DFT_ROWS
    xr4 = xr.reshape(B, DFT_N1, DFT_N2, C)
    xi4 = xi.reshape(B, DFT_N1, DFT_N2, C)
    rows = pl.BlockSpec((1, DFT_N1, nb, C), lambda b, j: (b, 0, j, 0))
    zr, zi = pl.pallas_call(
        _dft1_kernel, grid=(B, DFT_N2 // nb),
        in_specs=[_const_spec((2 * DFT_N1, 2 * DFT_N1)), rows, rows],
        out_specs=[rows, rows],
        out_shape=[jax.ShapeDtypeStruct((B, DFT_N1, DFT_N2, C), BF16)] * 2,
        compiler_params=pltpu.CompilerParams(
            dimension_semantics=("arbitrary", "arbitrary"), vmem_limit_bytes=VMEM_LIMIT),
        name="dft1",
    )(w['dft_m1'], xr4, xi4)
    kb = DFT_ROWS
    blk = pl.BlockSpec((1, kb, DFT_N2, C), lambda b, k: (b, k, 0, 0))
    y = pl.pallas_call(
        _dft2_kernel, grid=(B, DFT_N1 // kb),
        in_specs=[pl.BlockSpec((kb, DFT_N2, 2 * DFT_N2), lambda b, k: (k, 0, 0)), blk, blk],
        out_specs=pl.BlockSpec((1, DFT_N2, kb, C), lambda b, k: (b, 0, k, 0)),
        out_shape=jax.ShapeDtypeStruct((B, DFT_N2, DFT_N1, C), BF16),
        compiler_params=pltpu.CompilerParams(
            dimension_semantics=("arbitrary", "arbitrary"), vmem_limit_bytes=VMEM_LIMIT),
        name="dft2",
    )(w['dft_g'], zr, zi)
    return y.reshape(B, S, C)


def _attn_kernel(q_ref, k_ref, v_ref, o_ref, s_scr, acc_ref):
    tk = v_ref.shape[-1]
    nch = v_ref.shape[1]
    nq, tq = q_ref.shape[1], q_ref.shape[-1]
    groups = nch // ATTN_SUB

    def scores(f):
        j = jnp.minimum(f // nch, nq - 1)
        c = lax.rem(f, nch)
        k_c = k_ref[0, 0, pl.ds(pl.multiple_of(c * tk, tk), tk), :]
        return jnp.dot(k_c, q_ref[0, j], preferred_element_type=F32)

    def step(f, slot, m, s_max):
        s_next = scores(f + 1)
        s_scr[1 - slot] = s_next
        next_max = jnp.max(s_next, axis=0, keepdims=True)
        s = s_scr[slot]
        m_new = jnp.maximum(m, s_max)
        alpha = jnp.exp2(m - m_new)
        p = jnp.exp2(s - m_new)
        pv = jnp.dot(v_ref[0, lax.rem(f, nch), 0], p.astype(BF16), preferred_element_type=F32)
        acc_ref[...] = alpha * acc_ref[...] + pv
        return m_new, next_max

    acc_ref[...] = jnp.zeros(acc_ref.shape, F32)
    s_first = scores(0)
    s_scr[0] = s_first
    m0 = jnp.full((1, tq), -1e30, F32)

    def body(g, carry):
        m, s_max = carry
        for si in range(ATTN_SUB):
            m, s_max = step(g * ATTN_SUB + si, si % 2, m, s_max)
        block_done = lax.rem(g, groups) == groups - 1

        @pl.when(block_done)
        def _():
            j = g // groups
            o_t = acc_ref[:V_HEAD_DIM] * (1.0 / acc_ref[V_HEAD_DIM:V_HEAD_DIM + 1])
            o_ref[0, pl.ds(pl.multiple_of(j * tq, tq), tq), :] = o_t.T.astype(BF16)
            acc_ref[...] = jnp.zeros(acc_ref.shape, F32)

        return jnp.where(block_done, m0, m), s_max

    lax.fori_loop(0, nq * groups, body, (m0, jnp.max(s_first, axis=0, keepdims=True)))


def _attention(qt, kx, vt):
    B, nqb, _, tq = qt.shape
    S = nqb * tq
    nch, tk = vt.shape[1], vt.shape[-1]
    nq = ATTN_QBLOCKS
    assert ATTN_SUB % 2 == 0 and nch % ATTN_SUB == 0 and nqb % nq == 0
    return pl.pallas_call(
        _attn_kernel, grid=(B, N_HEADS, nqb // nq),
        in_specs=[pl.BlockSpec((1, nq, QK_PAD, tq), lambda b, h, i: (b, i, h, 0)),
                  pl.BlockSpec((1, 1, S, QK_PAD), lambda b, h, i: (b, h, 0, 0)),
                  pl.BlockSpec((1, nch, 1, V_ROWS, tk), lambda b, h, i: (b, 0, h, 0, 0))],
        out_specs=pl.BlockSpec((1, nq * tq, V_HEAD_DIM), lambda b, h, i: (b, i, h)),
        out_shape=jax.ShapeDtypeStruct((B, S, MLA_WIDTH), BF16),
        scratch_shapes=[pltpu.VMEM((2, tk, tq), F32), pltpu.VMEM((V_ROWS, tq), F32)],
        compiler_params=pltpu.CompilerParams(
            dimension_semantics=("arbitrary", "arbitrary", "arbitrary"), vmem_limit_bytes=VMEM_LIMIT),
        name="attn",
    )(qt, kx, vt)


def _merge_kernel(x_ref, f_ref, a_ref, g_f_ref, g_a_ref, w_out_ref, g_ffn_ref, wr_ref, b_r_ref, tri_ref, cnt0_ref,
                  h_ref, hn_ref, route_ref, w0_ref, w1_ref, cnt_ref, carry_ref):
    tm = x_ref.shape[0]

    @pl.when(pl.program_id(0) == 0)
    def _():
        carry_ref[...] = cnt0_ref[...]

    fn = _rms(f_ref[...].astype(F32), g_f_ref[...]).astype(BF16)
    an = _rms(a_ref[...].astype(F32), g_a_ref[...]).astype(BF16)
    merged = jnp.concatenate([fn, an], axis=-1)
    h = x_ref[...] + jnp.dot(merged, w_out_ref[...], preferred_element_type=F32)
    h_ref[...] = h
    hn = _rms(h, g_ffn_ref[...])
    hn_ref[...] = hn.astype(BF16).reshape(tm, ROW_TILES, LANES)

    hn_hi = hn.astype(BF16)
    hn_lo = (hn - hn_hi.astype(F32)).astype(BF16)
    nt = (((1,), (1,)), ((), ()))
    p_t = (lax.dot_general(wr_ref[...], hn_hi, nt, preferred_element_type=F32)
           + lax.dot_general(wr_ref[...], hn_lo, nt, preferred_element_type=F32))
    reps = tm // LANES
    logits = p_t[:LANES] + p_t[LANES:] + jnp.tile(b_r_ref[...], (1, reps))

    row = lax.broadcasted_iota(jnp.int32, (EXPERTS_PER_GROUP, tm), 0)
    neg = jnp.float32(-1e30)
    gl = logits[:EXPERTS_PER_GROUP]
    gmax = jnp.max(gl, axis=0, keepdims=True)
    g_sel = jnp.min(jnp.where(gl == gmax, row, EXPERTS_PER_GROUP), axis=0, keepdims=True)
    g_p = 1.0 / jnp.sum(jnp.exp(gl - gmax), axis=0, keepdims=True)
    el = logits[EXPERTS_PER_GROUP:2 * EXPERTS_PER_GROUP]
    for g in range(1, N_EXPERT_GROUPS):
        el = jnp.where(g_sel == g, logits[(g + 1) * EXPERTS_PER_GROUP:(g + 2) * EXPERTS_PER_GROUP], el)
    m1 = jnp.max(el, axis=0, keepdims=True)
    i1 = jnp.min(jnp.where(el == m1, row, EXPERTS_PER_GROUP), axis=0, keepdims=True)
    el2 = jnp.where(row == i1, neg, el)
    m2 = jnp.max(el2, axis=0, keepdims=True)
    i2 = jnp.min(jnp.where(el2 == m2, row, EXPERTS_PER_GROUP), axis=0, keepdims=True)
    t = jnp.exp(m2 - m1)
    inv = 1.0 / (1.0 + t)
    w0 = g_p * inv
    w1 = g_p * t * inv
    e0 = g_sel * EXPERTS_PER_GROUP + i1
    e1 = g_sel * EXPERTS_PER_GROUP + i2

    row_e = lax.broadcasted_iota(jnp.int32, (N_EXPERTS, tm), 0)
    oh0 = jnp.where(row_e == e0, 1.0, 0.0)
    oh1 = jnp.where(row_e == e1, 1.0, 0.0)
    oh = oh0 + oh1
    before = (jnp.dot(oh.astype(BF16), tri_ref[...], preferred_element_type=F32)
              + jnp.tile(carry_ref[...], (1, reps)))
    r0 = jnp.sum(before * oh0, axis=0, keepdims=True)
    r1 = jnp.sum(before * oh1, axis=0, keepdims=True)
    total = carry_ref[...] + jnp.broadcast_to(jnp.sum(oh, axis=1, keepdims=True), carry_ref.shape)
    carry_ref[...] = total
    cnt_ref[...] = total

    row_r = lax.broadcasted_iota(jnp.int32, (ROUTE_ROWS, tm), 0)
    route_ref[...] = jnp.where(row_r == 0, e0.astype(F32), jnp.where(row_r == 1, e1.astype(F32),
                               jnp.where(row_r == 2, r0, jnp.where(row_r == 3, r1, 0.0))))
    row_l = lax.broadcasted_iota(jnp.int32, (LANES, tm), 0)
    w_rows = jnp.where(row_l == 0, w0, jnp.where(row_l == 1, w1, 0.0)).T
    w0_ref[...] = jnp.broadcast_to(w_rows[:, 0:1], (tm, LANES))
    w1_ref[...] = jnp.broadcast_to(w_rows[:, 1:2], (tm, LANES))


def _merge(x, f_out, a_out, w, counts0):
    T = x.shape[0]
    tm = TOKEN_TILE
    n = T // tm
    cur = lambda t: (t, 0)
    out_shape = [
        jax.ShapeDtypeStruct((T, D_MODEL), F32),
        jax.ShapeDtypeStruct((T, ROW_TILES, LANES), BF16),
        jax.ShapeDtypeStruct((ROUTE_ROWS, T), F32),
        jax.ShapeDtypeStruct((T, LANES), F32),
        jax.ShapeDtypeStruct((T, LANES), F32),
        jax.ShapeDtypeStruct((N_EXPERTS, LANES), F32),
    ]
    in_specs = [
        pl.BlockSpec((tm, D_MODEL), cur),
        pl.BlockSpec((tm, FOURIER_WIDTH), cur),
        pl.BlockSpec((tm, MLA_WIDTH), cur),
        _const_spec((1, FOURIER_WIDTH)),
        _const_spec((1, MLA_WIDTH)),
        _const_spec((D_MODEL, D_MODEL)),
        _const_spec((1, D_MODEL)),
        _const_spec((2 * LANES, D_MODEL)),
        _const_spec((LANES, LANES)),
        _const_spec((tm, tm)),
        _const_spec((N_EXPERTS, LANES)),
    ]
    out_specs = [
        pl.BlockSpec((tm, D_MODEL), cur),
        pl.BlockSpec((tm, ROW_TILES, LANES), lambda t: (t, 0, 0)),
        pl.BlockSpec((ROUTE_ROWS, tm), lambda t: (0, t)),
        pl.BlockSpec((tm, LANES), cur),
        pl.BlockSpec((tm, LANES), cur),
        pl.BlockSpec((N_EXPERTS, LANES), lambda t: (0, 0)),
    ]
    tri = (lax.broadcasted_iota(jnp.int32, (tm, tm), 0) < lax.broadcasted_iota(jnp.int32, (tm, tm), 1)).astype(BF16)
    return pl.pallas_call(
        _merge_kernel, grid=(n,), in_specs=in_specs, out_specs=out_specs, out_shape=out_shape,
        scratch_shapes=[pltpu.VMEM((N_EXPERTS, LANES), F32)],
        compiler_params=pltpu.CompilerParams(dimension_semantics=("arbitrary",), vmem_limit_bytes=VMEM_LIMIT),
        name="merge",
    )(x, f_out, a_out, w['g_f'], w['g_a'], w['w_out'], w['g_ffn'], w['wr_t'], w['b_rt'], tri, counts0)


def _dispatch_kernel(blk_end_ref, nused_ref, dest_ref, hn_a_ref, hn_b_ref, xs_ref, zero_buf, sem, zsem, *, n_a):
    i = pl.program_id(0)
    tm = hn_a_ref.shape[0]
    nb = xs_ref.shape[0] // EXPERT_TILE

    @pl.when(i == 0)
    def _():
        zero_buf[...] = jnp.zeros(zero_buf.shape, BF16)

        def zero_block(j):
            return pltpu.make_async_copy(zero_buf, xs_ref.at[pl.ds(j * EXPERT_TILE, EXPERT_TILE)], zsem)

        def each_group(fn):
            for e in range(N_EXPERTS):
                end = blk_end_ref[e]
                start = blk_end_ref[e - 1] if e else 0

                @pl.when(end > start)
                def _():
                    fn(end - 1)

        def each_tail(fn):
            def body(j, carry):
                fn(j)
                return carry

            lax.fori_loop(nused_ref[0], nb, body, 0)

        each_group(lambda j: zero_block(j).start())
        each_tail(lambda j: zero_block(j).start())
        each_group(lambda j: zero_block(j).wait())
        each_tail(lambda j: zero_block(j).wait())

    def scatter(src_ref):
        def body(r, carry):
            for k in range(TOP_K):
                d = dest_ref[0, 0, k * tm + r]
                pltpu.make_async_copy(src_ref.at[r], xs_ref.at[d], sem).start(priority=k)
            return carry

        lax.fori_loop(0, tm, body, 0)
        for _ in range(TOP_K):
            pltpu.make_async_copy(src_ref, xs_ref.at[pl.ds(0, tm)], sem).wait()

    @pl.when(i < n_a)
    def _():
        scatter(hn_a_ref)

    @pl.when(i >= n_a)
    def _():
        scatter(hn_b_ref)


def _dispatch(blk_end, n_used, dest, hn_a, hn_b, n_rows):
    tm = TOKEN_TILE
    n_a, n_b = hn_a.shape[0] // tm, hn_b.shape[0] // tm
    tile = (tm, ROW_TILES, LANES)
    grid_spec = pltpu.PrefetchScalarGridSpec(
        num_scalar_prefetch=2, grid=(n_a + n_b,),
        in_specs=[pl.BlockSpec((1, 1, TOP_K * tm), lambda i, be, nu: (i, 0, 0), memory_space=pltpu.SMEM),
                  pl.BlockSpec(tile, lambda i, be, nu: (jnp.minimum(i, n_a - 1), 0, 0)),
                  pl.BlockSpec(tile, lambda i, be, nu: (jnp.maximum(i - n_a, 0), 0, 0))],
        out_specs=pl.BlockSpec(memory_space=pl.ANY),
        scratch_shapes=[pltpu.VMEM((EXPERT_TILE, ROW_TILES, LANES), BF16),
                        pltpu.SemaphoreType.DMA(()), pltpu.SemaphoreType.DMA(())],
    )
    return pl.pallas_call(
        functools.partial(_dispatch_kernel, n_a=n_a), grid_spec=grid_spec,
        out_shape=jax.ShapeDtypeStruct((n_rows, ROW_TILES, LANES), BF16),
        compiler_params=pltpu.CompilerParams(dimension_semantics=("arbitrary",), vmem_limit_bytes=VMEM_LIMIT),
        name="dispatch",
    )(blk_end, n_used, dest, hn_a, hn_b)


def _expert_kernel(be_ref, nused_ref, blk_end_ref, xs_ref, wg_ref, wu_ref, wd_ref, y_ref,
                   wg_f, wu_f, wd_f, wgu_s, wd_s, group_ref, sems):
    i = pl.program_id(0)
    e = be_ref[i]

    def weight_copies(expert, slot):
        return [pltpu.make_async_copy(src.at[expert], dst.at[slot], sems.at[slot])
                for src, dst in ((wg_ref, wg_f), (wu_ref, wu_f), (wd_ref, wd_f))]

    @pl.when(i == 0)
    def _():
        group_ref[0] = 0
        for cp in weight_copies(e, 0):
            cp.start()

    @pl.when((i == 0) | (e != be_ref[jnp.maximum(i - 1, 0)]))
    def _():
        slot = lax.rem(group_ref[0], 2)
        for cp in weight_copies(e, slot):
            cp.wait()
        wgu_s[:, :EXPERT_DIM] = wg_f[slot].astype(BF16)
        wgu_s[:, EXPERT_DIM:] = wu_f[slot].astype(BF16)
        wd_s[...] = wd_f[slot].astype(BF16)
        nxt = blk_end_ref[e]

        @pl.when(nxt < nused_ref[0])
        def _():
            for cp in weight_copies(be_ref[nxt], 1 - slot):
                cp.start(priority=1)

        group_ref[0] = group_ref[0] + 1

    @pl.when(i < nused_ref[0])
    def _():
        tm = xs_ref.shape[0]
        x = xs_ref[...].reshape(tm, D_MODEL)
        gu = jnp.dot(x, wgu_s[...], preferred_element_type=F32)
        g = gu[:, :EXPERT_DIM]
        u = gu[:, EXPERT_DIM:]
        hmid = (g * jax.nn.sigmoid(g) * u).astype(BF16)
        y = jnp.dot(hmid, wd_s[...], preferred_element_type=F32)
        y_ref[...] = y.astype(BF16).reshape(tm, ROW_TILES, LANES)

    @pl.when(i >= nused_ref[0])
    def _():
        y_ref[...] = jnp.zeros(y_ref.shape, BF16)


def _experts(block_expert, n_used, blk_end, xs, w):
    n_rows = xs.shape[0]
    tm = EXPERT_TILE
    nb = n_rows // tm
    row = lambda i, be, nu, bend: (jnp.minimum(i, nu[0] - 1), 0, 0)
    out_row = lambda i, be, nu, bend: (i, 0, 0)
    hbm = pl.BlockSpec(memory_space=pl.ANY)
    grid_spec = pltpu.PrefetchScalarGridSpec(
        num_scalar_prefetch=3, grid=(nb,),
        in_specs=[pl.BlockSpec((tm, ROW_TILES, LANES), row), hbm, hbm, hbm],
        out_specs=pl.BlockSpec((tm, ROW_TILES, LANES), out_row),
        scratch_shapes=[pltpu.VMEM((2, D_MODEL, EXPERT_DIM), F32), pltpu.VMEM((2, D_MODEL, EXPERT_DIM), F32),
                        pltpu.VMEM((2, EXPERT_DIM, D_MODEL), F32),
                        pltpu.VMEM((D_MODEL, 2 * EXPERT_DIM), BF16), pltpu.VMEM((EXPERT_DIM, D_MODEL), BF16),
                        pltpu.SMEM((1,), jnp.int32), pltpu.SemaphoreType.DMA((2,))],
    )
    return pl.pallas_call(
        _expert_kernel, grid_spec=grid_spec,
        out_shape=jax.ShapeDtypeStruct((n_rows, ROW_TILES, LANES), BF16),
        compiler_params=pltpu.CompilerParams(dimension_semantics=("arbitrary",), vmem_limit_bytes=VMEM_LIMIT),
        name="experts",
    )(block_expert, n_used, blk_end, xs, w['w_gate'], w['w_up'], w['w_down'])


def _combine_kernel(dest_ref, dest_next_ref, h_ref, w0_ref, w1_ref, g_ref, y_ref, o_ref, ybuf, sems, *, n_steps):
    tm = h_ref.shape[0]
    i = pl.program_id(0)
    slot = lax.rem(i, 2)

    def start_gather(d_ref, s):
        def body(r, carry):
            for k in range(TOP_K):
                d = d_ref[0, 0, k * tm + r]
                pltpu.make_async_copy(y_ref.at[d], ybuf.at[s, k, r], sems.at[s]).start(priority=k)
            return carry

        lax.fori_loop(0, tm, body, 0)

    @pl.when(i == 0)
    def _():
        start_gather(dest_ref, 0)

    @pl.when(i + 1 < n_steps)
    def _():
        start_gather(dest_next_ref, 1 - slot)

    for k in range(TOP_K):
        pltpu.make_async_copy(y_ref.at[pl.ds(0, tm)], ybuf.at[slot, k], sems.at[slot]).wait()

    y0 = ybuf[slot, 0].reshape(tm, D_MODEL)
    y1 = ybuf[slot, 1].reshape(tm, D_MODEL)
    w0 = w0_ref[...]
    w1 = w1_ref[...]
    cols = []
    for j in range(ROW_TILES):
        sl = slice(j * LANES, (j + 1) * LANES)
        cols.append(h_ref[:, sl] + w0 * y0[:, sl].astype(F32) + w1 * y1[:, sl].astype(F32))
    hs = jnp.concatenate(cols, axis=-1)
    o_ref[...] = _rms(hs, g_ref[...])


def _combine(dest, h, w0, w1, g_final, y):
    T = h.shape[0]
    tm = TOKEN_TILE
    n = T // tm
    dest3 = dest
    return pl.pallas_call(
        functools.partial(_combine_kernel, n_steps=n), grid=(n,),
        in_specs=[pl.BlockSpec((1, 1, TOP_K * tm), lambda i: (i, 0, 0), memory_space=pltpu.SMEM),
                  pl.BlockSpec((1, 1, TOP_K * tm), lambda i: (jnp.minimum(i + 1, n - 1), 0, 0),
                               memory_space=pltpu.SMEM),
                  pl.BlockSpec((tm, D_MODEL), lambda i: (i, 0)),
                  pl.BlockSpec((tm, LANES), lambda i: (i, 0)),
                  pl.BlockSpec((tm, LANES), lambda i: (i, 0)),
                  _const_spec((1, D_MODEL)),
                  pl.BlockSpec(memory_space=pl.ANY)],
        out_specs=pl.BlockSpec((tm, D_MODEL), lambda i: (i, 0)),
        out_shape=jax.ShapeDtypeStruct((T, D_MODEL), F32),
        scratch_shapes=[pltpu.VMEM((2, TOP_K, tm, ROW_TILES, LANES), BF16), pltpu.SemaphoreType.DMA((2,))],
        compiler_params=pltpu.CompilerParams(dimension_semantics=("arbitrary",), vmem_limit_bytes=VMEM_LIMIT),
        name="combine",
    )(dest3, dest3, h, w0, w1, g_final, y)


def _tables(S):
    pos = jnp.arange(S, dtype=F32)
    inv_freq = 1.0 / (ROPE_THETA ** (jnp.arange(0, QK_ROPE_DIM, 2, dtype=F32) / QK_ROPE_DIM))
    ang = pos[:, None] * inv_freq[None, :]
    cos, sin = jnp.cos(ang), jnp.sin(ang)
    cos2 = jnp.concatenate([cos, cos], axis=-1)
    sin2 = jnp.concatenate([sin, sin], axis=-1)
    zp = jnp.zeros((S, LANES - QK_ROPE_DIM), F32)
    t = {
        'cos_t': cos2.T, 'sin_t': sin2.T,
        'cos_p': jnp.concatenate([cos2, zp], axis=-1), 'sin_p': jnp.concatenate([sin2, zp], axis=-1),
    }

    def phase(rows, cols, n):
        m = (rows[:, None] * cols[None, :]) % n
        a = m.astype(F32) * jnp.float32(2.0 * math.pi / n)
        return jnp.cos(a), jnp.sin(a)

    i1 = jnp.arange(DFT_N1, dtype=jnp.int32)
    c1, s1 = phase(i1, i1, DFT_N1)
    t['dft_m1'] = jnp.concatenate(
        [jnp.concatenate([c1, s1], axis=1), jnp.concatenate([-s1, c1], axis=1)], axis=0).astype(BF16)
    k = jnp.arange(S, dtype=jnp.int32)
    n2 = jnp.arange(DFT_N2, dtype=jnp.int32)
    cg, sg = phase(k, n2, S)
    g = jnp.concatenate([cg, sg], axis=1) * jnp.float32(S ** -0.5)
    t['dft_g'] = g.reshape(DFT_N2, DFT_N1, 2 * DFT_N2).transpose(1, 0, 2).astype(BF16)
    return t


def _layer_weights(g_attn_norm, w_in, g_q_latent, w_uq, g_kv_latent, w_ukv, g_out_fourier, g_out_mla, w_out,
                   g_ffn_norm, w_router_group, b_router_group, w_router_expert, b_router_expert,
                   w_gate, w_up, w_down):
    half = QK_ROPE_DIM // 2
    s3 = FOURIER_WIDTH + Q_LORA_RANK + KV_LORA_RANK
    w_kr = w_in[:, s3:s3 + QK_ROPE_DIM]
    w_kr_rot = jnp.concatenate([-w_kr[:, half:], w_kr[:, :half]], axis=1)
    zc = jnp.zeros((D_MODEL, LANES - QK_ROPE_DIM), F32)
    w_in_ext = jnp.concatenate([w_in[:, :s3], w_kr, zc, w_kr_rot, zc], axis=1)

    c = jnp.arange(FOURIER_GROUP_DIM, dtype=jnp.int32)
    m = (c[:, None] * c[None, :]) % FOURIER_GROUP_DIM
    a = m.astype(F32) * jnp.float32(2.0 * math.pi / FOURIER_GROUP_DIM)
    cs = jnp.concatenate([jnp.cos(a), -jnp.sin(a)], axis=1) * jnp.float32(FOURIER_GROUP_DIM ** -0.5)

    qscale = jnp.float32(QK_HEAD_DIM ** -0.5 * math.log2(math.e))
    wq = (w_uq * qscale).reshape(Q_LORA_RANK, N_HEADS, QK_HEAD_DIM)
    wq_n = wq[:, :, :QK_NOPE_DIM]
    wq_r = wq[:, :, QK_NOPE_DIM:]
    to_t = lambda t: t.reshape(Q_LORA_RANK, -1).T
    wkv = w_ukv.reshape(KV_LORA_RANK, N_HEADS, QK_NOPE_DIM + V_HEAD_DIM)
    gpad = EXPERTS_PER_GROUP - N_EXPERT_GROUPS
    tail = LANES - EXPERTS_PER_GROUP - N_EXPERTS
    w_r = jnp.concatenate([w_router_group.T, jnp.zeros((gpad, D_MODEL), F32), w_router_expert.T,
                           jnp.zeros((tail, D_MODEL), F32)], axis=0)
    wr_hi = w_r.astype(BF16)
    b_r = jnp.concatenate([b_router_group, jnp.full((gpad,), -1e30, F32), b_router_expert,
                           jnp.zeros((tail,), F32)])
    return {
        'g_attn': g_attn_norm[None, :], 'w_in': w_in_ext.astype(BF16), 'cs': cs.astype(BF16),
        'g_q': g_q_latent[None, :], 'g_kv': g_kv_latent[None, :],
        'wqn_t': to_t(wq_n).astype(BF16), 'wqr_t': to_t(wq_r).astype(BF16),
        'wuk': wkv[:, :, :QK_NOPE_DIM].reshape(KV_LORA_RANK, -1).astype(BF16),
        'wuv_t': wkv[:, :, QK_NOPE_DIM:].reshape(KV_LORA_RANK, -1).T.astype(BF16),
        'g_f': g_out_fourier[None, :], 'g_a': g_out_mla[None, :], 'w_out': w_out.astype(BF16),
        'g_ffn': g_ffn_norm[None, :],
        'wr_t': jnp.concatenate([wr_hi, (w_r - wr_hi.astype(F32)).astype(BF16)], axis=0),
        'b_rt': jnp.broadcast_to(b_r[:, None], (LANES, LANES)),
        'w_gate': w_gate, 'w_up': w_up, 'w_down': w_down,
    }


def _block_plan(counts, n_assign):
    counts = counts[:, 0].astype(jnp.int32)
    nblk = (counts + EXPERT_TILE - 1) // EXPERT_TILE
    blk_end = jnp.cumsum(nblk)
    blk_start = blk_end - nblk
    n_blocks = (n_assign + N_EXPERTS * (EXPERT_TILE - 1) + EXPERT_TILE - 1) // EXPERT_TILE
    n_used = blk_end[-1]
    j = jnp.minimum(jnp.arange(n_blocks, dtype=jnp.int32), n_used - 1)
    block_expert = jnp.minimum(jnp.sum((blk_end[None, :] <= j[:, None]).astype(jnp.int32), axis=1), N_EXPERTS - 1)
    return blk_start, blk_end, block_expert, n_used[None], n_blocks * EXPERT_TILE


def _dest_rows(route, blk_start):
    T = route.shape[1]
    e = route[:TOP_K].astype(jnp.int32)
    rank = route[TOP_K:2 * TOP_K].astype(jnp.int32)
    first = jnp.zeros_like(e)
    for j in range(N_EXPERTS):
        first = jnp.where(e == j, blk_start[j], first)
    dest = (first * EXPERT_TILE + rank).reshape(TOP_K, T // TOKEN_TILE, TOKEN_TILE)
    return dest.transpose(1, 0, 2).reshape(T // TOKEN_TILE, 1, TOP_K * TOKEN_TILE)


def _mixers(x, wt, counts0):
    B, S, _ = x.shape
    T = B * S
    xr, xi, qt, kx, vt = _proj(x, wt)
    f_out = _seq_dft(xr, xi, wt)
    a_out = _attention(qt, kx, vt)
    flat = lambda t: t.reshape((T,) + t.shape[2:])
    return _merge(flat(x), flat(f_out), flat(a_out), wt, counts0)


def kernel(x_prompt, x_sample, g_attn_norm, w_in, g_q_latent, w_uq, g_kv_latent, w_ukv, g_out_fourier, g_out_mla, w_out, g_ffn_norm, w_router_group, b_router_group, w_router_expert, b_router_expert, w_gate, w_up, w_down, g_final):
    assert g_attn_norm.shape[0] == 1, "single-layer configuration"
    w = _layer_weights(g_attn_norm[0], w_in[0], g_q_latent[0], w_uq[0], g_kv_latent[0], w_ukv[0],
                       g_out_fourier[0], g_out_mla[0], w_out[0], g_ffn_norm[0], w_router_group[0],
                       b_router_group[0], w_router_expert[0], b_router_expert[0], w_gate[0], w_up[0], w_down[0])
    assert x_prompt.shape[1] == x_sample.shape[1]
    w.update(_tables(x_prompt.shape[1]))

    counts = jnp.zeros((N_EXPERTS, LANES), F32)
    batches = []
    for x in (x_prompt, x_sample):
        h, hn3, route, w0, w1, counts = _mixers(x, w, counts)
        batches.append((x.shape, h, hn3, route, w0, w1))

    n_assign = sum(b[1].shape[0] for b in batches) * TOP_K
    blk_start, blk_end, block_expert, n_used, n_rows = _block_plan(counts, n_assign)
    dests = [_dest_rows(b[3], blk_start) for b in batches]
    xs = _dispatch(blk_end, n_used, jnp.concatenate(dests, axis=0), batches[0][2], batches[1][2], n_rows)
    y = _experts(block_expert, n_used, blk_end, xs, w)
    outs = []
    for (shape, h, _, _, w0, w1), dest in zip(batches, dests):
        outs.append(_combine(dest, h, w0, w1, g_final[None, :], y).reshape(shape))
    return tuple(outs)
```

```python
import functools
import math

import jax
import jax.numpy as jnp
from jax import lax
from jax.experimental import pallas as pl
from jax.experimental.pallas import tpu as pltpu

D_MODEL = 2048
FOURIER_WIDTH = 1024
N_FOURIER_GROUPS = 4
FOURIER_GROUP_DIM = 256
MLA_WIDTH = 1024
V_HEAD_DIM = 128
N_HEADS = 8
QK_NOPE_DIM = 128
QK_ROPE_DIM = 64
QK_HEAD_DIM = 192
Q_LORA_RANK = 512
KV_LORA_RANK = 256
ROPE_THETA = 10000.0
N_EXPERT_GROUPS = 4
EXPERTS_PER_GROUP = 8
N_EXPERTS = 32
TOP_K = 2
EXPERT_DIM = 512
EPS = 1e-6

LANES = 128
BF16_SUBLANES = 16
MXU_DEPTH = 256
QK_PAD = MXU_DEPTH
V_ROWS = V_HEAD_DIM + BF16_SUBLANES
NEG_LARGE = -1e30
ROW_TILES = D_MODEL // LANES
ROUTE_ROWS = 8

DFT_N1 = 64
DFT_N2 = 128

TOKEN_TILE = 512
ATTN_TQ = 1024
ATTN_TK = 512
ATTN_SUB = 8
ATTN_QBLOCKS = 8
DFT_ROWS = 16
EXPERT_TILE = 256
VMEM_LIMIT = 56 * 1024 * 1024

F32 = jnp.float32
BF16 = jnp.bfloat16


def _const_spec(shape):
    nd = len(shape)
    return pl.BlockSpec(shape, lambda *_: (0,) * nd, pipeline_mode=pl.Buffered(1))


def _rms(x, g):
    return x * lax.rsqrt(jnp.mean(x * x, axis=-1, keepdims=True) + EPS) * g


def _proj_kernel(x_ref, g_attn_ref, w_in_ref, cs_ref, g_q_ref, g_kv_ref, wqn_ref, wqr_ref,
                 wuk_ref, wuvt_ref, cos_t_ref, sin_t_ref, cos_p_ref, sin_p_ref,
                 xr_ref, xi_ref, qt_ref, kx_ref, vt_ref):
    tm = x_ref.shape[1]
    x = x_ref[0]
    u = _rms(x, g_attn_ref[...]).astype(BF16)
    z = jnp.dot(u, w_in_ref[...], preferred_element_type=F32)

    zf = z[:, :FOURIER_WIDTH].astype(BF16)
    xr, xi = [], []
    for g in range(N_FOURIER_GROUPS):
        zg = zf[:, g * FOURIER_GROUP_DIM:(g + 1) * FOURIER_GROUP_DIM]
        xg = jnp.dot(zg, cs_ref[...], preferred_element_type=F32)
        xr.append(xg[:, :FOURIER_GROUP_DIM])
        xi.append(xg[:, FOURIER_GROUP_DIM:])
    xr_ref[0] = jnp.concatenate(xr, axis=-1).astype(BF16)
    xi_ref[0] = jnp.concatenate(xi, axis=-1).astype(BF16)

    o = FOURIER_WIDTH
    q_lat = z[:, o:o + Q_LORA_RANK]
    o += Q_LORA_RANK
    kv_lat = z[:, o:o + KV_LORA_RANK]
    o += KV_LORA_RANK
    kr = z[:, o:o + LANES]
    kr_rot = z[:, o + LANES:o + 2 * LANES]

    qn = _rms(q_lat, g_q_ref[...]).astype(BF16)
    kvn = _rms(kv_lat, g_kv_ref[...]).astype(BF16)

    nt = (((1,), (1,)), ((), ()))
    qn_t = lax.dot_general(wqn_ref[...], qn, nt, preferred_element_type=F32)
    qr_t = lax.dot_general(wqr_ref[...], qn, nt, preferred_element_type=F32)
    cos_t = cos_t_ref[...]
    sin_t = sin_t_ref[...]
    zpad = jnp.zeros((QK_PAD - QK_HEAD_DIM, tm), BF16)
    half = QK_ROPE_DIM // 2
    for h in range(N_HEADS):
        r0 = h * QK_PAD
        qt_ref[0, 0, r0:r0 + QK_NOPE_DIM, :] = qn_t[h * QK_NOPE_DIM:(h + 1) * QK_NOPE_DIM].astype(BF16)
        qr = qr_t[h * QK_ROPE_DIM:(h + 1) * QK_ROPE_DIM]
        rot = jnp.concatenate([-qr[half:], qr[:half]], axis=0)
        roped = qr * cos_t + rot * sin_t
        qt_ref[0, 0, r0 + QK_NOPE_DIM:r0 + QK_HEAD_DIM, :] = roped.astype(BF16)
        qt_ref[0, 0, r0 + QK_HEAD_DIM:r0 + QK_PAD, :] = zpad

    k_nope = jnp.dot(kvn, wuk_ref[...], preferred_element_type=F32)
    k_rope = (kr * cos_p_ref[...] + kr_rot * sin_p_ref[...]).astype(BF16)
    for h in range(N_HEADS):
        kx_ref[0, h, :, :QK_NOPE_DIM] = k_nope[:, h * QK_NOPE_DIM:(h + 1) * QK_NOPE_DIM].astype(BF16)
        kx_ref[0, h, :, QK_NOPE_DIM:] = k_rope

    v_t = lax.dot_general(wuvt_ref[...], kvn, nt, preferred_element_type=F32)
    vt_ref[0, 0, :, :V_HEAD_DIM, :] = v_t.astype(BF16).reshape(N_HEADS, V_HEAD_DIM, tm)
    vt_ref[0, 0, :, V_HEAD_DIM:, :] = jnp.ones((N_HEADS, V_ROWS - V_HEAD_DIM, tm), BF16)


def _proj(x, w):
    B, S, _ = x.shape
    tm = ATTN_TK
    q_split = ATTN_TQ // tm
    grid = (B, S // tm)
    tok = lambda b, i: (b, i, 0)
    out_shape = [
        jax.ShapeDtypeStruct((B, S, FOURIER_WIDTH), BF16),
        jax.ShapeDtypeStruct((B, S, FOURIER_WIDTH), BF16),
        jax.ShapeDtypeStruct((B, S // ATTN_TQ, N_HEADS * QK_PAD, ATTN_TQ), BF16),
        jax.ShapeDtypeStruct((B, N_HEADS, S, QK_PAD), BF16),
        jax.ShapeDtypeStruct((B, S // tm, N_HEADS, V_ROWS, tm), BF16),
    ]
    in_specs = [
        pl.BlockSpec((1, tm, D_MODEL), tok),
        _const_spec((1, D_MODEL)),
        _const_spec(w['w_in'].shape),
        _const_spec(w['cs'].shape),
        _const_spec((1, Q_LORA_RANK)),
        _const_spec((1, KV_LORA_RANK)),
        _const_spec(w['wqn_t'].shape),
        _const_spec(w['wqr_t'].shape),
        _const_spec(w['wuk'].shape),
        _const_spec(w['wuv_t'].shape),
        pl.BlockSpec((QK_ROPE_DIM, tm), lambda b, i: (0, i)),
        pl.BlockSpec((QK_ROPE_DIM, tm), lambda b, i: (0, i)),
        pl.BlockSpec((tm, LANES), lambda b, i: (i, 0)),
        pl.BlockSpec((tm, LANES), lambda b, i: (i, 0)),
    ]
    out_specs = [
        pl.BlockSpec((1, tm, FOURIER_WIDTH), tok),
        pl.BlockSpec((1, tm, FOURIER_WIDTH), tok),
        pl.BlockSpec((1, 1, N_HEADS * QK_PAD, tm), lambda b, i: (b, i // q_split, 0, i % q_split)),
        pl.BlockSpec((1, N_HEADS, tm, QK_PAD), lambda b, i: (b, 0, i, 0)),
        pl.BlockSpec((1, 1, N_HEADS, V_ROWS, tm), lambda b, i: (b, i, 0, 0, 0)),
    ]
    return pl.pallas_call(
        _proj_kernel, grid=grid, in_specs=in_specs, out_specs=out_specs, out_shape=out_shape,
        compiler_params=pltpu.CompilerParams(
            dimension_semantics=("arbitrary", "arbitrary"), vmem_limit_bytes=VMEM_LIMIT),
        name="proj",
    )(x, w['g_attn'], w['w_in'], w['cs'], w['g_q'], w['g_kv'], w['wqn_t'], w['wqr_t'],
      w['wuk'], w['wuv_t'], w['cos_t'], w['sin_t'], w['cos_p'], w['sin_p'])


def _dft1_kernel(m1_ref, xr_ref, xi_ref, zr_ref, zi_ref):
    nb, c = xr_ref.shape[2], xr_ref.shape[3]
    xr = xr_ref[0].reshape(DFT_N1, nb * c)
    xi = xi_ref[0].reshape(DFT_N1, nb * c)
    xc = jnp.concatenate([xr, xi], axis=0)
    z = jnp.dot(m1_ref[...], xc, preferred_element_type=F32).astype(BF16)
    zr_ref[0] = z[:DFT_N1].reshape(DFT_N1, nb, c)
    zi_ref[0] = z[DFT_N1:].reshape(DFT_N1, nb, c)


def _dft2_kernel(g_ref, zr_ref, zi_ref, o_ref):
    kb, c = zr_ref.shape[1], zr_ref.shape[3]
    ys = []
    for j in range(kb):
        zc = jnp.concatenate([zr_ref[0, j], zi_ref[0, j]], axis=0)
        ys.append(jnp.dot(g_ref[j], zc, preferred_element_type=F32).astype(BF16))
    o_ref[0] = jnp.concatenate(ys, axis=-1).reshape(DFT_N2, kb, c)


def _seq_dft(xr, xi, w):
    B, S, C = xr.shape
    assert S == DFT_N1 * DFT_N2
    nb = DFT_ROWS
    xr4 = xr.reshape(B, DFT_N1, DFT_N2, C)
    xi4 = xi.reshape(B, DFT_N1, DFT_N2, C)
    rows = pl.BlockSpec((1, DFT_N1, nb, C), lambda b, j: (b, 0, j, 0))
    zr, zi = pl.pallas_call(
        _dft1_kernel, grid=(B, DFT_N2 // nb),
        in_specs=[_const_spec((2 * DFT_N1, 2 * DFT_N1)), rows, rows],
        out_specs=[rows, rows],
        out_shape=[jax.ShapeDtypeStruct((B, DFT_N1, DFT_N2, C), BF16)] * 2,
        compiler_params=pltpu.CompilerParams(
            dimension_semantics=("arbitrary", "arbitrary"), vmem_limit_bytes=VMEM_LIMIT),
        name="dft1",
    )(w['dft_m1'], xr4, xi4)
    kb = DFT_ROWS
    blk = pl.BlockSpec((1, kb, DFT_N2, C), lambda b, k: (b, k, 0, 0))
    y = pl.pallas_call(
        _dft2_kernel, grid=(B, DFT_N1 // kb),
        in_specs=[pl.BlockSpec((kb, DFT_N2, 2 * DFT_N2), lambda b, k: (k, 0, 0)), blk, blk],
        out_specs=pl.BlockSpec((1, DFT_N2, kb, C), lambda b, k: (b, 0, k, 0)),
        out_shape=jax.ShapeDtypeStruct((B, DFT_N2, DFT_N1, C), BF16),
        compiler_params=pltpu.CompilerParams(
            dimension_semantics=("arbitrary", "arbitrary"), vmem_limit_bytes=VMEM_LIMIT),
        name="dft2",
    )(w['dft_g'], zr, zi)
    return y.reshape(B, S, C)


def _attn_kernel(q_ref, k_ref, v_ref, o_ref, s_scr, acc_ref):
    tk = v_ref.shape[-1]
    nch = v_ref.shape[1]
    nq, tq = q_ref.shape[1], q_ref.shape[-1]
    groups = nch // ATTN_SUB

    def scores(f):
        j = jnp.minimum(f // nch, nq - 1)
        c = lax.rem(f, nch)
        k_c = k_ref[0, 0, pl.ds(pl.multiple_of(c * tk, tk), tk), :]
        return jnp.dot(k_c, q_ref[0, j], preferred_element_type=F32)

    def step(f, slot, m, s_max):
        s_next = scores(f + 1)
        s_scr[1 - slot] = s_next
        next_max = jnp.max(s_next, axis=0, keepdims=True)
        s = s_scr[slot]
        m_new = jnp.maximum(m, s_max)
        alpha = jnp.exp2(m - m_new)
        p = jnp.exp2(s - m_new)
        pv = jnp.dot(v_ref[0, lax.rem(f, nch), 0], p.astype(BF16), preferred_element_type=F32)
        acc_ref[...] = alpha * acc_ref[...] + pv
        return m_new, next_max

    acc_ref[...] = jnp.zeros(acc_ref.shape, F32)
    s_first = scores(0)
    s_scr[0] = s_first
    m0 = jnp.full((1, tq), NEG_LARGE, F32)

    def body(g, carry):
        m, s_max = carry
        for si in range(ATTN_SUB):
            m, s_max = step(g * ATTN_SUB + si, si % 2, m, s_max)
        block_done = lax.rem(g, groups) == groups - 1

        @pl.when(block_done)
        def _():
            j = g // groups
            o_t = acc_ref[:V_HEAD_DIM] * (1.0 / acc_ref[V_HEAD_DIM:V_HEAD_DIM + 1])
            o_ref[0, pl.ds(pl.multiple_of(j * tq, tq), tq), :] = o_t.T.astype(BF16)
            acc_ref[...] = jnp.zeros(acc_ref.shape, F32)

        return jnp.where(block_done, m0, m), s_max

    lax.fori_loop(0, nq * groups, body, (m0, jnp.max(s_first, axis=0, keepdims=True)))


def _attention(qt, kx, vt):
    B, nqb, _, tq = qt.shape
    S = nqb * tq
    nch, tk = vt.shape[1], vt.shape[-1]
    nq = ATTN_QBLOCKS
    assert ATTN_SUB % 2 == 0 and nch % ATTN_SUB == 0 and nqb % nq == 0
    return pl.pallas_call(
        _attn_kernel, grid=(B, N_HEADS, nqb // nq),
        in_specs=[pl.BlockSpec((1, nq, QK_PAD, tq), lambda b, h, i: (b, i, h, 0)),
                  pl.BlockSpec((1, 1, S, QK_PAD), lambda b, h, i: (b, h, 0, 0)),
                  pl.BlockSpec((1, nch, 1, V_ROWS, tk), lambda b, h, i: (b, 0, h, 0, 0))],
        out_specs=pl.BlockSpec((1, nq * tq, V_HEAD_DIM), lambda b, h, i: (b, i, h)),
        out_shape=jax.ShapeDtypeStruct((B, S, MLA_WIDTH), BF16),
        scratch_shapes=[pltpu.VMEM((2, tk, tq), F32), pltpu.VMEM((V_ROWS, tq), F32)],
        compiler_params=pltpu.CompilerParams(
            dimension_semantics=("arbitrary", "arbitrary", "arbitrary"), vmem_limit_bytes=VMEM_LIMIT),
        name="attn",
    )(qt, kx, vt)


def _merge_kernel(x_ref, f_ref, a_ref, g_f_ref, g_a_ref, w_out_ref, g_ffn_ref, wr_ref, b_r_ref, tri_ref, cnt0_ref,
                  h_ref, hn_ref, route_ref, w0_ref, w1_ref, cnt_ref, carry_ref):
    tm = x_ref.shape[0]

    @pl.when(pl.program_id(0) == 0)
    def _():
        carry_ref[...] = cnt0_ref[...]

    fn = _rms(f_ref[...].astype(F32), g_f_ref[...]).astype(BF16)
    an = _rms(a_ref[...].astype(F32), g_a_ref[...]).astype(BF16)
    merged = jnp.concatenate([fn, an], axis=-1)
    h = x_ref[...] + jnp.dot(merged, w_out_ref[...], preferred_element_type=F32)
    h_ref[...] = h
    hn = _rms(h, g_ffn_ref[...])
    hn_ref[...] = hn.astype(BF16).reshape(tm, ROW_TILES, LANES)

    hn_hi = hn.astype(BF16)
    hn_lo = (hn - hn_hi.astype(F32)).astype(BF16)
    nt = (((1,), (1,)), ((), ()))
    p_t = (lax.dot_general(wr_ref[...], hn_hi, nt, preferred_element_type=F32)
           + lax.dot_general(wr_ref[...], hn_lo, nt, preferred_element_type=F32))
    reps = tm // LANES
    logits = p_t[:LANES] + p_t[LANES:] + jnp.tile(b_r_ref[...], (1, reps))

    row = lax.broadcasted_iota(jnp.int32, (EXPERTS_PER_GROUP, tm), 0)
    neg = jnp.float32(NEG_LARGE)
    gl = logits[:EXPERTS_PER_GROUP]
    gmax = jnp.max(gl, axis=0, keepdims=True)
    g_sel = jnp.min(jnp.where(gl == gmax, row, EXPERTS_PER_GROUP), axis=0, keepdims=True)
    g_p = 1.0 / jnp.sum(jnp.exp(gl - gmax), axis=0, keepdims=True)
    el = logits[EXPERTS_PER_GROUP:2 * EXPERTS_PER_GROUP]
    for g in range(1, N_EXPERT_GROUPS):
        el = jnp.where(g_sel == g, logits[(g + 1) * EXPERTS_PER_GROUP:(g + 2) * EXPERTS_PER_GROUP], el)
    m1 = jnp.max(el, axis=0, keepdims=True)
    i1 = jnp.min(jnp.where(el == m1, row, EXPERTS_PER_GROUP), axis=0, keepdims=True)
    el2 = jnp.where(row == i1, neg, el)
    m2 = jnp.max(el2, axis=0, keepdims=True)
    i2 = jnp.min(jnp.where(el2 == m2, row, EXPERTS_PER_GROUP), axis=0, keepdims=True)
    t = jnp.exp(m2 - m1)
    inv = 1.0 / (1.0 + t)
    w0 = g_p * inv
    w1 = g_p * t * inv
    e0 = g_sel * EXPERTS_PER_GROUP + i1
    e1 = g_sel * EXPERTS_PER_GROUP + i2

    row_e = lax.broadcasted_iota(jnp.int32, (N_EXPERTS, tm), 0)
    oh0 = jnp.where(row_e == e0, 1.0, 0.0)
    oh1 = jnp.where(row_e == e1, 1.0, 0.0)
    oh = oh0 + oh1
    before = (jnp.dot(oh.astype(BF16), tri_ref[...], preferred_element_type=F32)
              + jnp.tile(carry_ref[...], (1, reps)))
    r0 = jnp.sum(before * oh0, axis=0, keepdims=True)
    r1 = jnp.sum(before * oh1, axis=0, keepdims=True)
    total = carry_ref[...] + jnp.broadcast_to(jnp.sum(oh, axis=1, keepdims=True), carry_ref.shape)
    carry_ref[...] = total
    cnt_ref[...] = total

    row_r = lax.broadcasted_iota(jnp.int32, (ROUTE_ROWS, tm), 0)
    route_ref[...] = jnp.where(row_r == 0, e0.astype(F32), jnp.where(row_r == 1, e1.astype(F32),
                               jnp.where(row_r == 2, r0, jnp.where(row_r == 3, r1, 0.0))))
    row_l = lax.broadcasted_iota(jnp.int32, (LANES, tm), 0)
    w_rows = jnp.where(row_l == 0, w0, jnp.where(row_l == 1, w1, 0.0)).T
    w0_ref[...] = jnp.broadcast_to(w_rows[:, 0:1], (tm, LANES))
    w1_ref[...] = jnp.broadcast_to(w_rows[:, 1:2], (tm, LANES))


def _merge(x, f_out, a_out, w, counts0):
    T = x.shape[0]
    tm = TOKEN_TILE
    n = T // tm
    cur = lambda t: (t, 0)
    out_shape = [
        jax.ShapeDtypeStruct((T, D_MODEL), F32),
        jax.ShapeDtypeStruct((T, ROW_TILES, LANES), BF16),
        jax.ShapeDtypeStruct((ROUTE_ROWS, T), F32),
        jax.ShapeDtypeStruct((T, LANES), F32),
        jax.ShapeDtypeStruct((T, LANES), F32),
        jax.ShapeDtypeStruct((N_EXPERTS, LANES), F32),
    ]
    in_specs = [
        pl.BlockSpec((tm, D_MODEL), cur),
        pl.BlockSpec((tm, FOURIER_WIDTH), cur),
        pl.BlockSpec((tm, MLA_WIDTH), cur),
        _const_spec((1, FOURIER_WIDTH)),
        _const_spec((1, MLA_WIDTH)),
        _const_spec((D_MODEL, D_MODEL)),
        _const_spec((1, D_MODEL)),
        _const_spec((2 * LANES, D_MODEL)),
        _const_spec((LANES, LANES)),
        _const_spec((tm, tm)),
        _const_spec((N_EXPERTS, LANES)),
    ]
    out_specs = [
        pl.BlockSpec((tm, D_MODEL), cur),
        pl.BlockSpec((tm, ROW_TILES, LANES), lambda t: (t, 0, 0)),
        pl.BlockSpec((ROUTE_ROWS, tm), lambda t: (0, t)),
        pl.BlockSpec((tm, LANES), cur),
        pl.BlockSpec((tm, LANES), cur),
        pl.BlockSpec((N_EXPERTS, LANES), lambda t: (0, 0)),
    ]
    tri = (lax.broadcasted_iota(jnp.int32, (tm, tm), 0) < lax.broadcasted_iota(jnp.int32, (tm, tm), 1)).astype(BF16)
    return pl.pallas_call(
        _merge_kernel, grid=(n,), in_specs=in_specs, out_specs=out_specs, out_shape=out_shape,
        scratch_shapes=[pltpu.VMEM((N_EXPERTS, LANES), F32)],
        compiler_params=pltpu.CompilerParams(dimension_semantics=("arbitrary",), vmem_limit_bytes=VMEM_LIMIT),
        name="merge",
    )(x, f_out, a_out, w['g_f'], w['g_a'], w['w_out'], w['g_ffn'], w['wr_t'], w['b_rt'], tri, counts0)


def _dispatch_kernel(blk_end_ref, nused_ref, dest_ref, hn_a_ref, hn_b_ref, xs_ref, zero_buf, sem, zsem, *, n_a):
    i = pl.program_id(0)
    tm = hn_a_ref.shape[0]
    nb = xs_ref.shape[0] // EXPERT_TILE

    @pl.when(i == 0)
    def _():
        zero_buf[...] = jnp.zeros(zero_buf.shape, BF16)

        def zero_block(j):
            return pltpu.make_async_copy(zero_buf, xs_ref.at[pl.ds(j * EXPERT_TILE, EXPERT_TILE)], zsem)

        def each_group(fn):
            for e in range(N_EXPERTS):
                end = blk_end_ref[e]
                start = blk_end_ref[e - 1] if e else 0

                @pl.when(end > start)
                def _():
                    fn(end - 1)

        def each_tail(fn):
            def body(j, carry):
                fn(j)
                return carry

            lax.fori_loop(nused_ref[0], nb, body, 0)

        each_group(lambda j: zero_block(j).start())
        each_tail(lambda j: zero_block(j).start())
        each_group(lambda j: zero_block(j).wait())
        each_tail(lambda j: zero_block(j).wait())

    def scatter(src_ref):
        def body(r, carry):
            for k in range(TOP_K):
                d = dest_ref[0, 0, k * tm + r]
                pltpu.make_async_copy(src_ref.at[r], xs_ref.at[d], sem).start(priority=k)
            return carry

        lax.fori_loop(0, tm, body, 0)
        for _ in range(TOP_K):
            pltpu.make_async_copy(src_ref, xs_ref.at[pl.ds(0, tm)], sem).wait()

    @pl.when(i < n_a)
    def _():
        scatter(hn_a_ref)

    @pl.when(i >= n_a)
    def _():
        scatter(hn_b_ref)


def _dispatch(blk_end, n_used, dest, hn_a, hn_b, n_rows):
    tm = TOKEN_TILE
    n_a, n_b = hn_a.shape[0] // tm, hn_b.shape[0] // tm
    tile = (tm, ROW_TILES, LANES)
    grid_spec = pltpu.PrefetchScalarGridSpec(
        num_scalar_prefetch=2, grid=(n_a + n_b,),
        in_specs=[pl.BlockSpec((1, 1, TOP_K * tm), lambda i, be, nu: (i, 0, 0), memory_space=pltpu.SMEM),
                  pl.BlockSpec(tile, lambda i, be, nu: (jnp.minimum(i, n_a - 1), 0, 0)),
                  pl.BlockSpec(tile, lambda i, be, nu: (jnp.maximum(i - n_a, 0), 0, 0))],
        out_specs=pl.BlockSpec(memory_space=pl.ANY),
        scratch_shapes=[pltpu.VMEM((EXPERT_TILE, ROW_TILES, LANES), BF16),
                        pltpu.SemaphoreType.DMA(()), pltpu.SemaphoreType.DMA(())],
    )
    return pl.pallas_call(
        functools.partial(_dispatch_kernel, n_a=n_a), grid_spec=grid_spec,
        out_shape=jax.ShapeDtypeStruct((n_rows, ROW_TILES, LANES), BF16),
        compiler_params=pltpu.CompilerParams(dimension_semantics=("arbitrary",), vmem_limit_bytes=VMEM_LIMIT),
        name="dispatch",
    )(blk_end, n_used, dest, hn_a, hn_b)


def _expert_kernel(be_ref, nused_ref, blk_end_ref, xs_ref, wg_ref, wu_ref, wd_ref, y_ref,
                   wg_f, wu_f, wd_f, wgu_s, wd_s, group_ref, sems):
    i = pl.program_id(0)
    e = be_ref[i]

    def weight_copies(expert, slot):
        return [pltpu.make_async_copy(src.at[expert], dst.at[slot], sems.at[slot])
                for src, dst in ((wg_ref, wg_f), (wu_ref, wu_f), (wd_ref, wd_f))]

    @pl.when(i == 0)
    def _():
        group_ref[0] = 0
        for cp in weight_copies(e, 0):
            cp.start()

    @pl.when((i == 0) | (e != be_ref[jnp.maximum(i - 1, 0)]))
    def _():
        slot = lax.rem(group_ref[0], 2)
        for cp in weight_copies(e, slot):
            cp.wait()
        wgu_s[:, :EXPERT_DIM] = wg_f[slot].astype(BF16)
        wgu_s[:, EXPERT_DIM:] = wu_f[slot].astype(BF16)
        wd_s[...] = wd_f[slot].astype(BF16)
        nxt = blk_end_ref[e]

        @pl.when(nxt < nused_ref[0])
        def _():
            for cp in weight_copies(be_ref[nxt], 1 - slot):
                cp.start(priority=1)

        group_ref[0] = group_ref[0] + 1

    @pl.when(i < nused_ref[0])
    def _():
        tm = xs_ref.shape[0]
        x = xs_ref[...].reshape(tm, D_MODEL)
        gu = jnp.dot(x, wgu_s[...], preferred_element_type=F32)
        g = gu[:, :EXPERT_DIM]
        u = gu[:, EXPERT_DIM:]
        hmid = (g * jax.nn.sigmoid(g) * u).astype(BF16)
        y = jnp.dot(hmid, wd_s[...], preferred_element_type=F32)
        y_ref[...] = y.astype(BF16).reshape(tm, ROW_TILES, LANES)

    @pl.when(i >= nused_ref[0])
    def _():
        y_ref[...] = jnp.zeros(y_ref.shape, BF16)


def _experts(block_expert, n_used, blk_end, xs, w):
    n_rows = xs.shape[0]
    tm = EXPERT_TILE
    nb = n_rows // tm
    row = lambda i, be, nu, bend: (jnp.minimum(i, nu[0] - 1), 0, 0)
    out_row = lambda i, be, nu, bend: (i, 0, 0)
    hbm = pl.BlockSpec(memory_space=pl.ANY)
    grid_spec = pltpu.PrefetchScalarGridSpec(
        num_scalar_prefetch=3, grid=(nb,),
        in_specs=[pl.BlockSpec((tm, ROW_TILES, LANES), row), hbm, hbm, hbm],
        out_specs=pl.BlockSpec((tm, ROW_TILES, LANES), out_row),
        scratch_shapes=[pltpu.VMEM((2, D_MODEL, EXPERT_DIM), F32), pltpu.VMEM((2, D_MODEL, EXPERT_DIM), F32),
                        pltpu.VMEM((2, EXPERT_DIM, D_MODEL), F32),
                        pltpu.VMEM((D_MODEL, 2 * EXPERT_DIM), BF16), pltpu.VMEM((EXPERT_DIM, D_MODEL), BF16),
                        pltpu.SMEM((1,), jnp.int32), pltpu.SemaphoreType.DMA((2,))],
    )
    return pl.pallas_call(
        _expert_kernel, grid_spec=grid_spec,
        out_shape=jax.ShapeDtypeStruct((n_rows, ROW_TILES, LANES), BF16),
        compiler_params=pltpu.CompilerParams(dimension_semantics=("arbitrary",), vmem_limit_bytes=VMEM_LIMIT),
        name="experts",
    )(block_expert, n_used, blk_end, xs, w['w_gate'], w['w_up'], w['w_down'])


def _combine_kernel(dest_ref, dest_next_ref, h_ref, w0_ref, w1_ref, g_ref, y_ref, o_ref, ybuf, sems, *, n_steps):
    tm = h_ref.shape[0]
    i = pl.program_id(0)
    slot = lax.rem(i, 2)

    def start_gather(d_ref, s):
        def body(r, carry):
            for k in range(TOP_K):
                d = d_ref[0, 0, k * tm + r]
                pltpu.make_async_copy(y_ref.at[d], ybuf.at[s, k, r], sems.at[s]).start(priority=k)
            return carry

        lax.fori_loop(0, tm, body, 0)

    @pl.when(i == 0)
    def _():
        start_gather(dest_ref, 0)

    @pl.when(i + 1 < n_steps)
    def _():
        start_gather(dest_next_ref, 1 - slot)

    for k in range(TOP_K):
        pltpu.make_async_copy(y_ref.at[pl.ds(0, tm)], ybuf.at[slot, k], sems.at[slot]).wait()

    y0 = ybuf[slot, 0].reshape(tm, D_MODEL)
    y1 = ybuf[slot, 1].reshape(tm, D_MODEL)
    w0 = w0_ref[...]
    w1 = w1_ref[...]
    cols = []
    for j in range(ROW_TILES):
        sl = slice(j * LANES, (j + 1) * LANES)
        cols.append(h_ref[:, sl] + w0 * y0[:, sl].astype(F32) + w1 * y1[:, sl].astype(F32))
    hs = jnp.concatenate(cols, axis=-1)
    o_ref[...] = _rms(hs, g_ref[...])


def _combine(dest, h, w0, w1, g_final, y):
    T = h.shape[0]
    tm = TOKEN_TILE
    n = T // tm
    dest3 = dest
    return pl.pallas_call(
        functools.partial(_combine_kernel, n_steps=n), grid=(n,),
        in_specs=[pl.BlockSpec((1, 1, TOP_K * tm), lambda i: (i, 0, 0), memory_space=pltpu.SMEM),
                  pl.BlockSpec((1, 1, TOP_K * tm), lambda i: (jnp.minimum(i + 1, n - 1), 0, 0),
                               memory_space=pltpu.SMEM),
                  pl.BlockSpec((tm, D_MODEL), lambda i: (i, 0)),
                  pl.BlockSpec((tm, LANES), lambda i: (i, 0)),
                  pl.BlockSpec((tm, LANES), lambda i: (i, 0)),
                  _const_spec((1, D_MODEL)),
                  pl.BlockSpec(memory_space=pl.ANY)],
        out_specs=pl.BlockSpec((tm, D_MODEL), lambda i: (i, 0)),
        out_shape=jax.ShapeDtypeStruct((T, D_MODEL), F32),
        scratch_shapes=[pltpu.VMEM((2, TOP_K, tm, ROW_TILES, LANES), BF16), pltpu.SemaphoreType.DMA((2,))],
        compiler_params=pltpu.CompilerParams(dimension_semantics=("arbitrary",), vmem_limit_bytes=VMEM_LIMIT),
        name="combine",
    )(dest3, dest3, h, w0, w1, g_final, y)


def _tables(S):
    pos = jnp.arange(S, dtype=F32)
    inv_freq = 1.0 / (ROPE_THETA ** (jnp.arange(0, QK_ROPE_DIM, 2, dtype=F32) / QK_ROPE_DIM))
    ang = pos[:, None] * inv_freq[None, :]
    cos, sin = jnp.cos(ang), jnp.sin(ang)
    cos2 = jnp.concatenate([cos, cos], axis=-1)
    sin2 = jnp.concatenate([sin, sin], axis=-1)
    zp = jnp.zeros((S, LANES - QK_ROPE_DIM), F32)
    t = {
        'cos_t': cos2.T, 'sin_t': sin2.T,
        'cos_p': jnp.concatenate([cos2, zp], axis=-1), 'sin_p': jnp.concatenate([sin2, zp], axis=-1),
    }

    def phase(rows, cols, n):
        m = (rows[:, None] * cols[None, :]) % n
        a = m.astype(F32) * jnp.float32(2.0 * math.pi / n)
        return jnp.cos(a), jnp.sin(a)

    i1 = jnp.arange(DFT_N1, dtype=jnp.int32)
    c1, s1 = phase(i1, i1, DFT_N1)
    t['dft_m1'] = jnp.concatenate(
        [jnp.concatenate([c1, s1], axis=1), jnp.concatenate([-s1, c1], axis=1)], axis=0).astype(BF16)
    k = jnp.arange(S, dtype=jnp.int32)
    n2 = jnp.arange(DFT_N2, dtype=jnp.int32)
    cg, sg = phase(k, n2, S)
    g = jnp.concatenate([cg, sg], axis=1) * jnp.float32(S ** -0.5)
    t['dft_g'] = g.reshape(DFT_N2, DFT_N1, 2 * DFT_N2).transpose(1, 0, 2).astype(BF16)
    return t


def _layer_weights(g_attn_norm, w_in, g_q_latent, w_uq, g_kv_latent, w_ukv, g_out_fourier, g_out_mla, w_out,
                   g_ffn_norm, w_router_group, b_router_group, w_router_expert, b_router_expert,
                   w_gate, w_up, w_down):
    half = QK_ROPE_DIM // 2
    s3 = FOURIER_WIDTH + Q_LORA_RANK + KV_LORA_RANK
    w_kr = w_in[:, s3:s3 + QK_ROPE_DIM]
    w_kr_rot = jnp.concatenate([-w_kr[:, half:], w_kr[:, :half]], axis=1)
    zc = jnp.zeros((D_MODEL, LANES - QK_ROPE_DIM), F32)
    w_in_ext = jnp.concatenate([w_in[:, :s3], w_kr, zc, w_kr_rot, zc], axis=1)

    c = jnp.arange(FOURIER_GROUP_DIM, dtype=jnp.int32)
    m = (c[:, None] * c[None, :]) % FOURIER_GROUP_DIM
    a = m.astype(F32) * jnp.float32(2.0 * math.pi / FOURIER_GROUP_DIM)
    cs = jnp.concatenate([jnp.cos(a), -jnp.sin(a)], axis=1) * jnp.float32(FOURIER_GROUP_DIM ** -0.5)

    qscale = jnp.float32(QK_HEAD_DIM ** -0.5 * math.log2(math.e))
    wq = (w_uq * qscale).reshape(Q_LORA_RANK, N_HEADS, QK_HEAD_DIM)
    wq_n = wq[:, :, :QK_NOPE_DIM]
    wq_r = wq[:, :, QK_NOPE_DIM:]
    to_t = lambda t: t.reshape(Q_LORA_RANK, -1).T
    wkv = w_ukv.reshape(KV_LORA_RANK, N_HEADS, QK_NOPE_DIM + V_HEAD_DIM)
    gpad = EXPERTS_PER_GROUP - N_EXPERT_GROUPS
    tail = LANES - EXPERTS_PER_GROUP - N_EXPERTS
    w_r = jnp.concatenate([w_router_group.T, jnp.zeros((gpad, D_MODEL), F32), w_router_expert.T,
                           jnp.zeros((tail, D_MODEL), F32)], axis=0)
    wr_hi = w_r.astype(BF16)
    b_r = jnp.concatenate([b_router_group, jnp.full((gpad,), NEG_LARGE, F32), b_router_expert,
                           jnp.zeros((tail,), F32)])
    return {
        'g_attn': g_attn_norm[None, :], 'w_in': w_in_ext.astype(BF16), 'cs': cs.astype(BF16),
        'g_q': g_q_latent[None, :], 'g_kv': g_kv_latent[None, :],
        'wqn_t': to_t(wq_n).astype(BF16), 'wqr_t': to_t(wq_r).astype(BF16),
        'wuk': wkv[:, :, :QK_NOPE_DIM].reshape(KV_LORA_RANK, -1).astype(BF16),
        'wuv_t': wkv[:, :, QK_NOPE_DIM:].reshape(KV_LORA_RANK, -1).T.astype(BF16),
        'g_f': g_out_fourier[None, :], 'g_a': g_out_mla[None, :], 'w_out': w_out.astype(BF16),
        'g_ffn': g_ffn_norm[None, :],
        'wr_t': jnp.concatenate([wr_hi, (w_r - wr_hi.astype(F32)).astype(BF16)], axis=0),
        'b_rt': jnp.broadcast_to(b_r[:, None], (LANES, LANES)),
        'w_gate': w_gate, 'w_up': w_up, 'w_down': w_down,
    }


def _block_plan(counts, n_assign):
    counts = counts[:, 0].astype(jnp.int32)
    nblk = (counts + EXPERT_TILE - 1) // EXPERT_TILE
    blk_end = jnp.cumsum(nblk)
    blk_start = blk_end - nblk
    n_blocks = (n_assign + N_EXPERTS * (EXPERT_TILE - 1) + EXPERT_TILE - 1) // EXPERT_TILE
    n_used = blk_end[-1]
    j = jnp.minimum(jnp.arange(n_blocks, dtype=jnp.int32), n_used - 1)
    block_expert = jnp.minimum(jnp.sum((blk_end[None, :] <= j[:, None]).astype(jnp.int32), axis=1), N_EXPERTS - 1)
    return blk_start, blk_end, block_expert, n_used[None], n_blocks * EXPERT_TILE


def _dest_rows(route, blk_start):
    T = route.shape[1]
    e = route[:TOP_K].astype(jnp.int32)
    rank = route[TOP_K:2 * TOP_K].astype(jnp.int32)
    first = jnp.zeros_like(e)
    for j in range(N_EXPERTS):
        first = jnp.where(e == j, blk_start[j], first)
    dest = (first * EXPERT_TILE + rank).reshape(TOP_K, T // TOKEN_TILE, TOKEN_TILE)
    return dest.transpose(1, 0, 2).reshape(T // TOKEN_TILE, 1, TOP_K * TOKEN_TILE)


def _mixers(x, wt, counts0):
    B, S, _ = x.shape
    T = B * S
    xr, xi, qt, kx, vt = _proj(x, wt)
    f_out = _seq_dft(xr, xi, wt)
    a_out = _attention(qt, kx, vt)
    flat = lambda t: t.reshape((T,) + t.shape[2:])
    return _merge(flat(x), flat(f_out), flat(a_out), wt, counts0)


def kernel(x_prompt, x_sample, g_attn_norm, w_in, g_q_latent, w_uq, g_kv_latent, w_ukv, g_out_fourier, g_out_mla, w_out, g_ffn_norm, w_router_group, b_router_group, w_router_expert, b_router_expert, w_gate, w_up, w_down, g_final):
    assert g_attn_norm.shape[0] == 1, "single-layer configuration"
    w = _layer_weights(g_attn_norm[0], w_in[0], g_q_latent[0], w_uq[0], g_kv_latent[0], w_ukv[0],
                       g_out_fourier[0], g_out_mla[0], w_out[0], g_ffn_norm[0], w_router_group[0],
                       b_router_group[0], w_router_expert[0], b_router_expert[0], w_gate[0], w_up[0], w_down[0])
    assert x_prompt.shape[1] == x_sample.shape[1]
    w.update(_tables(x_prompt.shape[1]))

    counts = jnp.zeros((N_EXPERTS, LANES), F32)
    batches = []
    for x in (x_prompt, x_sample):
        h, hn3, route, w0, w1, counts = _mixers(x, w, counts)
        batches.append((x.shape, h, hn3, route, w0, w1))

    n_assign = sum(b[1].shape[0] for b in batches) * TOP_K
    blk_start, blk_end, block_expert, n_used, n_rows = _block_plan(counts, n_assign)
    dests = [_dest_rows(b[3], blk_start) for b in batches]
    xs = _dispatch(blk_end, n_used, jnp.concatenate(dests, axis=0), batches[0][2], batches[1][2], n_rows)
    y = _experts(block_expert, n_used, blk_end, xs, w)
    outs = []
    for (shape, h, _, _, w0, w1), dest in zip(batches, dests):
        outs.append(_combine(dest, h, w0, w1, g_final[None, :], y).reshape(shape))
    return tuple(outs)
```

```python
import functools
import math

import jax
import jax.numpy as jnp
from jax import lax
from jax.experimental import pallas as pl
from jax.experimental.pallas import tpu as pltpu

D_MODEL = 2048
FOURIER_WIDTH = 1024
N_FOURIER_GROUPS = 4
FOURIER_GROUP_DIM = 256
MLA_WIDTH = 1024
V_HEAD_DIM = 128
N_HEADS = 8
QK_NOPE_DIM = 128
QK_ROPE_DIM = 64
QK_HEAD_DIM = 192
Q_LORA_RANK = 512
KV_LORA_RANK = 256
ROPE_THETA = 10000.0
N_EXPERT_GROUPS = 4
EXPERTS_PER_GROUP = 8
N_EXPERTS = 32
TOP_K = 2
EXPERT_DIM = 512
EPS = 1e-6

LANES = 128
BF16_SUBLANES = 16
MXU_DEPTH = 256
QK_PAD = MXU_DEPTH
V_ROWS = V_HEAD_DIM + BF16_SUBLANES
NEG_LARGE = -1e30
ROW_TILES = D_MODEL // LANES
ROUTE_ROWS = 8

DFT_N1 = 64
DFT_N2 = 128

TOKEN_TILE = 512
ATTN_TQ = 1024
ATTN_TK = 512
ATTN_SUB = 8
ATTN_QBLOCKS = 8
DFT_ROWS = 16
EXPERT_TILE = 256
VMEM_LIMIT = 56 * 1024 * 1024

F32 = jnp.float32
BF16 = jnp.bfloat16


def _const_spec(shape):
    nd = len(shape)
    return pl.BlockSpec(shape, lambda *_: (0,) * nd, pipeline_mode=pl.Buffered(1))


def _rms(x, g):
    return x * lax.rsqrt(jnp.mean(x * x, axis=-1, keepdims=True) + EPS) * g


def _proj_kernel(x_ref, g_attn_ref, w_in_ref, cs_ref, g_q_ref, g_kv_ref, wqn_ref, wqr_ref,
                 wuk_ref, wuvt_ref, cos_t_ref, sin_t_ref, cos_p_ref, sin_p_ref,
                 xr_ref, xi_ref, qt_ref, kx_ref, vt_ref):
    tm = x_ref.shape[1]
    x = x_ref[0]
    u = _rms(x, g_attn_ref[...]).astype(BF16)
    z = jnp.dot(u, w_in_ref[...], preferred_element_type=F32)

    zf = z[:, :FOURIER_WIDTH].astype(BF16)
    xr, xi = [], []
    for g in range(N_FOURIER_GROUPS):
        zg = zf[:, g * FOURIER_GROUP_DIM:(g + 1) * FOURIER_GROUP_DIM]
        xg = jnp.dot(zg, cs_ref[...], preferred_element_type=F32)
        xr.append(xg[:, :FOURIER_GROUP_DIM])
        xi.append(xg[:, FOURIER_GROUP_DIM:])
    xr_ref[0] = jnp.concatenate(xr, axis=-1).astype(BF16)
    xi_ref[0] = jnp.concatenate(xi, axis=-1).astype(BF16)

    o = FOURIER_WIDTH
    q_lat = z[:, o:o + Q_LORA_RANK]
    o += Q_LORA_RANK
    kv_lat = z[:, o:o + KV_LORA_RANK]
    o += KV_LORA_RANK
    kr = z[:, o:o + LANES]
    kr_rot = z[:, o + LANES:o + 2 * LANES]

    qn = _rms(q_lat, g_q_ref[...]).astype(BF16)
    kvn = _rms(kv_lat, g_kv_ref[...]).astype(BF16)

    nt = (((1,), (1,)), ((), ()))
    qn_t = lax.dot_general(wqn_ref[...], qn, nt, preferred_element_type=F32)
    qr_t = lax.dot_general(wqr_ref[...], qn, nt, preferred_element_type=F32)
    cos_t = cos_t_ref[...]
    sin_t = sin_t_ref[...]
    zpad = jnp.zeros((QK_PAD - QK_HEAD_DIM, tm), BF16)
    half = QK_ROPE_DIM // 2
    for h in range(N_HEADS):
        r0 = h * QK_PAD
        qt_ref[0, 0, r0:r0 + QK_NOPE_DIM, :] = qn_t[h * QK_NOPE_DIM:(h + 1) * QK_NOPE_DIM].astype(BF16)
        qr = qr_t[h * QK_ROPE_DIM:(h + 1) * QK_ROPE_DIM]
        rot = jnp.concatenate([-qr[half:], qr[:half]], axis=0)
        roped = qr * cos_t + rot * sin_t
        qt_ref[0, 0, r0 + QK_NOPE_DIM:r0 + QK_HEAD_DIM, :] = roped.astype(BF16)
        qt_ref[0, 0, r0 + QK_HEAD_DIM:r0 + QK_PAD, :] = zpad

    k_nope = jnp.dot(kvn, wuk_ref[...], preferred_element_type=F32)
    k_rope = (kr * cos_p_ref[...] + kr_rot * sin_p_ref[...]).astype(BF16)
    for h in range(N_HEADS):
        kx_ref[0, h, :, :QK_NOPE_DIM] = k_nope[:, h * QK_NOPE_DIM:(h + 1) * QK_NOPE_DIM].astype(BF16)
        kx_ref[0, h, :, QK_NOPE_DIM:] = k_rope

    v_t = lax.dot_general(wuvt_ref[...], kvn, nt, preferred_element_type=F32)
    vt_ref[0, 0, :, :V_HEAD_DIM, :] = v_t.astype(BF16).reshape(N_HEADS, V_HEAD_DIM, tm)
    vt_ref[0, 0, :, V_HEAD_DIM:, :] = jnp.ones((N_HEADS, V_ROWS - V_HEAD_DIM, tm), BF16)


def _proj(x, w):
    B, S, _ = x.shape
    tm = ATTN_TK
    q_split = ATTN_TQ // tm
    grid = (B, S // tm)
    tok = lambda b, i: (b, i, 0)
    out_shape = [
        jax.ShapeDtypeStruct((B, S, FOURIER_WIDTH), BF16),
        jax.ShapeDtypeStruct((B, S, FOURIER_WIDTH), BF16),
        jax.ShapeDtypeStruct((B, S // ATTN_TQ, N_HEADS * QK_PAD, ATTN_TQ), BF16),
        jax.ShapeDtypeStruct((B, N_HEADS, S, QK_PAD), BF16),
        jax.ShapeDtypeStruct((B, S // tm, N_HEADS, V_ROWS, tm), BF16),
    ]
    in_specs = [
        pl.BlockSpec((1, tm, D_MODEL), tok),
        _const_spec((1, D_MODEL)),
        _const_spec(w['w_in'].shape),
        _const_spec(w['cs'].shape),
        _const_spec((1, Q_LORA_RANK)),
        _const_spec((1, KV_LORA_RANK)),
        _const_spec(w['wqn_t'].shape),
        _const_spec(w['wqr_t'].shape),
        _const_spec(w['wuk'].shape),
        _const_spec(w['wuv_t'].shape),
        pl.BlockSpec((QK_ROPE_DIM, tm), lambda b, i: (0, i)),
        pl.BlockSpec((QK_ROPE_DIM, tm), lambda b, i: (0, i)),
        pl.BlockSpec((tm, LANES), lambda b, i: (i, 0)),
        pl.BlockSpec((tm, LANES), lambda b, i: (i, 0)),
    ]
    out_specs = [
        pl.BlockSpec((1, tm, FOURIER_WIDTH), tok),
        pl.BlockSpec((1, tm, FOURIER_WIDTH), tok),
        pl.BlockSpec((1, 1, N_HEADS * QK_PAD, tm), lambda b, i: (b, i // q_split, 0, i % q_split)),
        pl.BlockSpec((1, N_HEADS, tm, QK_PAD), lambda b, i: (b, 0, i, 0)),
        pl.BlockSpec((1, 1, N_HEADS, V_ROWS, tm), lambda b, i: (b, i, 0, 0, 0)),
    ]
    return pl.pallas_call(
        _proj_kernel, grid=grid, in_specs=in_specs, out_specs=out_specs, out_shape=out_shape,
        compiler_params=pltpu.CompilerParams(
            dimension_semantics=("arbitrary", "arbitrary"), vmem_limit_bytes=VMEM_LIMIT),
        name="proj",
    )(x, w['g_attn'], w['w_in'], w['cs'], w['g_q'], w['g_kv'], w['wqn_t'], w['wqr_t'],
      w['wuk'], w['wuv_t'], w['cos_t'], w['sin_t'], w['cos_p'], w['sin_p'])


def _dft1_kernel(m1_ref, xr_ref, xi_ref, zr_ref, zi_ref):
    nb, c = xr_ref.shape[2], xr_ref.shape[3]
    xr = xr_ref[0].reshape(DFT_N1, nb * c)
    xi = xi_ref[0].reshape(DFT_N1, nb * c)
    xc = jnp.concatenate([xr, xi], axis=0)
    z = jnp.dot(m1_ref[...], xc, preferred_element_type=F32).astype(BF16)
    zr_ref[0] = z[:DFT_N1].reshape(DFT_N1, nb, c)
    zi_ref[0] = z[DFT_N1:].reshape(DFT_N1, nb, c)


def _dft2_kernel(g_ref, zr_ref, zi_ref, o_ref):
    kb, c = zr_ref.shape[1], zr_ref.shape[3]
    ys = []
    for j in range(kb):
        zc = jnp.concatenate([zr_ref[0, j], zi_ref[0, j]], axis=0)
        ys.append(jnp.dot(g_ref[j], zc, preferred_element_type=F32).astype(BF16))
    o_ref[0] = jnp.concatenate(ys, axis=-1).reshape(DFT_N2, kb, c)


def _seq_dft(xr, xi, w):
    B, S, C = xr.shape
    assert S == DFT_N1 * DFT_N2
    nb = DFT_ROWS
    xr4 = xr.reshape(B, DFT_N1, DFT_N2, C)
    xi4 = xi.reshape(B, DFT_N1, DFT_N2, C)
    rows = pl.BlockSpec((1, DFT_N1, nb, C), lambda b, j: (b, 0, j, 0))
    zr, zi = pl.pallas_call(
        _dft1_kernel, grid=(B, DFT_N2 // nb),
        in_specs=[_const_spec((2 * DFT_N1, 2 * DFT_N1)), rows, rows],
        out_specs=[rows, rows],
        out_shape=[jax.ShapeDtypeStruct((B, DFT_N1, DFT_N2, C), BF16)] * 2,
        compiler_params=pltpu.CompilerParams(
            dimension_semantics=("arbitrary", "arbitrary"), vmem_limit_bytes=VMEM_LIMIT),
        name="dft1",
    )(w['dft_m1'], xr4, xi4)
    kb = DFT_ROWS
    blk = pl.BlockSpec((1, kb, DFT_N2, C), lambda b, k: (b, k, 0, 0))
    y = pl.pallas_call(
        _dft2_kernel, grid=(B, DFT_N1 // kb),
        in_specs=[pl.BlockSpec((kb, DFT_N2, 2 * DFT_N2), lambda b, k: (k, 0, 0)), blk, blk],
        out_specs=pl.BlockSpec((1, DFT_N2, kb, C), lambda b, k: (b, 0, k, 0)),
        out_shape=jax.ShapeDtypeStruct((B, DFT_N2, DFT_N1, C), BF16),
        compiler_params=pltpu.CompilerParams(
            dimension_semantics=("arbitrary", "arbitrary"), vmem_limit_bytes=VMEM_LIMIT),
        name="dft2",
    )(w['dft_g'], zr, zi)
    return y.reshape(B, S, C)


def _attn_kernel(q_ref, k_ref, v_ref, o_ref, s_scr, acc_ref):
    tk = v_ref.shape[-1]
    nch = v_ref.shape[1]
    nq, tq = q_ref.shape[1], q_ref.shape[-1]
    groups = nch // ATTN_SUB

    def scores(f):
        j = jnp.minimum(f // nch, nq - 1)
        c = lax.rem(f, nch)
        k_c = k_ref[0, 0, pl.ds(pl.multiple_of(c * tk, tk), tk), :]
        return jnp.dot(k_c, q_ref[0, j], preferred_element_type=F32)

    def step(f, slot, m, s_max):
        s_next = scores(f + 1)
        s_scr[1 - slot] = s_next
        next_max = jnp.max(s_next, axis=0, keepdims=True)
        s = s_scr[slot]
        m_new = jnp.maximum(m, s_max)
        alpha = jnp.exp2(m - m_new)
        p = jnp.exp2(s - m_new)
        pv = jnp.dot(v_ref[0, lax.rem(f, nch), 0], p.astype(BF16), preferred_element_type=F32)
        acc_ref[...] = alpha * acc_ref[...] + pv
        return m_new, next_max

    acc_ref[...] = jnp.zeros(acc_ref.shape, F32)
    s_first = scores(0)
    s_scr[0] = s_first
    m0 = jnp.full((1, tq), NEG_LARGE, F32)

    def body(g, carry):
        m, s_max = carry
        for si in range(ATTN_SUB):
            m, s_max = step(g * ATTN_SUB + si, si % 2, m, s_max)
        block_done = lax.rem(g, groups) == groups - 1

        @pl.when(block_done)
        def _():
            j = g // groups
            o_t = acc_ref[:V_HEAD_DIM] * (1.0 / acc_ref[V_HEAD_DIM:V_HEAD_DIM + 1])
            o_ref[0, pl.ds(pl.multiple_of(j * tq, tq), tq), :] = o_t.T.astype(BF16)
            acc_ref[...] = jnp.zeros(acc_ref.shape, F32)

        return jnp.where(block_done, m0, m), s_max

    lax.fori_loop(0, nq * groups, body, (m0, jnp.max(s_first, axis=0, keepdims=True)))


def _attention(qt, kx, vt):
    B, nqb, _, tq = qt.shape
    S = nqb * tq
    nch, tk = vt.shape[1], vt.shape[-1]
    nq = ATTN_QBLOCKS
    assert ATTN_SUB % 2 == 0 and nch % ATTN_SUB == 0 and nqb % nq == 0
    return pl.pallas_call(
        _attn_kernel, grid=(B, N_HEADS, nqb // nq),
        in_specs=[pl.BlockSpec((1, nq, QK_PAD, tq), lambda b, h, i: (b, i, h, 0)),
                  pl.BlockSpec((1, 1, S, QK_PAD), lambda b, h, i: (b, h, 0, 0)),
                  pl.BlockSpec((1, nch, 1, V_ROWS, tk), lambda b, h, i: (b, 0, h, 0, 0))],
        out_specs=pl.BlockSpec((1, nq * tq, V_HEAD_DIM), lambda b, h, i: (b, i, h)),
        out_shape=jax.ShapeDtypeStruct((B, S, MLA_WIDTH), BF16),
        scratch_shapes=[pltpu.VMEM((2, tk, tq), F32), pltpu.VMEM((V_ROWS, tq), F32)],
        compiler_params=pltpu.CompilerParams(
            dimension_semantics=("arbitrary", "arbitrary", "arbitrary"), vmem_limit_bytes=VMEM_LIMIT),
        name="attn",
    )(qt, kx, vt)


def _merge_kernel(x_ref, f_ref, a_ref, g_f_ref, g_a_ref, w_out_ref, g_ffn_ref, wr_ref, b_r_ref, tri_ref, cnt0_ref,
                  h_ref, hn_ref, route_ref, w0_ref, w1_ref, cnt_ref, carry_ref):
    tm = x_ref.shape[0]

    @pl.when(pl.program_id(0) == 0)
    def _():
        carry_ref[...] = cnt0_ref[...]

    fn = _rms(f_ref[...].astype(F32), g_f_ref[...]).astype(BF16)
    an = _rms(a_ref[...].astype(F32), g_a_ref[...]).astype(BF16)
    merged = jnp.concatenate([fn, an], axis=-1)
    h = x_ref[...] + jnp.dot(merged, w_out_ref[...], preferred_element_type=F32)
    h_ref[...] = h
    hn = _rms(h, g_ffn_ref[...])
    hn_ref[...] = hn.astype(BF16).reshape(tm, ROW_TILES, LANES)

    hn_hi = hn.astype(BF16)
    hn_lo = (hn - hn_hi.astype(F32)).astype(BF16)
    nt = (((1,), (1,)), ((), ()))
    p_t = (lax.dot_general(wr_ref[...], hn_hi, nt, preferred_element_type=F32)
           + lax.dot_general(wr_ref[...], hn_lo, nt, preferred_element_type=F32))
    reps = tm // LANES
    logits = p_t[:LANES] + p_t[LANES:] + jnp.tile(b_r_ref[...], (1, reps))

    row = lax.broadcasted_iota(jnp.int32, (EXPERTS_PER_GROUP, tm), 0)
    neg = jnp.float32(NEG_LARGE)
    gl = logits[:EXPERTS_PER_GROUP]
    gmax = jnp.max(gl, axis=0, keepdims=True)
    g_sel = jnp.min(jnp.where(gl == gmax, row, EXPERTS_PER_GROUP), axis=0, keepdims=True)
    g_p = 1.0 / jnp.sum(jnp.exp(gl - gmax), axis=0, keepdims=True)
    el = logits[EXPERTS_PER_GROUP:2 * EXPERTS_PER_GROUP]
    for g in range(1, N_EXPERT_GROUPS):
        el = jnp.where(g_sel == g, logits[(g + 1) * EXPERTS_PER_GROUP:(g + 2) * EXPERTS_PER_GROUP], el)
    m1 = jnp.max(el, axis=0, keepdims=True)
    i1 = jnp.min(jnp.where(el == m1, row, EXPERTS_PER_GROUP), axis=0, keepdims=True)
    el2 = jnp.where(row == i1, neg, el)
    m2 = jnp.max(el2, axis=0, keepdims=True)
    i2 = jnp.min(jnp.where(el2 == m2, row, EXPERTS_PER_GROUP), axis=0, keepdims=True)
    t = jnp.exp(m2 - m1)
    inv = 1.0 / (1.0 + t)
    w0 = g_p * inv
    w1 = g_p * t * inv
    e0 = g_sel * EXPERTS_PER_GROUP + i1
    e1 = g_sel * EXPERTS_PER_GROUP + i2

    row_e = lax.broadcasted_iota(jnp.int32, (N_EXPERTS, tm), 0)
    oh0 = jnp.where(row_e == e0, 1.0, 0.0)
    oh1 = jnp.where(row_e == e1, 1.0, 0.0)
    oh = oh0 + oh1
    before = (jnp.dot(oh.astype(BF16), tri_ref[...], preferred_element_type=F32)
              + jnp.tile(carry_ref[...], (1, reps)))
    r0 = jnp.sum(before * oh0, axis=0, keepdims=True)
    r1 = jnp.sum(before * oh1, axis=0, keepdims=True)
    total = carry_ref[...] + jnp.broadcast_to(jnp.sum(oh, axis=1, keepdims=True), carry_ref.shape)
    carry_ref[...] = total
    cnt_ref[...] = total

    row_r = lax.broadcasted_iota(jnp.int32, (ROUTE_ROWS, tm), 0)
    route_ref[...] = jnp.where(row_r == 0, e0.astype(F32), jnp.where(row_r == 1, e1.astype(F32),
                               jnp.where(row_r == 2, r0, jnp.where(row_r == 3, r1, 0.0))))
    row_l = lax.broadcasted_iota(jnp.int32, (LANES, tm), 0)
    w_rows = jnp.where(row_l == 0, w0, jnp.where(row_l == 1, w1, 0.0)).T
    w0_ref[...] = jnp.broadcast_to(w_rows[:, 0:1], (tm, LANES))
    w1_ref[...] = jnp.broadcast_to(w_rows[:, 1:2], (tm, LANES))


def _merge(x, f_out, a_out, w, counts0):
    T = x.shape[0]
    tm = TOKEN_TILE
    n = T // tm
    cur = lambda t: (t, 0)
    out_shape = [
        jax.ShapeDtypeStruct((T, D_MODEL), F32),
        jax.ShapeDtypeStruct((T, ROW_TILES, LANES), BF16),
        jax.ShapeDtypeStruct((ROUTE_ROWS, T), F32),
        jax.ShapeDtypeStruct((T, LANES), F32),
        jax.ShapeDtypeStruct((T, LANES), F32),
        jax.ShapeDtypeStruct((N_EXPERTS, LANES), F32),
    ]
    in_specs = [
        pl.BlockSpec((tm, D_MODEL), cur),
        pl.BlockSpec((tm, FOURIER_WIDTH), cur),
        pl.BlockSpec((tm, MLA_WIDTH), cur),
        _const_spec((1, FOURIER_WIDTH)),
        _const_spec((1, MLA_WIDTH)),
        _const_spec((D_MODEL, D_MODEL)),
        _const_spec((1, D_MODEL)),
        _const_spec((2 * LANES, D_MODEL)),
        _const_spec((LANES, LANES)),
        _const_spec((tm, tm)),
        _const_spec((N_EXPERTS, LANES)),
    ]
    out_specs = [
        pl.BlockSpec((tm, D_MODEL), cur),
        pl.BlockSpec((tm, ROW_TILES, LANES), lambda t: (t, 0, 0)),
        pl.BlockSpec((ROUTE_ROWS, tm), lambda t: (0, t)),
        pl.BlockSpec((tm, LANES), cur),
        pl.BlockSpec((tm, LANES), cur),
        pl.BlockSpec((N_EXPERTS, LANES), lambda t: (0, 0)),
    ]
    tri = (lax.broadcasted_iota(jnp.int32, (tm, tm), 0) < lax.broadcasted_iota(jnp.int32, (tm, tm), 1)).astype(BF16)
    return pl.pallas_call(
        _merge_kernel, grid=(n,), in_specs=in_specs, out_specs=out_specs, out_shape=out_shape,
        scratch_shapes=[pltpu.VMEM((N_EXPERTS, LANES), F32)],
        compiler_params=pltpu.CompilerParams(dimension_semantics=("arbitrary",), vmem_limit_bytes=VMEM_LIMIT),
        name="merge",
    )(x, f_out, a_out, w['g_f'], w['g_a'], w['w_out'], w['g_ffn'], w['wr_t'], w['b_rt'], tri, counts0)


def _dispatch_kernel(blk_end_ref, nused_ref, dest_ref, hn_a_ref, hn_b_ref, xs_ref, zero_buf, sem, zsem, *, n_a):
    i = pl.program_id(0)
    tm = hn_a_ref.shape[0]
    nb = xs_ref.shape[0] // EXPERT_TILE

    @pl.when(i == 0)
    def _():
        zero_buf[...] = jnp.zeros(zero_buf.shape, BF16)

        def zero_block(j):
            return pltpu.make_async_copy(zero_buf, xs_ref.at[pl.ds(j * EXPERT_TILE, EXPERT_TILE)], zsem)

        def each_group(fn):
            for e in range(N_EXPERTS):
                end = blk_end_ref[e]
                start = blk_end_ref[e - 1] if e else 0

                @pl.when(end > start)
                def _():
                    fn(end - 1)

        def each_tail(fn):
            def body(j, carry):
                fn(j)
                return carry

            lax.fori_loop(nused_ref[0], nb, body, 0)

        each_group(lambda j: zero_block(j).start())
        each_tail(lambda j: zero_block(j).start())
        each_group(lambda j: zero_block(j).wait())
        each_tail(lambda j: zero_block(j).wait())

    def scatter(src_ref):
        def body(r, carry):
            for k in range(TOP_K):
                d = dest_ref[0, 0, k * tm + r]
                pltpu.make_async_copy(src_ref.at[r], xs_ref.at[d], sem).start(priority=k)
            return carry

        lax.fori_loop(0, tm, body, 0)
        for _ in range(TOP_K):
            pltpu.make_async_copy(src_ref, xs_ref.at[pl.ds(0, tm)], sem).wait()

    @pl.when(i < n_a)
    def _():
        scatter(hn_a_ref)

    @pl.when(i >= n_a)
    def _():
        scatter(hn_b_ref)


def _dispatch(blk_end, n_used, dest, hn_a, hn_b, n_rows):
    tm = TOKEN_TILE
    n_a, n_b = hn_a.shape[0] // tm, hn_b.shape[0] // tm
    tile = (tm, ROW_TILES, LANES)
    grid_spec = pltpu.PrefetchScalarGridSpec(
        num_scalar_prefetch=2, grid=(n_a + n_b,),
        in_specs=[pl.BlockSpec((1, 1, TOP_K * tm), lambda i, be, nu: (i, 0, 0), memory_space=pltpu.SMEM),
                  pl.BlockSpec(tile, lambda i, be, nu: (jnp.minimum(i, n_a - 1), 0, 0)),
                  pl.BlockSpec(tile, lambda i, be, nu: (jnp.maximum(i - n_a, 0), 0, 0))],
        out_specs=pl.BlockSpec(memory_space=pl.ANY),
        scratch_shapes=[pltpu.VMEM((EXPERT_TILE, ROW_TILES, LANES), BF16),
                        pltpu.SemaphoreType.DMA(()), pltpu.SemaphoreType.DMA(())],
    )
    return pl.pallas_call(
        functools.partial(_dispatch_kernel, n_a=n_a), grid_spec=grid_spec,
        out_shape=jax.ShapeDtypeStruct((n_rows, ROW_TILES, LANES), BF16),
        compiler_params=pltpu.CompilerParams(dimension_semantics=("arbitrary",), vmem_limit_bytes=VMEM_LIMIT),
        name="dispatch",
    )(blk_end, n_used, dest, hn_a, hn_b)


def _expert_kernel(be_ref, nused_ref, blk_end_ref, xs_ref, wg_ref, wu_ref, wd_ref, y_ref,
                   wg_f, wu_f, wd_f, wgu_s, wd_s, group_ref, sems):
    i = pl.program_id(0)
    e = be_ref[i]

    def weight_copies(expert, slot):
        return [pltpu.make_async_copy(src.at[expert], dst.at[slot], sems.at[slot])
                for src, dst in ((wg_ref, wg_f), (wu_ref, wu_f), (wd_ref, wd_f))]

    @pl.when(i == 0)
    def _():
        group_ref[0] = 0
        for cp in weight_copies(e, 0):
            cp.start()

    @pl.when((i == 0) | (e != be_ref[jnp.maximum(i - 1, 0)]))
    def _():
        slot = lax.rem(group_ref[0], 2)
        for cp in weight_copies(e, slot):
            cp.wait()
        wgu_s[:, :EXPERT_DIM] = wg_f[slot].astype(BF16)
        wgu_s[:, EXPERT_DIM:] = wu_f[slot].astype(BF16)
        wd_s[...] = wd_f[slot].astype(BF16)
        nxt = blk_end_ref[e]

        @pl.when(nxt < nused_ref[0])
        def _():
            for cp in weight_copies(be_ref[nxt], 1 - slot):
                cp.start(priority=1)

        group_ref[0] = group_ref[0] + 1

    @pl.when(i < nused_ref[0])
    def _():
        tm = xs_ref.shape[0]
        x = xs_ref[...].reshape(tm, D_MODEL)
        gu = jnp.dot(x, wgu_s[...], preferred_element_type=F32)
        g = gu[:, :EXPERT_DIM]
        u = gu[:, EXPERT_DIM:]
        hmid = (g * jax.nn.sigmoid(g) * u).astype(BF16)
        y = jnp.dot(hmid, wd_s[...], preferred_element_type=F32)
        y_ref[...] = y.astype(BF16).reshape(tm, ROW_TILES, LANES)

    @pl.when(i >= nused_ref[0])
    def _():
        y_ref[...] = jnp.zeros(y_ref.shape, BF16)


def _experts(block_expert, n_used, blk_end, xs, w):
    n_rows = xs.shape[0]
    tm = EXPERT_TILE
    nb = n_rows // tm
    row = lambda i, be, nu, bend: (jnp.minimum(i, nu[0] - 1), 0, 0)
    out_row = lambda i, be, nu, bend: (i, 0, 0)
    hbm = pl.BlockSpec(memory_space=pl.ANY)
    grid_spec = pltpu.PrefetchScalarGridSpec(
        num_scalar_prefetch=3, grid=(nb,),
        in_specs=[pl.BlockSpec((tm, ROW_TILES, LANES), row), hbm, hbm, hbm],
        out_specs=pl.BlockSpec((tm, ROW_TILES, LANES), out_row),
        scratch_shapes=[pltpu.VMEM((2, D_MODEL, EXPERT_DIM), F32), pltpu.VMEM((2, D_MODEL, EXPERT_DIM), F32),
                        pltpu.VMEM((2, EXPERT_DIM, D_MODEL), F32),
                        pltpu.VMEM((D_MODEL, 2 * EXPERT_DIM), BF16), pltpu.VMEM((EXPERT_DIM, D_MODEL), BF16),
                        pltpu.SMEM((1,), jnp.int32), pltpu.SemaphoreType.DMA((2,))],
    )
    return pl.pallas_call(
        _expert_kernel, grid_spec=grid_spec,
        out_shape=jax.ShapeDtypeStruct((n_rows, ROW_TILES, LANES), BF16),
        compiler_params=pltpu.CompilerParams(dimension_semantics=("arbitrary",), vmem_limit_bytes=VMEM_LIMIT),
        name="experts",
    )(block_expert, n_used, blk_end, xs, w['w_gate'], w['w_up'], w['w_down'])


def _combine_kernel(dest_ref, dest_next_ref, h_ref, w0_ref, w1_ref, g_ref, y_ref, o_ref, ybuf, sems, *, n_steps):
    tm = h_ref.shape[0]
    i = pl.program_id(0)
    slot = lax.rem(i, 2)

    def start_gather(d_ref, s):
        def body(r, carry):
            for k in range(TOP_K):
                d = d_ref[0, 0, k * tm + r]
                pltpu.make_async_copy(y_ref.at[d], ybuf.at[s, k, r], sems.at[s]).start(priority=k)
            return carry

        lax.fori_loop(0, tm, body, 0)

    @pl.when(i == 0)
    def _():
        start_gather(dest_ref, 0)

    @pl.when(i + 1 < n_steps)
    def _():
        start_gather(dest_next_ref, 1 - slot)

    for k in range(TOP_K):
        pltpu.make_async_copy(y_ref.at[pl.ds(0, tm)], ybuf.at[slot, k], sems.at[slot]).wait()

    y0 = ybuf[slot, 0].reshape(tm, D_MODEL)
    y1 = ybuf[slot, 1].reshape(tm, D_MODEL)
    w0 = w0_ref[...]
    w1 = w1_ref[...]
    cols = []
    for j in range(ROW_TILES):
        sl = slice(j * LANES, (j + 1) * LANES)
        cols.append(h_ref[:, sl] + w0 * y0[:, sl].astype(F32) + w1 * y1[:, sl].astype(F32))
    hs = jnp.concatenate(cols, axis=-1)
    o_ref[...] = _rms(hs, g_ref[...])


def _combine(dest, h, w0, w1, g_final, y):
    T = h.shape[0]
    tm = TOKEN_TILE
    n = T // tm
    dest3 = dest
    return pl.pallas_call(
        functools.partial(_combine_kernel, n_steps=n), grid=(n,),
        in_specs=[pl.BlockSpec((1, 1, TOP_K * tm), lambda i: (i, 0, 0), memory_space=pltpu.SMEM),
                  pl.BlockSpec((1, 1, TOP_K * tm), lambda i: (jnp.minimum(i + 1, n - 1), 0, 0),
                               memory_space=pltpu.SMEM),
                  pl.BlockSpec((tm, D_MODEL), lambda i: (i, 0)),
                  pl.BlockSpec((tm, LANES), lambda i: (i, 0)),
                  pl.BlockSpec((tm, LANES), lambda i: (i, 0)),
                  _const_spec((1, D_MODEL)),
                  pl.BlockSpec(memory_space=pl.ANY)],
        out_specs=pl.BlockSpec((tm, D_MODEL), lambda i: (i, 0)),
        out_shape=jax.ShapeDtypeStruct((T, D_MODEL), F32),
        scratch_shapes=[pltpu.VMEM((2, TOP_K, tm, ROW_TILES, LANES), BF16), pltpu.SemaphoreType.DMA((2,))],
        compiler_params=pltpu.CompilerParams(dimension_semantics=("arbitrary",), vmem_limit_bytes=VMEM_LIMIT),
        name="combine",
    )(dest3, dest3, h, w0, w1, g_final, y)


def _tables(S):
    pos = jnp.arange(S, dtype=F32)
    inv_freq = 1.0 / (ROPE_THETA ** (jnp.arange(0, QK_ROPE_DIM, 2, dtype=F32) / QK_ROPE_DIM))
    ang = pos[:, None] * inv_freq[None, :]
    cos, sin = jnp.cos(ang), jnp.sin(ang)
    cos2 = jnp.concatenate([cos, cos], axis=-1)
    sin2 = jnp.concatenate([sin, sin], axis=-1)
    zp = jnp.zeros((S, LANES - QK_ROPE_DIM), F32)
    t = {
        'cos_t': cos2.T, 'sin_t': sin2.T,
        'cos_p': jnp.concatenate([cos2, zp], axis=-1), 'sin_p': jnp.concatenate([sin2, zp], axis=-1),
    }

    def phase(rows, cols, n):
        m = (rows[:, None] * cols[None, :]) % n
        a = m.astype(F32) * jnp.float32(2.0 * math.pi / n)
        return jnp.cos(a), jnp.sin(a)

    i1 = jnp.arange(DFT_N1, dtype=jnp.int32)
    c1, s1 = phase(i1, i1, DFT_N1)
    t['dft_m1'] = jnp.concatenate(
        [jnp.concatenate([c1, s1], axis=1), jnp.concatenate([-s1, c1], axis=1)], axis=0).astype(BF16)
    n2 = jnp.arange(DFT_N2, dtype=jnp.int32)
    ca, sa = phase(i1, n2, S)
    cb, sb = phase(n2, n2, DFT_N2)
    cg = ca[:, None, :] * cb[None] - sa[:, None, :] * sb[None]
    sg = sa[:, None, :] * cb[None] + ca[:, None, :] * sb[None]
    t['dft_g'] = (jnp.concatenate([cg, sg], axis=-1) * jnp.float32(S ** -0.5)).astype(BF16)
    return t


def _layer_weights(g_attn_norm, w_in, g_q_latent, w_uq, g_kv_latent, w_ukv, g_out_fourier, g_out_mla, w_out,
                   g_ffn_norm, w_router_group, b_router_group, w_router_expert, b_router_expert,
                   w_gate, w_up, w_down):
    half = QK_ROPE_DIM // 2
    s3 = FOURIER_WIDTH + Q_LORA_RANK + KV_LORA_RANK
    w_kr = w_in[:, s3:s3 + QK_ROPE_DIM]
    w_kr_rot = jnp.concatenate([-w_kr[:, half:], w_kr[:, :half]], axis=1)
    zc = jnp.zeros((D_MODEL, LANES - QK_ROPE_DIM), F32)
    w_in_ext = jnp.concatenate([w_in[:, :s3], w_kr, zc, w_kr_rot, zc], axis=1)

    c = jnp.arange(FOURIER_GROUP_DIM, dtype=jnp.int32)
    m = (c[:, None] * c[None, :]) % FOURIER_GROUP_DIM
    a = m.astype(F32) * jnp.float32(2.0 * math.pi / FOURIER_GROUP_DIM)
    cs = jnp.concatenate([jnp.cos(a), -jnp.sin(a)], axis=1) * jnp.float32(FOURIER_GROUP_DIM ** -0.5)

    qscale = jnp.float32(QK_HEAD_DIM ** -0.5 * math.log2(math.e))
    wq = (w_uq * qscale).reshape(Q_LORA_RANK, N_HEADS, QK_HEAD_DIM)
    wq_n = wq[:, :, :QK_NOPE_DIM]
    wq_r = wq[:, :, QK_NOPE_DIM:]
    to_t = lambda t: t.reshape(Q_LORA_RANK, -1).T
    wkv = w_ukv.reshape(KV_LORA_RANK, N_HEADS, QK_NOPE_DIM + V_HEAD_DIM)
    gpad = EXPERTS_PER_GROUP - N_EXPERT_GROUPS
    tail = LANES - EXPERTS_PER_GROUP - N_EXPERTS
    w_r = jnp.concatenate([w_router_group.T, jnp.zeros((gpad, D_MODEL), F32), w_router_expert.T,
                           jnp.zeros((tail, D_MODEL), F32)], axis=0)
    wr_hi = w_r.astype(BF16)
    b_r = jnp.concatenate([b_router_group, jnp.full((gpad,), NEG_LARGE, F32), b_router_expert,
                           jnp.zeros((tail,), F32)])
    return {
        'g_attn': g_attn_norm[None, :], 'w_in': w_in_ext.astype(BF16), 'cs': cs.astype(BF16),
        'g_q': g_q_latent[None, :], 'g_kv': g_kv_latent[None, :],
        'wqn_t': to_t(wq_n).astype(BF16), 'wqr_t': to_t(wq_r).astype(BF16),
        'wuk': wkv[:, :, :QK_NOPE_DIM].reshape(KV_LORA_RANK, -1).astype(BF16),
        'wuv_t': wkv[:, :, QK_NOPE_DIM:].reshape(KV_LORA_RANK, -1).T.astype(BF16),
        'g_f': g_out_fourier[None, :], 'g_a': g_out_mla[None, :], 'w_out': w_out.astype(BF16),
        'g_ffn': g_ffn_norm[None, :],
        'wr_t': jnp.concatenate([wr_hi, (w_r - wr_hi.astype(F32)).astype(BF16)], axis=0),
        'b_rt': jnp.broadcast_to(b_r[:, None], (LANES, LANES)),
        'w_gate': w_gate, 'w_up': w_up, 'w_down': w_down,
    }


def _block_plan(counts, n_assign):
    counts = counts[:, 0].astype(jnp.int32)
    nblk = (counts + EXPERT_TILE - 1) // EXPERT_TILE
    blk_end = jnp.cumsum(nblk)
    blk_start = blk_end - nblk
    n_blocks = (n_assign + N_EXPERTS * (EXPERT_TILE - 1) + EXPERT_TILE - 1) // EXPERT_TILE
    n_used = blk_end[-1]
    j = jnp.minimum(jnp.arange(n_blocks, dtype=jnp.int32), n_used - 1)
    block_expert = jnp.minimum(jnp.sum((blk_end[None, :] <= j[:, None]).astype(jnp.int32), axis=1), N_EXPERTS - 1)
    return blk_start, blk_end, block_expert, n_used[None], n_blocks * EXPERT_TILE


def _dest_rows(route, blk_start):
    T = route.shape[1]
    e = route[:TOP_K].astype(jnp.int32)
    rank = route[TOP_K:2 * TOP_K].astype(jnp.int32)
    first = jnp.zeros_like(e)
    for j in range(N_EXPERTS):
        first = jnp.where(e == j, blk_start[j], first)
    dest = (first * EXPERT_TILE + rank).reshape(TOP_K, T // TOKEN_TILE, TOKEN_TILE)
    return dest.transpose(1, 0, 2).reshape(T // TOKEN_TILE, 1, TOP_K * TOKEN_TILE)


def _mixers(x, wt, counts0):
    B, S, _ = x.shape
    T = B * S
    xr, xi, qt, kx, vt = _proj(x, wt)
    f_out = _seq_dft(xr, xi, wt)
    a_out = _attention(qt, kx, vt)
    flat = lambda t: t.reshape((T,) + t.shape[2:])
    return _merge(flat(x), flat(f_out), flat(a_out), wt, counts0)


def kernel(x_prompt, x_sample, g_attn_norm, w_in, g_q_latent, w_uq, g_kv_latent, w_ukv, g_out_fourier, g_out_mla, w_out, g_ffn_norm, w_router_group, b_router_group, w_router_expert, b_router_expert, w_gate, w_up, w_down, g_final):
    assert g_attn_norm.shape[0] == 1, "single-layer configuration"
    w = _layer_weights(g_attn_norm[0], w_in[0], g_q_latent[0], w_uq[0], g_kv_latent[0], w_ukv[0],
                       g_out_fourier[0], g_out_mla[0], w_out[0], g_ffn_norm[0], w_router_group[0],
                       b_router_group[0], w_router_expert[0], b_router_expert[0], w_gate[0], w_up[0], w_down[0])
    assert x_prompt.shape[1] == x_sample.shape[1]
    w.update(_tables(x_prompt.shape[1]))

    counts = jnp.zeros((N_EXPERTS, LANES), F32)
    batches = []
    for x in (x_prompt, x_sample):
        h, hn3, route, w0, w1, counts = _mixers(x, w, counts)
        batches.append((x.shape, h, hn3, route, w0, w1))

    n_assign = sum(b[1].shape[0] for b in batches) * TOP_K
    blk_start, blk_end, block_expert, n_used, n_rows = _block_plan(counts, n_assign)
    dests = [_dest_rows(b[3], blk_start) for b in batches]
    xs = _dispatch(blk_end, n_used, jnp.concatenate(dests, axis=0), batches[0][2], batches[1][2], n_rows)
    y = _experts(block_expert, n_used, blk_end, xs, w)
    outs = []
    for (shape, h, _, _, w0, w1), dest in zip(batches, dests):
        outs.append(_combine(dest, h, w0, w1, g_final[None, :], y).reshape(shape))
    return tuple(outs)
```

```python
import functools
import math

import jax
import jax.numpy as jnp
from jax import lax
from jax.experimental import pallas as pl
from jax.experimental.pallas import tpu as pltpu

D_MODEL = 2048
FOURIER_WIDTH = 1024
N_FOURIER_GROUPS = 4
FOURIER_GROUP_DIM = 256
MLA_WIDTH = 1024
V_HEAD_DIM = 128
N_HEADS = 8
QK_NOPE_DIM = 128
QK_ROPE_DIM = 64
QK_HEAD_DIM = 192
Q_LORA_RANK = 512
KV_LORA_RANK = 256
ROPE_THETA = 10000.0
N_EXPERT_GROUPS = 4
EXPERTS_PER_GROUP = 8
N_EXPERTS = 32
TOP_K = 2
EXPERT_DIM = 512
EPS = 1e-6

LANES = 128
BF16_SUBLANES = 16
MXU_DEPTH = 256
QK_PAD = MXU_DEPTH
V_ROWS = V_HEAD_DIM + BF16_SUBLANES
NEG_LARGE = -1e30
ROW_TILES = D_MODEL // LANES
ROUTE_ROWS = 8

DFT_N1 = 64
DFT_N2 = 128

TOKEN_TILE = 512
ATTN_TQ = 1024
ATTN_TK = 512
ATTN_SUB = 8
ATTN_QBLOCKS = 8
DFT_ROWS = 16
EXPERT_TILE = 256
VMEM_LIMIT = 56 * 1024 * 1024

F32 = jnp.float32
BF16 = jnp.bfloat16


def _const_spec(shape):
    nd = len(shape)
    return pl.BlockSpec(shape, lambda *_: (0,) * nd, pipeline_mode=pl.Buffered(1))


def _rms(x, g):
    return x * lax.rsqrt(jnp.mean(x * x, axis=-1, keepdims=True) + EPS) * g


def _proj_kernel(x_ref, g_attn_ref, w_in_ref, cs_ref, g_q_ref, g_kv_ref, wqn_ref, wqr_ref,
                 wuk_ref, wuvt_ref, cos_t_ref, sin_t_ref, cos_p_ref, sin_p_ref,
                 xr_ref, xi_ref, qt_ref, kx_ref, vt_ref):
    tm = x_ref.shape[1]
    x = x_ref[0]
    u = _rms(x, g_attn_ref[...]).astype(BF16)
    z = jnp.dot(u, w_in_ref[...], preferred_element_type=F32)

    zf = z[:, :FOURIER_WIDTH].astype(BF16)
    xr, xi = [], []
    for g in range(N_FOURIER_GROUPS):
        zg = zf[:, g * FOURIER_GROUP_DIM:(g + 1) * FOURIER_GROUP_DIM]
        xg = jnp.dot(zg, cs_ref[...], preferred_element_type=F32)
        xr.append(xg[:, :FOURIER_GROUP_DIM])
        xi.append(xg[:, FOURIER_GROUP_DIM:])
    xr_ref[0] = jnp.concatenate(xr, axis=-1).astype(BF16)
    xi_ref[0] = jnp.concatenate(xi, axis=-1).astype(BF16)

    o = FOURIER_WIDTH
    q_lat = z[:, o:o + Q_LORA_RANK]
    o += Q_LORA_RANK
    kv_lat = z[:, o:o + KV_LORA_RANK]
    o += KV_LORA_RANK
    kr = z[:, o:o + LANES]
    kr_rot = z[:, o + LANES:o + 2 * LANES]

    qn = _rms(q_lat, g_q_ref[...]).astype(BF16)
    kvn = _rms(kv_lat, g_kv_ref[...]).astype(BF16)

    nt = (((1,), (1,)), ((), ()))
    qn_t = lax.dot_general(wqn_ref[...], qn, nt, preferred_element_type=F32)
    qr_t = lax.dot_general(wqr_ref[...], qn, nt, preferred_element_type=F32)
    cos_t = cos_t_ref[...]
    sin_t = sin_t_ref[...]
    zpad = jnp.zeros((QK_PAD - QK_HEAD_DIM, tm), BF16)
    half = QK_ROPE_DIM // 2
    for h in range(N_HEADS):
        r0 = h * QK_PAD
        qt_ref[0, 0, r0:r0 + QK_NOPE_DIM, :] = qn_t[h * QK_NOPE_DIM:(h + 1) * QK_NOPE_DIM].astype(BF16)
        qr = qr_t[h * QK_ROPE_DIM:(h + 1) * QK_ROPE_DIM]
        rot = jnp.concatenate([-qr[half:], qr[:half]], axis=0)
        roped = qr * cos_t + rot * sin_t
        qt_ref[0, 0, r0 + QK_NOPE_DIM:r0 + QK_HEAD_DIM, :] = roped.astype(BF16)
        qt_ref[0, 0, r0 + QK_HEAD_DIM:r0 + QK_PAD, :] = zpad

    k_nope = jnp.dot(kvn, wuk_ref[...], preferred_element_type=F32)
    k_rope = (kr * cos_p_ref[...] + kr_rot * sin_p_ref[...]).astype(BF16)
    for h in range(N_HEADS):
        kx_ref[0, h, :, :QK_NOPE_DIM] = k_nope[:, h * QK_NOPE_DIM:(h + 1) * QK_NOPE_DIM].astype(BF16)
        kx_ref[0, h, :, QK_NOPE_DIM:] = k_rope

    v_t = lax.dot_general(wuvt_ref[...], kvn, nt, preferred_element_type=F32)
    vt_ref[0, 0, :, :V_HEAD_DIM, :] = v_t.astype(BF16).reshape(N_HEADS, V_HEAD_DIM, tm)
    vt_ref[0, 0, :, V_HEAD_DIM:, :] = jnp.ones((N_HEADS, V_ROWS - V_HEAD_DIM, tm), BF16)


def _proj(x, w):
    B, S, _ = x.shape
    tm = ATTN_TK
    q_split = ATTN_TQ // tm
    grid = (B, S // tm)
    tok = lambda b, i: (b, i, 0)
    out_shape = [
        jax.ShapeDtypeStruct((B, S, FOURIER_WIDTH), BF16),
        jax.ShapeDtypeStruct((B, S, FOURIER_WIDTH), BF16),
        jax.ShapeDtypeStruct((B, S // ATTN_TQ, N_HEADS * QK_PAD, ATTN_TQ), BF16),
        jax.ShapeDtypeStruct((B, N_HEADS, S, QK_PAD), BF16),
        jax.ShapeDtypeStruct((B, S // tm, N_HEADS, V_ROWS, tm), BF16),
    ]
    in_specs = [
        pl.BlockSpec((1, tm, D_MODEL), tok),
        _const_spec((1, D_MODEL)),
        _const_spec(w['w_in'].shape),
        _const_spec(w['cs'].shape),
        _const_spec((1, Q_LORA_RANK)),
        _const_spec((1, KV_LORA_RANK)),
        _const_spec(w['wqn_t'].shape),
        _const_spec(w['wqr_t'].shape),
        _const_spec(w['wuk'].shape),
        _const_spec(w['wuv_t'].shape),
        pl.BlockSpec((QK_ROPE_DIM, tm), lambda b, i: (0, i)),
        pl.BlockSpec((QK_ROPE_DIM, tm), lambda b, i: (0, i)),
        pl.BlockSpec((tm, LANES), lambda b, i: (i, 0)),
        pl.BlockSpec((tm, LANES), lambda b, i: (i, 0)),
    ]
    out_specs = [
        pl.BlockSpec((1, tm, FOURIER_WIDTH), tok),
        pl.BlockSpec((1, tm, FOURIER_WIDTH), tok),
        pl.BlockSpec((1, 1, N_HEADS * QK_PAD, tm), lambda b, i: (b, i // q_split, 0, i % q_split)),
        pl.BlockSpec((1, N_HEADS, tm, QK_PAD), lambda b, i: (b, 0, i, 0)),
        pl.BlockSpec((1, 1, N_HEADS, V_ROWS, tm), lambda b, i: (b, i, 0, 0, 0)),
    ]
    return pl.pallas_call(
        _proj_kernel, grid=grid, in_specs=in_specs, out_specs=out_specs, out_shape=out_shape,
        compiler_params=pltpu.CompilerParams(
            dimension_semantics=("arbitrary", "arbitrary"), vmem_limit_bytes=VMEM_LIMIT),
        name="proj",
    )(x, w['g_attn'], w['w_in'], w['cs'], w['g_q'], w['g_kv'], w['wqn_t'], w['wqr_t'],
      w['wuk'], w['wuv_t'], w['cos_t'], w['sin_t'], w['cos_p'], w['sin_p'])


def _dft1_kernel(m1_ref, xr_ref, xi_ref, zr_ref, zi_ref):
    nb, c = xr_ref.shape[2], xr_ref.shape[3]
    xr = xr_ref[0].reshape(DFT_N1, nb * c)
    xi = xi_ref[0].reshape(DFT_N1, nb * c)
    xc = jnp.concatenate([xr, xi], axis=0)
    z = jnp.dot(m1_ref[...], xc, preferred_element_type=F32).astype(BF16)
    zr_ref[0] = z[:DFT_N1].reshape(DFT_N1, nb, c)
    zi_ref[0] = z[DFT_N1:].reshape(DFT_N1, nb, c)


def _dft2_kernel(g_ref, zr_ref, zi_ref, o_ref):
    kb, c = zr_ref.shape[1], zr_ref.shape[3]
    ys = []
    for j in range(kb):
        zc = jnp.concatenate([zr_ref[0, j], zi_ref[0, j]], axis=0)
        ys.append(jnp.dot(g_ref[j], zc, preferred_element_type=F32).astype(BF16))
    o_ref[0] = jnp.concatenate(ys, axis=-1).reshape(DFT_N2, kb, c)


def _seq_dft(xr, xi, w):
    B, S, C = xr.shape
    assert S == DFT_N1 * DFT_N2
    nb = DFT_ROWS
    xr4 = xr.reshape(B, DFT_N1, DFT_N2, C)
    xi4 = xi.reshape(B, DFT_N1, DFT_N2, C)
    rows = pl.BlockSpec((1, DFT_N1, nb, C), lambda b, j: (b, 0, j, 0))
    zr, zi = pl.pallas_call(
        _dft1_kernel, grid=(B, DFT_N2 // nb),
        in_specs=[_const_spec((2 * DFT_N1, 2 * DFT_N1)), rows, rows],
        out_specs=[rows, rows],
        out_shape=[jax.ShapeDtypeStruct((B, DFT_N1, DFT_N2, C), BF16)] * 2,
        compiler_params=pltpu.CompilerParams(
            dimension_semantics=("arbitrary", "arbitrary"), vmem_limit_bytes=VMEM_LIMIT),
        name="dft1",
    )(w['dft_m1'], xr4, xi4)
    kb = DFT_ROWS
    blk = pl.BlockSpec((1, kb, DFT_N2, C), lambda b, k: (b, k, 0, 0))
    y = pl.pallas_call(
        _dft2_kernel, grid=(B, DFT_N1 // kb),
        in_specs=[pl.BlockSpec((kb, DFT_N2, 2 * DFT_N2), lambda b, k: (k, 0, 0)), blk, blk],
        out_specs=pl.BlockSpec((1, DFT_N2, kb, C), lambda b, k: (b, 0, k, 0)),
        out_shape=jax.ShapeDtypeStruct((B, DFT_N2, DFT_N1, C), BF16),
        compiler_params=pltpu.CompilerParams(
            dimension_semantics=("arbitrary", "arbitrary"), vmem_limit_bytes=VMEM_LIMIT),
        name="dft2",
    )(w['dft_g'], zr, zi)
    return y.reshape(B, S, C)


def _attn_kernel(q_ref, k_ref, v_ref, o_ref, s_scr, acc_ref):
    tk = v_ref.shape[-1]
    nch = v_ref.shape[1]
    nq, tq = q_ref.shape[1], q_ref.shape[-1]
    groups = nch // ATTN_SUB

    def scores(f):
        j = jnp.minimum(f // nch, nq - 1)
        c = lax.rem(f, nch)
        k_c = k_ref[0, 0, pl.ds(pl.multiple_of(c * tk, tk), tk), :]
        return jnp.dot(k_c, q_ref[0, j], preferred_element_type=F32)

    def step(f, slot, m, s_max):
        s_next = scores(f + 1)
        s_scr[1 - slot] = s_next
        next_max = jnp.max(s_next, axis=0, keepdims=True)
        s = s_scr[slot]
        m_new = jnp.maximum(m, s_max)
        alpha = jnp.exp2(m - m_new)
        p = jnp.exp2(s - m_new)
        pv = jnp.dot(v_ref[0, lax.rem(f, nch), 0], p.astype(BF16), preferred_element_type=F32)
        acc_ref[...] = alpha * acc_ref[...] + pv
        return m_new, next_max

    acc_ref[...] = jnp.zeros(acc_ref.shape, F32)
    s_first = scores(0)
    s_scr[0] = s_first
    m0 = jnp.full((1, tq), NEG_LARGE, F32)

    def body(g, carry):
        m, s_max = carry
        for si in range(ATTN_SUB):
            m, s_max = step(g * ATTN_SUB + si, si % 2, m, s_max)
        block_done = lax.rem(g, groups) == groups - 1

        @pl.when(block_done)
        def _():
            j = g // groups
            o_t = acc_ref[:V_HEAD_DIM] * (1.0 / acc_ref[V_HEAD_DIM:V_HEAD_DIM + 1])
            o_ref[0, pl.ds(pl.multiple_of(j * tq, tq), tq), :] = o_t.T.astype(BF16)
            acc_ref[...] = jnp.zeros(acc_ref.shape, F32)

        return jnp.where(block_done, m0, m), s_max

    lax.fori_loop(0, nq * groups, body, (m0, jnp.max(s_first, axis=0, keepdims=True)))


def _attention(qt, kx, vt):
    B, nqb, _, tq = qt.shape
    S = nqb * tq
    nch, tk = vt.shape[1], vt.shape[-1]
    nq = ATTN_QBLOCKS
    assert ATTN_SUB % 2 == 0 and nch % ATTN_SUB == 0 and nqb % nq == 0
    return pl.pallas_call(
        _attn_kernel, grid=(B, N_HEADS, nqb // nq),
        in_specs=[pl.BlockSpec((1, nq, QK_PAD, tq), lambda b, h, i: (b, i, h, 0)),
                  pl.BlockSpec((1, 1, S, QK_PAD), lambda b, h, i: (b, h, 0, 0)),
                  pl.BlockSpec((1, nch, 1, V_ROWS, tk), lambda b, h, i: (b, 0, h, 0, 0))],
        out_specs=pl.BlockSpec((1, nq * tq, V_HEAD_DIM), lambda b, h, i: (b, i, h)),
        out_shape=jax.ShapeDtypeStruct((B, S, MLA_WIDTH), BF16),
        scratch_shapes=[pltpu.VMEM((2, tk, tq), F32), pltpu.VMEM((V_ROWS, tq), F32)],
        compiler_params=pltpu.CompilerParams(
            dimension_semantics=("arbitrary", "arbitrary", "arbitrary"), vmem_limit_bytes=VMEM_LIMIT),
        name="attn",
    )(qt, kx, vt)


def _merge_kernel(x_ref, f_ref, a_ref, g_f_ref, g_a_ref, w_out_ref, g_ffn_ref, wr_ref, b_r_ref, tri_ref, cnt0_ref,
                  h_ref, hn_ref, route_ref, w0_ref, w1_ref, cnt_ref, carry_ref):
    tm = x_ref.shape[0]

    @pl.when(pl.program_id(0) == 0)
    def _():
        carry_ref[...] = cnt0_ref[...]

    fn = _rms(f_ref[...].astype(F32), g_f_ref[...]).astype(BF16)
    an = _rms(a_ref[...].astype(F32), g_a_ref[...]).astype(BF16)
    merged = jnp.concatenate([fn, an], axis=-1)
    h = x_ref[...] + jnp.dot(merged, w_out_ref[...], preferred_element_type=F32)
    h_ref[...] = h
    hn = _rms(h, g_ffn_ref[...])
    hn_ref[...] = hn.astype(BF16).reshape(tm, ROW_TILES, LANES)

    hn_hi = hn.astype(BF16)
    hn_lo = (hn - hn_hi.astype(F32)).astype(BF16)
    nt = (((1,), (1,)), ((), ()))
    p_t = (lax.dot_general(wr_ref[...], hn_hi, nt, preferred_element_type=F32)
           + lax.dot_general(wr_ref[...], hn_lo, nt, preferred_element_type=F32))
    reps = tm // LANES
    logits = p_t[:LANES] + p_t[LANES:] + jnp.tile(b_r_ref[...], (1, reps))

    row = lax.broadcasted_iota(jnp.int32, (EXPERTS_PER_GROUP, tm), 0)
    neg = jnp.float32(NEG_LARGE)
    gl = logits[:EXPERTS_PER_GROUP]
    gmax = jnp.max(gl, axis=0, keepdims=True)
    g_sel = jnp.min(jnp.where(gl == gmax, row, EXPERTS_PER_GROUP), axis=0, keepdims=True)
    g_p = 1.0 / jnp.sum(jnp.exp(gl - gmax), axis=0, keepdims=True)
    el = logits[EXPERTS_PER_GROUP:2 * EXPERTS_PER_GROUP]
    for g in range(1, N_EXPERT_GROUPS):
        el = jnp.where(g_sel == g, logits[(g + 1) * EXPERTS_PER_GROUP:(g + 2) * EXPERTS_PER_GROUP], el)
    m1 = jnp.max(el, axis=0, keepdims=True)
    i1 = jnp.min(jnp.where(el == m1, row, EXPERTS_PER_GROUP), axis=0, keepdims=True)
    el2 = jnp.where(row == i1, neg, el)
    m2 = jnp.max(el2, axis=0, keepdims=True)
    i2 = jnp.min(jnp.where(el2 == m2, row, EXPERTS_PER_GROUP), axis=0, keepdims=True)
    t = jnp.exp(m2 - m1)
    inv = 1.0 / (1.0 + t)
    w0 = g_p * inv
    w1 = g_p * t * inv
    e0 = g_sel * EXPERTS_PER_GROUP + i1
    e1 = g_sel * EXPERTS_PER_GROUP + i2

    row_e = lax.broadcasted_iota(jnp.int32, (N_EXPERTS, tm), 0)
    oh0 = jnp.where(row_e == e0, 1.0, 0.0)
    oh1 = jnp.where(row_e == e1, 1.0, 0.0)
    oh = oh0 + oh1
    before = (jnp.dot(oh.astype(BF16), tri_ref[...], preferred_element_type=F32)
              + jnp.tile(carry_ref[...], (1, reps)))
    r0 = jnp.sum(before * oh0, axis=0, keepdims=True)
    r1 = jnp.sum(before * oh1, axis=0, keepdims=True)
    total = carry_ref[...] + jnp.broadcast_to(jnp.sum(oh, axis=1, keepdims=True), carry_ref.shape)
    carry_ref[...] = total
    cnt_ref[...] = total

    row_r = lax.broadcasted_iota(jnp.int32, (ROUTE_ROWS, tm), 0)
    route_ref[...] = jnp.where(row_r == 0, e0.astype(F32), jnp.where(row_r == 1, e1.astype(F32),
                               jnp.where(row_r == 2, r0, jnp.where(row_r == 3, r1, 0.0))))
    row_l = lax.broadcasted_iota(jnp.int32, (LANES, tm), 0)
    w_rows = jnp.where(row_l == 0, w0, jnp.where(row_l == 1, w1, 0.0)).T
    w0_ref[...] = jnp.broadcast_to(w_rows[:, 0:1], (tm, LANES))
    w1_ref[...] = jnp.broadcast_to(w_rows[:, 1:2], (tm, LANES))


def _merge(x, f_out, a_out, w, counts0):
    T = x.shape[0]
    tm = TOKEN_TILE
    n = T // tm
    cur = lambda t: (t, 0)
    out_shape = [
        jax.ShapeDtypeStruct((T, D_MODEL), F32),
        jax.ShapeDtypeStruct((T, ROW_TILES, LANES), BF16),
        jax.ShapeDtypeStruct((ROUTE_ROWS, T), F32),
        jax.ShapeDtypeStruct((T, LANES), F32),
        jax.ShapeDtypeStruct((T, LANES), F32),
        jax.ShapeDtypeStruct((N_EXPERTS, LANES), F32),
    ]
    in_specs = [
        pl.BlockSpec((tm, D_MODEL), cur),
        pl.BlockSpec((tm, FOURIER_WIDTH), cur),
        pl.BlockSpec((tm, MLA_WIDTH), cur),
        _const_spec((1, FOURIER_WIDTH)),
        _const_spec((1, MLA_WIDTH)),
        _const_spec((D_MODEL, D_MODEL)),
        _const_spec((1, D_MODEL)),
        _const_spec((2 * LANES, D_MODEL)),
        _const_spec((LANES, LANES)),
        _const_spec((tm, tm)),
        _const_spec((N_EXPERTS, LANES)),
    ]
    out_specs = [
        pl.BlockSpec((tm, D_MODEL), cur),
        pl.BlockSpec((tm, ROW_TILES, LANES), lambda t: (t, 0, 0)),
        pl.BlockSpec((ROUTE_ROWS, tm), lambda t: (0, t)),
        pl.BlockSpec((tm, LANES), cur),
        pl.BlockSpec((tm, LANES), cur),
        pl.BlockSpec((N_EXPERTS, LANES), lambda t: (0, 0)),
    ]
    tri = (lax.broadcasted_iota(jnp.int32, (tm, tm), 0) < lax.broadcasted_iota(jnp.int32, (tm, tm), 1)).astype(BF16)
    return pl.pallas_call(
        _merge_kernel, grid=(n,), in_specs=in_specs, out_specs=out_specs, out_shape=out_shape,
        scratch_shapes=[pltpu.VMEM((N_EXPERTS, LANES), F32)],
        compiler_params=pltpu.CompilerParams(dimension_semantics=("arbitrary",), vmem_limit_bytes=VMEM_LIMIT),
        name="merge",
    )(x, f_out, a_out, w['g_f'], w['g_a'], w['w_out'], w['g_ffn'], w['wr_t'], w['b_rt'], tri, counts0)


def _dispatch_kernel(blk_end_ref, nused_ref, dest_ref, hn_a_ref, hn_b_ref, xs_ref, ring, zero_buf, in_sems, out_sems,
                     zsem, *, n_a, n_steps):
    i = pl.program_id(0)
    tm = ring.shape[1]
    nb = xs_ref.shape[0] // EXPERT_TILE
    slot = lax.rem(i, 3)

    def load(t):
        s = lax.rem(t, 3)

        @pl.when(t < n_a)
        def _():
            src = hn_a_ref.at[pl.ds(pl.multiple_of(t * tm, tm), tm)]
            pltpu.make_async_copy(src, ring.at[s], in_sems.at[s]).start()

        @pl.when(t >= n_a)
        def _():
            src = hn_b_ref.at[pl.ds(pl.multiple_of((t - n_a) * tm, tm), tm)]
            pltpu.make_async_copy(src, ring.at[s], in_sems.at[s]).start()

    def drain(s):
        for _ in range(TOP_K):
            pltpu.make_async_copy(ring.at[s], xs_ref.at[pl.ds(0, tm)], out_sems.at[s]).wait()

    @pl.when(i == 0)
    def _():
        zero_buf[...] = jnp.zeros(zero_buf.shape, BF16)

        def zero_block(j):
            return pltpu.make_async_copy(zero_buf, xs_ref.at[pl.ds(j * EXPERT_TILE, EXPERT_TILE)], zsem)

        def each_group(fn):
            for e in range(N_EXPERTS):
                end = blk_end_ref[e]
                start = blk_end_ref[e - 1] if e else 0

                @pl.when(end > start)
                def _():
                    fn(end - 1)

        def each_tail(fn):
            def body(j, carry):
                fn(j)
                return carry

            lax.fori_loop(nused_ref[0], nb, body, 0)

        each_group(lambda j: zero_block(j).start())
        each_tail(lambda j: zero_block(j).start())
        each_group(lambda j: zero_block(j).wait())
        each_tail(lambda j: zero_block(j).wait())
        load(i)

    @pl.when(i + 1 < n_steps)
    def _():
        load(i + 1)

    pltpu.make_async_copy(hn_a_ref.at[pl.ds(0, tm)], ring.at[slot], in_sems.at[slot]).wait()

    def body(r, carry):
        for k in range(TOP_K):
            d = dest_ref[0, 0, k * tm + r]
            pltpu.make_async_copy(ring.at[slot, r], xs_ref.at[d], out_sems.at[slot]).start(priority=k)
        return carry

    lax.fori_loop(0, tm, body, 0)

    @pl.when(i > 0)
    def _():
        drain(lax.rem(i + 2, 3))

    @pl.when(i == n_steps - 1)
    def _():
        drain(slot)


def _dispatch(blk_end, n_used, dest, hn_a, hn_b, n_rows):
    tm = TOKEN_TILE
    n_a, n_b = hn_a.shape[0] // tm, hn_b.shape[0] // tm
    hbm = pl.BlockSpec(memory_space=pl.ANY)
    grid_spec = pltpu.PrefetchScalarGridSpec(
        num_scalar_prefetch=2, grid=(n_a + n_b,),
        in_specs=[pl.BlockSpec((1, 1, TOP_K * tm), lambda i, be, nu: (i, 0, 0), memory_space=pltpu.SMEM),
                  hbm, hbm],
        out_specs=hbm,
        scratch_shapes=[pltpu.VMEM((3, tm, ROW_TILES, LANES), BF16),
                        pltpu.VMEM((EXPERT_TILE, ROW_TILES, LANES), BF16),
                        pltpu.SemaphoreType.DMA((3,)), pltpu.SemaphoreType.DMA((3,)), pltpu.SemaphoreType.DMA(())],
    )
    return pl.pallas_call(
        functools.partial(_dispatch_kernel, n_a=n_a, n_steps=n_a + n_b), grid_spec=grid_spec,
        out_shape=jax.ShapeDtypeStruct((n_rows, ROW_TILES, LANES), BF16),
        compiler_params=pltpu.CompilerParams(dimension_semantics=("arbitrary",), vmem_limit_bytes=VMEM_LIMIT),
        name="dispatch",
    )(blk_end, n_used, dest, hn_a, hn_b)


def _expert_kernel(be_ref, nused_ref, blk_end_ref, xs_ref, wg_ref, wu_ref, wd_ref, y_ref,
                   wg_f, wu_f, wd_f, wgu_s, wd_s, group_ref, sems):
    i = pl.program_id(0)
    e = be_ref[i]

    def weight_copies(expert, slot):
        return [pltpu.make_async_copy(src.at[expert], dst.at[slot], sems.at[slot])
                for src, dst in ((wg_ref, wg_f), (wu_ref, wu_f), (wd_ref, wd_f))]

    @pl.when(i == 0)
    def _():
        group_ref[0] = 0
        for cp in weight_copies(e, 0):
            cp.start()

    @pl.when((i == 0) | (e != be_ref[jnp.maximum(i - 1, 0)]))
    def _():
        slot = lax.rem(group_ref[0], 2)
        for cp in weight_copies(e, slot):
            cp.wait()
        wgu_s[:, :EXPERT_DIM] = wg_f[slot].astype(BF16)
        wgu_s[:, EXPERT_DIM:] = wu_f[slot].astype(BF16)
        wd_s[...] = wd_f[slot].astype(BF16)
        nxt = blk_end_ref[e]

        @pl.when(nxt < nused_ref[0])
        def _():
            for cp in weight_copies(be_ref[nxt], 1 - slot):
                cp.start(priority=1)

        group_ref[0] = group_ref[0] + 1

    @pl.when(i < nused_ref[0])
    def _():
        tm = xs_ref.shape[0]
        x = xs_ref[...].reshape(tm, D_MODEL)
        gu = jnp.dot(x, wgu_s[...], preferred_element_type=F32)
        g = gu[:, :EXPERT_DIM]
        u = gu[:, EXPERT_DIM:]
        hmid = (g * jax.nn.sigmoid(g) * u).astype(BF16)
        y = jnp.dot(hmid, wd_s[...], preferred_element_type=F32)
        y_ref[...] = y.astype(BF16).reshape(tm, ROW_TILES, LANES)

    @pl.when(i >= nused_ref[0])
    def _():
        y_ref[...] = jnp.zeros(y_ref.shape, BF16)


def _experts(block_expert, n_used, blk_end, xs, w):
    n_rows = xs.shape[0]
    tm = EXPERT_TILE
    nb = n_rows // tm
    row = lambda i, be, nu, bend: (jnp.minimum(i, nu[0] - 1), 0, 0)
    out_row = lambda i, be, nu, bend: (i, 0, 0)
    hbm = pl.BlockSpec(memory_space=pl.ANY)
    grid_spec = pltpu.PrefetchScalarGridSpec(
        num_scalar_prefetch=3, grid=(nb,),
        in_specs=[pl.BlockSpec((tm, ROW_TILES, LANES), row), hbm, hbm, hbm],
        out_specs=pl.BlockSpec((tm, ROW_TILES, LANES), out_row),
        scratch_shapes=[pltpu.VMEM((2, D_MODEL, EXPERT_DIM), F32), pltpu.VMEM((2, D_MODEL, EXPERT_DIM), F32),
                        pltpu.VMEM((2, EXPERT_DIM, D_MODEL), F32),
                        pltpu.VMEM((D_MODEL, 2 * EXPERT_DIM), BF16), pltpu.VMEM((EXPERT_DIM, D_MODEL), BF16),
                        pltpu.SMEM((1,), jnp.int32), pltpu.SemaphoreType.DMA((2,))],
    )
    return pl.pallas_call(
        _expert_kernel, grid_spec=grid_spec,
        out_shape=jax.ShapeDtypeStruct((n_rows, ROW_TILES, LANES), BF16),
        compiler_params=pltpu.CompilerParams(dimension_semantics=("arbitrary",), vmem_limit_bytes=VMEM_LIMIT),
        name="experts",
    )(block_expert, n_used, blk_end, xs, w['w_gate'], w['w_up'], w['w_down'])


def _combine_kernel(dest_ref, dest_next_ref, h_ref, w0_ref, w1_ref, g_ref, y_ref, o_ref, ybuf, sems, *, n_steps):
    tm = h_ref.shape[0]
    i = pl.program_id(0)
    slot = lax.rem(i, 2)

    def start_gather(d_ref, s):
        def body(r, carry):
            for k in range(TOP_K):
                d = d_ref[0, 0, k * tm + r]
                pltpu.make_async_copy(y_ref.at[d], ybuf.at[s, k, r], sems.at[s]).start(priority=k)
            return carry

        lax.fori_loop(0, tm, body, 0)

    @pl.when(i == 0)
    def _():
        start_gather(dest_ref, 0)

    @pl.when(i + 1 < n_steps)
    def _():
        start_gather(dest_next_ref, 1 - slot)

    for k in range(TOP_K):
        pltpu.make_async_copy(y_ref.at[pl.ds(0, tm)], ybuf.at[slot, k], sems.at[slot]).wait()

    y0 = ybuf[slot, 0].reshape(tm, D_MODEL)
    y1 = ybuf[slot, 1].reshape(tm, D_MODEL)
    w0 = w0_ref[...]
    w1 = w1_ref[...]
    cols = []
    for j in range(ROW_TILES):
        sl = slice(j * LANES, (j + 1) * LANES)
        cols.append(h_ref[:, sl] + w0 * y0[:, sl].astype(F32) + w1 * y1[:, sl].astype(F32))
    hs = jnp.concatenate(cols, axis=-1)
    o_ref[...] = _rms(hs, g_ref[...])


def _combine(dest, h, w0, w1, g_final, y):
    T = h.shape[0]
    tm = TOKEN_TILE
    n = T // tm
    dest3 = dest
    return pl.pallas_call(
        functools.partial(_combine_kernel, n_steps=n), grid=(n,),
        in_specs=[pl.BlockSpec((1, 1, TOP_K * tm), lambda i: (i, 0, 0), memory_space=pltpu.SMEM),
                  pl.BlockSpec((1, 1, TOP_K * tm), lambda i: (jnp.minimum(i + 1, n - 1), 0, 0),
                               memory_space=pltpu.SMEM),
                  pl.BlockSpec((tm, D_MODEL), lambda i: (i, 0)),
                  pl.BlockSpec((tm, LANES), lambda i: (i, 0)),
                  pl.BlockSpec((tm, LANES), lambda i: (i, 0)),
                  _const_spec((1, D_MODEL)),
                  pl.BlockSpec(memory_space=pl.ANY)],
        out_specs=pl.BlockSpec((tm, D_MODEL), lambda i: (i, 0)),
        out_shape=jax.ShapeDtypeStruct((T, D_MODEL), F32),
        scratch_shapes=[pltpu.VMEM((2, TOP_K, tm, ROW_TILES, LANES), BF16), pltpu.SemaphoreType.DMA((2,))],
        compiler_params=pltpu.CompilerParams(dimension_semantics=("arbitrary",), vmem_limit_bytes=VMEM_LIMIT),
        name="combine",
    )(dest3, dest3, h, w0, w1, g_final, y)


def _tables(S):
    pos = jnp.arange(S, dtype=F32)
    inv_freq = 1.0 / (ROPE_THETA ** (jnp.arange(0, QK_ROPE_DIM, 2, dtype=F32) / QK_ROPE_DIM))
    ang = pos[:, None] * inv_freq[None, :]
    cos, sin = jnp.cos(ang), jnp.sin(ang)
    cos2 = jnp.concatenate([cos, cos], axis=-1)
    sin2 = jnp.concatenate([sin, sin], axis=-1)
    zp = jnp.zeros((S, LANES - QK_ROPE_DIM), F32)
    t = {
        'cos_t': cos2.T, 'sin_t': sin2.T,
        'cos_p': jnp.concatenate([cos2, zp], axis=-1), 'sin_p': jnp.concatenate([sin2, zp], axis=-1),
    }

    def phase(rows, cols, n):
        m = (rows[:, None] * cols[None, :]) % n
        a = m.astype(F32) * jnp.float32(2.0 * math.pi / n)
        return jnp.cos(a), jnp.sin(a)

    i1 = jnp.arange(DFT_N1, dtype=jnp.int32)
    c1, s1 = phase(i1, i1, DFT_N1)
    t['dft_m1'] = jnp.concatenate(
        [jnp.concatenate([c1, s1], axis=1), jnp.concatenate([-s1, c1], axis=1)], axis=0).astype(BF16)
    k = jnp.arange(S, dtype=jnp.int32)
    n2 = jnp.arange(DFT_N2, dtype=jnp.int32)
    cg, sg = phase(k, n2, S)
    g = jnp.concatenate([cg, sg], axis=1) * jnp.float32(S ** -0.5)
    t['dft_g'] = g.reshape(DFT_N2, DFT_N1, 2 * DFT_N2).transpose(1, 0, 2).astype(BF16)
    return t


def _layer_weights(g_attn_norm, w_in, g_q_latent, w_uq, g_kv_latent, w_ukv, g_out_fourier, g_out_mla, w_out,
                   g_ffn_norm, w_router_group, b_router_group, w_router_expert, b_router_expert,
                   w_gate, w_up, w_down):
    half = QK_ROPE_DIM // 2
    s3 = FOURIER_WIDTH + Q_LORA_RANK + KV_LORA_RANK
    w_kr = w_in[:, s3:s3 + QK_ROPE_DIM]
    w_kr_rot = jnp.concatenate([-w_kr[:, half:], w_kr[:, :half]], axis=1)
    zc = jnp.zeros((D_MODEL, LANES - QK_ROPE_DIM), F32)
    w_in_ext = jnp.concatenate([w_in[:, :s3], w_kr, zc, w_kr_rot, zc], axis=1)

    c = jnp.arange(FOURIER_GROUP_DIM, dtype=jnp.int32)
    m = (c[:, None] * c[None, :]) % FOURIER_GROUP_DIM
    a = m.astype(F32) * jnp.float32(2.0 * math.pi / FOURIER_GROUP_DIM)
    cs = jnp.concatenate([jnp.cos(a), -jnp.sin(a)], axis=1) * jnp.float32(FOURIER_GROUP_DIM ** -0.5)

    qscale = jnp.float32(QK_HEAD_DIM ** -0.5 * math.log2(math.e))
    wq = (w_uq * qscale).reshape(Q_LORA_RANK, N_HEADS, QK_HEAD_DIM)
    wq_n = wq[:, :, :QK_NOPE_DIM]
    wq_r = wq[:, :, QK_NOPE_DIM:]
    to_t = lambda t: t.reshape(Q_LORA_RANK, -1).T
    wkv = w_ukv.reshape(KV_LORA_RANK, N_HEADS, QK_NOPE_DIM + V_HEAD_DIM)
    gpad = EXPERTS_PER_GROUP - N_EXPERT_GROUPS
    tail = LANES - EXPERTS_PER_GROUP - N_EXPERTS
    w_r = jnp.concatenate([w_router_group.T, jnp.zeros((gpad, D_MODEL), F32), w_router_expert.T,
                           jnp.zeros((tail, D_MODEL), F32)], axis=0)
    wr_hi = w_r.astype(BF16)
    b_r = jnp.concatenate([b_router_group, jnp.full((gpad,), NEG_LARGE, F32), b_router_expert,
                           jnp.zeros((tail,), F32)])
    return {
        'g_attn': g_attn_norm[None, :], 'w_in': w_in_ext.astype(BF16), 'cs': cs.astype(BF16),
        'g_q': g_q_latent[None, :], 'g_kv': g_kv_latent[None, :],
        'wqn_t': to_t(wq_n).astype(BF16), 'wqr_t': to_t(wq_r).astype(BF16),
        'wuk': wkv[:, :, :QK_NOPE_DIM].reshape(KV_LORA_RANK, -1).astype(BF16),
        'wuv_t': wkv[:, :, QK_NOPE_DIM:].reshape(KV_LORA_RANK, -1).T.astype(BF16),
        'g_f': g_out_fourier[None, :], 'g_a': g_out_mla[None, :], 'w_out': w_out.astype(BF16),
        'g_ffn': g_ffn_norm[None, :],
        'wr_t': jnp.concatenate([wr_hi, (w_r - wr_hi.astype(F32)).astype(BF16)], axis=0),
        'b_rt': jnp.broadcast_to(b_r[:, None], (LANES, LANES)),
        'w_gate': w_gate, 'w_up': w_up, 'w_down': w_down,
    }


def _block_plan(counts, n_assign):
    counts = counts[:, 0].astype(jnp.int32)
    nblk = (counts + EXPERT_TILE - 1) // EXPERT_TILE
    blk_end = jnp.cumsum(nblk)
    blk_start = blk_end - nblk
    n_blocks = (n_assign + N_EXPERTS * (EXPERT_TILE - 1) + EXPERT_TILE - 1) // EXPERT_TILE
    n_used = blk_end[-1]
    j = jnp.minimum(jnp.arange(n_blocks, dtype=jnp.int32), n_used - 1)
    block_expert = jnp.minimum(jnp.sum((blk_end[None, :] <= j[:, None]).astype(jnp.int32), axis=1), N_EXPERTS - 1)
    return blk_start, blk_end, block_expert, n_used[None], n_blocks * EXPERT_TILE


def _dest_rows(route, blk_start):
    T = route.shape[1]
    e = route[:TOP_K].astype(jnp.int32)
    rank = route[TOP_K:2 * TOP_K].astype(jnp.int32)
    first = jnp.zeros_like(e)
    for j in range(N_EXPERTS):
        first = jnp.where(e == j, blk_start[j], first)
    dest = (first * EXPERT_TILE + rank).reshape(TOP_K, T // TOKEN_TILE, TOKEN_TILE)
    return dest.transpose(1, 0, 2).reshape(T // TOKEN_TILE, 1, TOP_K * TOKEN_TILE)


def _mixers(x, wt, counts0):
    B, S, _ = x.shape
    T = B * S
    xr, xi, qt, kx, vt = _proj(x, wt)
    f_out = _seq_dft(xr, xi, wt)
    a_out = _attention(qt, kx, vt)
    flat = lambda t: t.reshape((T,) + t.shape[2:])
    return _merge(flat(x), flat(f_out), flat(a_out), wt, counts0)


def kernel(x_prompt, x_sample, g_attn_norm, w_in, g_q_latent, w_uq, g_kv_latent, w_ukv, g_out_fourier, g_out_mla, w_out, g_ffn_norm, w_router_group, b_router_group, w_router_expert, b_router_expert, w_gate, w_up, w_down, g_final):
    assert g_attn_norm.shape[0] == 1, "single-layer configuration"
    w = _layer_weights(g_attn_norm[0], w_in[0], g_q_latent[0], w_uq[0], g_kv_latent[0], w_ukv[0],
                       g_out_fourier[0], g_out_mla[0], w_out[0], g_ffn_norm[0], w_router_group[0],
                       b_router_group[0], w_router_expert[0], b_router_expert[0], w_gate[0], w_up[0], w_down[0])
    assert x_prompt.shape[1] == x_sample.shape[1]
    w.update(_tables(x_prompt.shape[1]))

    counts = jnp.zeros((N_EXPERTS, LANES), F32)
    batches = []
    for x in (x_prompt, x_sample):
        h, hn3, route, w0, w1, counts = _mixers(x, w, counts)
        batches.append((x.shape, h, hn3, route, w0, w1))

    n_assign = sum(b[1].shape[0] for b in batches) * TOP_K
    blk_start, blk_end, block_expert, n_used, n_rows = _block_plan(counts, n_assign)
    dests = [_dest_rows(b[3], blk_start) for b in batches]
    xs = _dispatch(blk_end, n_used, jnp.concatenate(dests, axis=0), batches[0][2], batches[1][2], n_rows)
    y = _experts(block_expert, n_used, blk_end, xs, w)
    outs = []
    for (shape, h, _, _, w0, w1), dest in zip(batches, dests):
        outs.append(_combine(dest, h, w0, w1, g_final[None, :], y).reshape(shape))
    return tuple(outs)
```
